```python
import jax, jax.numpy as jnp
from jax import lax
import numpy as np

D_MODEL = 2048
BATCH = 4
SEQ = 4096
DEPTH = 2

N_MEM = 256
EPS = 1e-6
N_EVEN = (DEPTH + 1) // 2
N_ODD = DEPTH // 2
MIX_A = D_MODEL // 2
POOL_WINDOWS = (2, 4, 8, 16)
N_POOL_GROUPS = len(POOL_WINDOWS)
POOL_GROUP = MIX_A // N_POOL_GROUPS
MIX_B = D_MODEL - MIX_A
HG_HEAD = 128
HG_HEADS = MIX_B // HG_HEAD
HG_CHUNK = 64
IN_EVEN = MIX_A + 4 * MIX_B
FOX_HEAD = 128
FOX_HEADS = D_MODEL // FOX_HEAD
FOX_BLOCK = 128
IN_ODD = 3 * D_MODEL + FOX_HEADS
XA_HEADS = 4
XA_HEAD = D_MODEL // XA_HEADS
D_FF = -(-8 * D_MODEL // (3 * 256)) * 256

kernel_name = "hybrid_pool_hgrn2_fox_trunk"


def rmsnorm(x, g):
    xf = x.astype(jnp.float32)
    y = xf * lax.rsqrt(jnp.mean(xf * xf, axis=-1, keepdims=True) + EPS)
    return (y * g.astype(jnp.float32)).astype(x.dtype)


def pool_mixer(u, w_pool, pool_scale):
    B, T, _ = u.shape
    uf = u.astype(jnp.float32)
    c = jnp.pad(jnp.cumsum(uf, axis=1), ((0, 0), (1, 0), (0, 0)))
    t = jnp.arange(T)
    outs = []
    for gi, w in enumerate(POOL_WINDOWS):
        cg = c[:, :, gi * POOL_GROUP:(gi + 1) * POOL_GROUP]
        c_lag = jnp.pad(cg, ((0, 0), (w - 1, 0), (0, 0)))[:, :T]
        cnt = jnp.minimum(t + 1, w).astype(jnp.float32)[None, :, None]
        mean = (cg[:, 1:] - c_lag) / cnt
        outs.append(mean - uf[:, :, gi * POOL_GROUP:(gi + 1) * POOL_GROUP])
    p = jnp.stack(outs, axis=2)
    y = jnp.einsum('btgc,gcd->btgd', p, w_pool.astype(jnp.float32)).reshape(B, T, MIX_A)
    return (y * pool_scale.astype(jnp.float32)).astype(u.dtype)


def hgrn2_mixer(q, fl, i, g, lb, norm_g):
    B, T, _ = q.shape
    H, Dh, C = HG_HEADS, HG_HEAD, HG_CHUNK
    N = T // C
    f = lb + (1.0 - lb) * jax.nn.sigmoid(fl.astype(jnp.float32))
    logf = jnp.log(f)
    k = 1.0 - f
    qf = jax.nn.silu(q.astype(jnp.float32)) * (Dh ** -0.5)

    def to_chunks(a):
        return a.reshape(B, N, C, H, Dh).transpose(1, 0, 3, 2, 4)

    qc, kc, vc = to_chunks(qf), to_chunks(k), to_chunks(i.astype(jnp.float32))
    bc = jnp.cumsum(to_chunks(logf), axis=3)
    causal = jnp.tril(jnp.ones((C, C), dtype=bool))[:, :, None]

    def step(S, inp):
        qh, kh, vh, bh = inp
        diff = bh[:, :, :, None, :] - bh[:, :, None, :, :]
        decay = jnp.exp(jnp.where(causal, diff, -jnp.inf))
        A = jnp.einsum('bhtk,bhsk,bhtsk->bhts', qh, kh, decay)
        o = (jnp.einsum('bhts,bhsv->bhtv', A, vh)
             + jnp.einsum('bhtk,bhkv->bhtv', qh * jnp.exp(bh), S))
        b_last = bh[:, :, -1:, :]
        S = (jnp.exp(b_last[:, :, 0, :])[..., None] * S
             + jnp.einsum('bhsk,bhsv->bhkv', kh * jnp.exp(b_last - bh), vh))
        return S, o

    S0 = jnp.zeros((B, H, Dh, Dh), jnp.float32)
    _, o = lax.scan(step, S0, (qc, kc, vc, bc))
    o = o.transpose(1, 0, 3, 2, 4).reshape(B, T, H, Dh)
    o = rmsnorm(o, norm_g).reshape(B, T, MIX_B)
    return (o * jax.nn.silu(g.astype(jnp.float32))).astype(q.dtype)


def fox_attention(q, k, v, fl):
    B, T, H, Dh = q.shape
    Fc = jnp.cumsum(jax.nn.log_sigmoid(fl.astype(jnp.float32)), axis=1).transpose(0, 2, 1)
    scale = Dh ** -0.5
    outs = []
    for blk in range(T // FOX_BLOCK):
        q0, q1 = blk * FOX_BLOCK, (blk + 1) * FOX_BLOCK
        s = jnp.einsum('bqhd,bkhd->bhqk', q[:, q0:q1], k[:, :q1]).astype(jnp.float32) * scale
        s = s + (Fc[:, :, q0:q1, None] - Fc[:, :, None, :q1])
        mask = (q0 + jnp.arange(FOX_BLOCK))[:, None] >= jnp.arange(q1)[None, :]
        p = jax.nn.softmax(jnp.where(mask, s, -jnp.inf), axis=-1)
        outs.append(jnp.einsum('bhqk,bkhd->bqhd', p.astype(v.dtype), v[:, :q1]))
    return jnp.concatenate(outs, axis=1).reshape(B, T, H * Dh)


def cross_attention(h, mem_n, wq, wkv, wo):
    B, T, _ = h.shape
    M = mem_n.shape[1]
    q = (h @ wq).reshape(B, T, XA_HEADS, XA_HEAD)
    kv = mem_n @ wkv
    k = kv[..., :D_MODEL].reshape(B, M, XA_HEADS, XA_HEAD)
    v = kv[..., D_MODEL:].reshape(B, M, XA_HEADS, XA_HEAD)
    s = jnp.einsum('bthd,bmhd->bhtm', q, k).astype(jnp.float32) * (XA_HEAD ** -0.5)
    p = jax.nn.softmax(s, axis=-1)
    o = jnp.einsum('bhtm,bmhd->bthd', p.astype(v.dtype), v).reshape(B, T, D_MODEL)
    return o @ wo


def setup_inputs(seed: int = 0) -> dict:
    key = jax.random.key(seed)
    ks = jax.random.split(key, 24)
    D = D_MODEL

    def nrm(k, shape, s):
        return jax.random.normal(k, shape, jnp.float32) * s

    def gain(k, shape):
        return 1.0 + 0.05 * jax.random.normal(k, shape, jnp.float32)

    return {
        "x": nrm(ks[0], (BATCH, SEQ, D), 1.0),
        "mem": nrm(ks[1], (BATCH, N_MEM, D), 1.0),
        "lb_table": nrm(ks[2], (DEPTH + 1, MIX_B), 0.5),
        "ev_norm": gain(ks[3], (N_EVEN, D)),
        "ev_w_in": nrm(ks[4], (N_EVEN, D, IN_EVEN), D ** -0.5),
        "ev_w_pool": nrm(ks[5], (N_EVEN, N_POOL_GROUPS, POOL_GROUP, POOL_GROUP), POOL_GROUP ** -0.5),
        "ev_pool_scale": gain(ks[6], (N_EVEN, MIX_A)),
        "ev_hg_norm": gain(ks[7], (N_EVEN, HG_HEAD)),
        "ev_w_out": nrm(ks[8], (N_EVEN, D, D), D ** -0.5),
        "od_norm": gain(ks[9], (N_ODD, D)),
        "od_w_in": nrm(ks[10], (N_ODD, D, IN_ODD), D ** -0.5),
        "od_b_f": 2.0 + nrm(ks[11], (N_ODD, FOX_HEADS), 0.1),
        "od_w_out": nrm(ks[12], (N_ODD, D, D), D ** -0.5),
        "xa_norm": gain(ks[13], (DEPTH, D)),
        "xa_mem_norm": gain(ks[14], (DEPTH, D)),
        "xa_wq": nrm(ks[15], (DEPTH, D, D), D ** -0.5),
        "xa_wkv": nrm(ks[16], (DEPTH, D, 2 * D), D ** -0.5),
        "xa_wo": nrm(ks[17], (DEPTH, D, D), D ** -0.5),
        "ffn_norm": gain(ks[18], (DEPTH, D)),
        "ffn_w_gate": nrm(ks[19], (DEPTH, D, D_FF), D ** -0.5),
        "ffn_w_up": nrm(ks[20], (DEPTH, D, D_FF), D ** -0.5),
        "ffn_w_down": nrm(ks[21], (DEPTH, D_FF, D), D_FF ** -0.5),
        "final_norm": gain(ks[22], (D,)),
    }


def reference(x, mem, lb_table, ev_norm, ev_w_in, ev_w_pool, ev_pool_scale, ev_hg_norm, ev_w_out,
              od_norm, od_w_in, od_b_f, od_w_out, xa_norm, xa_mem_norm, xa_wq, xa_wkv, xa_wo,
              ffn_norm, ffn_w_gate, ffn_w_up, ffn_w_down, final_norm):
    B, T, D = x.shape
    lb_cum = jnp.cumsum(jax.nn.softmax(lb_table.astype(jnp.float32), axis=0), axis=0)
    for l in range(DEPTH):
        if l % 2 == 0:
            e = l // 2
            h = rmsnorm(x, ev_norm[e])
            z = h @ ev_w_in[e]
            u = z[..., :MIX_A]
            q, fl, i, g = jnp.split(z[..., MIX_A:], 4, axis=-1)
            ya = pool_mixer(u, ev_w_pool[e], ev_pool_scale[e])
            yb = hgrn2_mixer(q, fl, i, g, lb_cum[l + 1] - lb_cum[0], ev_hg_norm[e])
            x = x + jnp.concatenate([ya, yb], axis=-1) @ ev_w_out[e]
        else:
            o = l // 2
            h = rmsnorm(x, od_norm[o])
            z = h @ od_w_in[o]
            q = z[..., :D].reshape(B, T, FOX_HEADS, FOX_HEAD)
            k = z[..., D:2 * D].reshape(B, T, FOX_HEADS, FOX_HEAD)
            v = z[..., 2 * D:3 * D].reshape(B, T, FOX_HEADS, FOX_HEAD)
            fl = z[..., 3 * D:] + od_b_f[o]
            x = x + fox_attention(q, k, v, fl) @ od_w_out[o]
        h = rmsnorm(x, xa_norm[l])
        x = x + cross_attention(h, rmsnorm(mem, xa_mem_norm[l]), xa_wq[l], xa_wkv[l], xa_wo[l])
        h = rmsnorm(x, ffn_norm[l])
        x = x + (jax.nn.silu(h @ ffn_w_gate[l]) * (h @ ffn_w_up[l])) @ ffn_w_down[l]
    return rmsnorm(x, final_norm)
```

```python
import functools

import jax
import jax.numpy as jnp
from jax import lax
from jax.experimental import pallas as pl
from jax.experimental.pallas import tpu as pltpu

F32 = jnp.float32
BF16 = jnp.bfloat16

EPS = 1e-6
N_MEM = 256
POOL_WINDOWS = (2, 4, 8, 16)
POOL_HALO = 16
HG_HEAD = 128
HG_BLOCK = 16
FOX_HEAD = 128
XA_HEADS = 4

LANES = 128
VMEM_CAP = 56 * 1024 * 1024
VMEM_FLOOR = 32 * 1024 * 1024


def _params(semantics, vmem_estimate):
    limit = int(min(max(vmem_estimate * 5 // 4, VMEM_FLOOR), VMEM_CAP))
    return pltpu.CompilerParams(dimension_semantics=semantics, vmem_limit_bytes=limit)


def _nbytes(shape, dtype):
    n = jnp.dtype(dtype).itemsize
    for s in shape:
        n *= s
    return n


def _rmsnorm_body(x_ref, g_ref, o_ref):
    x = x_ref[...]
    ms = jnp.mean(x * x, axis=-1, keepdims=True)
    o_ref[...] = (x * lax.rsqrt(ms + EPS) * g_ref[...]).astype(o_ref.dtype)


def _rmsnorm(x, g, out_dtype, tm=512):
    M, D = x.shape
    tm = min(tm, M)
    est = 2 * (_nbytes((tm, D), x.dtype) + _nbytes((tm, D), out_dtype)) + 3 * _nbytes((tm, D), F32)
    return pl.pallas_call(
        _rmsnorm_body,
        grid=(M // tm,),
        in_specs=[pl.BlockSpec((tm, D), lambda i: (i, 0)),
                  pl.BlockSpec((1, D), lambda i: (0, 0))],
        out_specs=pl.BlockSpec((tm, D), lambda i: (i, 0)),
        out_shape=jax.ShapeDtypeStruct((M, D), out_dtype),
        compiler_params=_params(("parallel",), est),
        name="rmsnorm",
    )(x, g.reshape(1, D).astype(F32))


def _mm_body(x_ref, w_ref, cs_ref, o_ref):
    acc = jnp.dot(x_ref[...], w_ref[...], preferred_element_type=F32)
    o_ref[...] = (acc * cs_ref[...]).astype(o_ref.dtype)


def _mm_res_body(x_ref, w_ref, r_ref, o_ref):
    acc = jnp.dot(x_ref[...], w_ref[...], preferred_element_type=F32)
    o_ref[...] = (r_ref[...] + acc).astype(o_ref.dtype)


def _mm_swiglu_body(x_ref, wg_ref, wu_ref, o_ref):
    x = x_ref[...]
    g = jnp.dot(x, wg_ref[...], preferred_element_type=F32)
    u = jnp.dot(x, wu_ref[...], preferred_element_type=F32)
    o_ref[...] = (g * jax.nn.sigmoid(g) * u).astype(o_ref.dtype)


def _pick_tile(n, pref):
    t = min(pref, n)
    while n % t:
        t //= 2
    return t


def _matmul(x, w, out_dtype, colscale=None, tm=1024, tn=1024):
    M, K = x.shape
    N = w.shape[1]
    tm, tn = _pick_tile(M, tm), _pick_tile(N, tn)
    if colscale is None:
        colscale = jnp.ones((N,), F32)
    est = 2 * (_nbytes((tm, K), x.dtype) + _nbytes((K, tn), w.dtype) + _nbytes((tm, tn), out_dtype)) \
        + 2 * _nbytes((tm, tn), F32)
    return pl.pallas_call(
        _mm_body,
        grid=(N // tn, M // tm),
        in_specs=[pl.BlockSpec((tm, K), lambda j, i: (i, 0)),
                  pl.BlockSpec((K, tn), lambda j, i: (0, j)),
                  pl.BlockSpec((1, tn), lambda j, i: (0, j))],
        out_specs=pl.BlockSpec((tm, tn), lambda j, i: (i, j)),
        out_shape=jax.ShapeDtypeStruct((M, N), out_dtype),
        compiler_params=_params(("parallel", "parallel"), est),
        name="matmul",
    )(x, w, colscale.reshape(1, N))


def _matmul_residual(x, w, r, tm=512, tn=1024):
    M, K = x.shape
    N = w.shape[1]
    tm, tn = _pick_tile(M, tm), _pick_tile(N, tn)
    est = 2 * (_nbytes((tm, K), x.dtype) + _nbytes((K, tn), w.dtype) + 2 * _nbytes((tm, tn), F32)) \
        + 2 * _nbytes((tm, tn), F32)
    return pl.pallas_call(
        _mm_res_body,
        grid=(N // tn, M // tm),
        in_specs=[pl.BlockSpec((tm, K), lambda j, i: (i, 0)),
                  pl.BlockSpec((K, tn), lambda j, i: (0, j)),
                  pl.BlockSpec((tm, tn), lambda j, i: (i, j))],
        out_specs=pl.BlockSpec((tm, tn), lambda j, i: (i, j)),
        out_shape=jax.ShapeDtypeStruct((M, N), F32),
        compiler_params=_params(("parallel", "parallel"), est),
        name="matmul_residual",
    )(x, w, r)


def _matmul_swiglu(x, wg, wu, tm=1024, tn=512):
    M, K = x.shape
    N = wg.shape[1]
    tm, tn = _pick_tile(M, tm), _pick_tile(N, tn)
    est = 2 * (_nbytes((tm, K), x.dtype) + 2 * _nbytes((K, tn), wg.dtype) + _nbytes((tm, tn), BF16)) \
        + 4 * _nbytes((tm, tn), F32)
    return pl.pallas_call(
        _mm_swiglu_body,
        grid=(N // tn, M // tm),
        in_specs=[pl.BlockSpec((tm, K), lambda j, i: (i, 0)),
                  pl.BlockSpec((K, tn), lambda j, i: (0, j)),
                  pl.BlockSpec((K, tn), lambda j, i: (0, j))],
        out_specs=pl.BlockSpec((tm, tn), lambda j, i: (i, j)),
        out_shape=jax.ShapeDtypeStruct((M, N), BF16),
        compiler_params=_params(("parallel", "parallel"), est),
        name="matmul_swiglu",
    )(x, wg, wu)


def _pool_body(u_ref, halo_ref, w_ref, sc_ref, o_ref, ext_ref, *, tt, group):
    t = pl.program_id(1)
    u = u_ref[0]
    ext_ref[0:POOL_HALO, :] = jnp.where(t > 0, halo_ref[0], 0.0)
    ext_ref[POOL_HALO:POOL_HALO + tt, :] = u
    pos = t * tt + lax.broadcasted_iota(jnp.int32, (tt, 1), 0)
    for gi, win in enumerate(POOL_WINDOWS):
        c0, c1 = gi * group, (gi + 1) * group
        tok = u[:, c0:c1]
        acc = tok
        for d in range(1, win):
            acc = acc + ext_ref[POOL_HALO - d:POOL_HALO - d + tt, c0:c1]
        cnt = jnp.minimum(pos + 1, win).astype(F32)
        p = acc / cnt - tok
        y = jnp.dot(p.astype(BF16), w_ref[gi], preferred_element_type=F32)
        o_ref[0, :, c0:c1] = (y * sc_ref[:, c0:c1]).astype(o_ref.dtype)


def _pool_mixer(z, w_pool, scale, mix_a, tt=512):
    B, T, _ = z.shape
    tt = min(tt, T)
    group = mix_a // len(POOL_WINDOWS)
    halo_blocks = tt // POOL_HALO
    est = 2 * (_nbytes((tt, mix_a), F32) + _nbytes((tt, mix_a), BF16)) + 4 * _nbytes((tt, mix_a), F32)
    return pl.pallas_call(
        functools.partial(_pool_body, tt=tt, group=group),
        grid=(B, T // tt),
        in_specs=[pl.BlockSpec((1, tt, mix_a), lambda b, t: (b, t, 0)),
                  pl.BlockSpec((1, POOL_HALO, mix_a),
                               lambda b, t: (b, jnp.maximum(t * halo_blocks - 1, 0), 0)),
                  pl.BlockSpec((len(POOL_WINDOWS), group, group), lambda b, t: (0, 0, 0)),
                  pl.BlockSpec((1, mix_a), lambda b, t: (0, 0))],
        out_specs=pl.BlockSpec((1, tt, mix_a), lambda b, t: (b, t, 0)),
        out_shape=jax.ShapeDtypeStruct((B, T, mix_a), BF16),
        scratch_shapes=[pltpu.VMEM((tt + POOL_HALO, mix_a), F32)],
        compiler_params=_params(("parallel", "parallel"), est),
        name="pool_mixer",
    )(z, z, w_pool, scale.reshape(1, mix_a).astype(F32))


def _hgrn_body(lbt_ref, ng_ref, q_ref, f_ref, i_ref, g_ref, o_ref, st_ref, acc_ref, *, layer, tt):
    R = HG_BLOCK
    nb = tt // R
    t = pl.program_id(2)

    @pl.when(t == 0)
    def _():
        st_ref[...] = jnp.zeros_like(st_ref)

    lbt = lbt_ref[...]
    e = jnp.exp(lbt - jnp.max(lbt, axis=0, keepdims=True))
    sm = e / jnp.sum(e, axis=0, keepdims=True)
    lb = jnp.sum(sm[1:layer + 2], axis=0, keepdims=True)

    f = lb + (1.0 - lb) * jax.nn.sigmoid(f_ref[0])
    logf = jnp.log(f)
    kk = 1.0 - f
    q = q_ref[0]
    qf = q * jax.nn.sigmoid(q) * (HG_HEAD ** -0.5)
    v = i_ref[0]

    rin = lax.broadcasted_iota(jnp.int32, (tt, HG_HEAD), 0) & (R - 1)
    b = logf
    sh = 1
    while sh < R:
        b = b + jnp.where(rin >= sh, pltpu.roll(b, sh, axis=0), 0.0)
        sh *= 2

    b3 = b.reshape(nb, R, HG_HEAD)
    q3 = qf.reshape(nb, R, HG_HEAD)
    k3 = kk.reshape(nb, R, HG_HEAD)
    v3 = v.reshape(nb, R, HG_HEAD)

    tpos = lax.broadcasted_iota(jnp.int32, (1, R, 1), 1)
    od = jnp.zeros((nb, R, HG_HEAD), F32)
    for s in range(R):
        w = q3 * jnp.exp(b3 - b3[:, s:s + 1, :]) * k3[:, s:s + 1, :]
        a = jnp.sum(w, axis=-1, keepdims=True)
        a = jnp.where(tpos >= s, a, 0.0)
        od = od + a * v3[:, s:s + 1, :]

    b_last = b3[:, R - 1:R, :]
    q_in = (q3 * jnp.exp(b3)).astype(BF16)
    k_out = (k3 * jnp.exp(b_last - b3)).astype(BF16)
    dec = jnp.exp(b_last)
    v_bf = v3.astype(BF16)

    st = st_ref[...]
    for blk in range(nb):
        o_int = lax.dot_general(q_in[blk], st.astype(BF16), (((1,), (1,)), ((), ())),
                                preferred_element_type=F32)
        acc_ref[blk * R:(blk + 1) * R, :] = od[blk] + o_int
        upd = lax.dot_general(v_bf[blk], k_out[blk], (((0,), (0,)), ((), ())),
                              preferred_element_type=F32)
        st = st * dec[blk] + upd
    st_ref[...] = st

    o = acc_ref[...]
    ms = jnp.mean(o * o, axis=-1, keepdims=True)
    o = o * lax.rsqrt(ms + EPS) * ng_ref[...]
    g = g_ref[0]
    o_ref[0] = (o * (g * jax.nn.sigmoid(g))).astype(o_ref.dtype)


def _hgrn_mixer(z, lb_table, norm_g, layer, mix_a, mix_b, tt=256):
    B, T, _ = z.shape
    tt = min(tt, T)
    heads = mix_b // HG_HEAD
    c0 = mix_a // HG_HEAD
    est = 2 * 5 * _nbytes((tt, HG_HEAD), F32) + 24 * _nbytes((tt, HG_HEAD), F32)

    def col(off):
        return pl.BlockSpec((1, tt, HG_HEAD), lambda b, h, t, off=off: (b, t, c0 + off * heads + h))

    return pl.pallas_call(
        functools.partial(_hgrn_body, layer=layer, tt=tt),
        grid=(B, heads, T // tt),
        in_specs=[pl.BlockSpec((lb_table.shape[0], HG_HEAD), lambda b, h, t: (0, h)),
                  pl.BlockSpec((1, HG_HEAD), lambda b, h, t: (0, 0)),
                  col(0), col(1), col(2), col(3)],
        out_specs=pl.BlockSpec((1, tt, HG_HEAD), lambda b, h, t: (b, t, h)),
        out_shape=jax.ShapeDtypeStruct((B, T, mix_b), BF16),
        scratch_shapes=[pltpu.VMEM((HG_HEAD, HG_HEAD), F32),
                        pltpu.VMEM((tt, HG_HEAD), F32)],
        compiler_params=_params(("parallel", "parallel", "arbitrary"), est),
        name="hgrn2_mixer",
    )(lb_table.astype(F32), norm_g.reshape(1, HG_HEAD).astype(F32), z, z, z, z)


def _fgate_body(h_ref, w_ref, b_ref, o_ref, carry_ref, *, tt):
    t = pl.program_id(1)

    @pl.when(t == 0)
    def _():
        carry_ref[...] = jnp.zeros_like(carry_ref)

    fl = jnp.dot(h_ref[0], w_ref[...], preferred_element_type=F32) + b_ref[...]
    c = jnp.minimum(fl, 0.0) - jnp.log1p(jnp.exp(-jnp.abs(fl)))
    row = lax.broadcasted_iota(jnp.int32, c.shape, 0)
    sh = 1
    while sh < tt:
        c = c + jnp.where(row >= sh, pltpu.roll(c, sh, axis=0), 0.0)
        sh *= 2
    c = c + carry_ref[...]
    o_ref[0] = c
    carry_ref[...] = c[tt - 1:tt, :]


def _fox_gates(h, wf, bf, tt=512):
    B, T, D = h.shape
    tt = min(tt, T)
    est = 2 * (_nbytes((tt, D), BF16) + _nbytes((D, LANES), BF16) + _nbytes((tt, LANES), F32)) \
        + 8 * _nbytes((tt, LANES), F32)
    return pl.pallas_call(
        functools.partial(_fgate_body, tt=tt),
        grid=(B, T // tt),
        in_specs=[pl.BlockSpec((1, tt, D), lambda b, t: (b, t, 0)),
                  pl.BlockSpec((D, LANES), lambda b, t: (0, 0)),
                  pl.BlockSpec((1, LANES), lambda b, t: (0, 0))],
        out_specs=pl.BlockSpec((1, tt, LANES), lambda b, t: (b, t, 0)),
        out_shape=jax.ShapeDtypeStruct((B, T, LANES), F32),
        scratch_shapes=[pltpu.VMEM((1, LANES), F32)],
        compiler_params=_params(("parallel", "arbitrary"), est),
        name="fox_gates",
    )(h, wf, bf)


def _fox_body(q_ref, k_ref, v_ref, fc_ref, fr_ref, o_ref, m_ref, l_ref, acc_ref, *, tq):
    i = pl.program_id(2)
    q = q_ref[0]
    fq = fc_ref[0, 0]
    m_ref[...] = jnp.full_like(m_ref, -jnp.inf)
    l_ref[...] = jnp.zeros_like(l_ref)
    acc_ref[...] = jnp.zeros_like(acc_ref)

    def step(j, diagonal):
        start = pl.multiple_of(j * tq, tq)
        k = k_ref[0, pl.ds(start, tq), :]
        v = v_ref[0, pl.ds(start, tq), :]
        fk = fr_ref[0, 0, j]
        s = lax.dot_general(q, k, (((1,), (1,)), ((), ())), preferred_element_type=F32)
        s = s + (fq - fk)
        if diagonal:
            row = lax.broadcasted_iota(jnp.int32, (tq, tq), 0)
            col = lax.broadcasted_iota(jnp.int32, (tq, tq), 1)
            s = jnp.where(row >= col, s, -jnp.inf)
        m_prev = m_ref[...]
        m_new = jnp.maximum(m_prev, jnp.max(s, axis=-1, keepdims=True))
        alpha = jnp.exp(m_prev - m_new)
        p = jnp.exp(s - m_new)
        l_ref[...] = alpha * l_ref[...] + jnp.sum(p, axis=-1, keepdims=True)
        acc_ref[...] = alpha * acc_ref[...] + jnp.dot(p.astype(BF16), v, preferred_element_type=F32)
        m_ref[...] = m_new

    def body(j, carry):
        step(j, False)
        return carry

    lax.fori_loop(0, i, body, 0)
    step(i, True)
    o_ref[0] = (acc_ref[...] / l_ref[...]).astype(o_ref.dtype)


def _fox_attention(qkv, fcol, frow, heads, tq):
    B, T, _ = qkv.shape
    nq = T // tq
    est = 2 * (2 * _nbytes((T, FOX_HEAD), BF16) + 2 * _nbytes((tq, FOX_HEAD), BF16)
               + _nbytes((tq, LANES), F32) + _nbytes((nq, 8, tq), F32)) \
        + 6 * _nbytes((tq, tq), F32) + 3 * _nbytes((tq, LANES), F32)
    return pl.pallas_call(
        functools.partial(_fox_body, tq=tq),
        grid=(B, heads, nq),
        in_specs=[pl.BlockSpec((1, tq, FOX_HEAD), lambda b, h, i: (b, i, h)),
                  pl.BlockSpec((1, T, FOX_HEAD), lambda b, h, i: (b, 0, heads + h)),
                  pl.BlockSpec((1, T, FOX_HEAD), lambda b, h, i: (b, 0, 2 * heads + h)),
                  pl.BlockSpec((1, 1, tq, 1), lambda b, h, i: (b, h, i, 0)),
                  pl.BlockSpec((1, 1, nq, 1, tq), lambda b, h, i: (b, h, 0, 0, 0))],
        out_specs=pl.BlockSpec((1, tq, FOX_HEAD), lambda b, h, i: (b, i, h)),
        out_shape=jax.ShapeDtypeStruct((B, T, heads * FOX_HEAD), BF16),
        scratch_shapes=[pltpu.VMEM((tq, 1), F32),
                        pltpu.VMEM((tq, 1), F32),
                        pltpu.VMEM((tq, FOX_HEAD), F32)],
        compiler_params=_params(("parallel", "parallel", "arbitrary"), est),
        name="fox_attention",
    )(qkv, qkv, qkv, fcol, frow)


def _xattn_body(q_ref, kv_ref, o_ref, *, d_model):
    hd = d_model // XA_HEADS
    for h in range(XA_HEADS):
        q = q_ref[0, :, h * hd:(h + 1) * hd]
        k = kv_ref[0, :, h * hd:(h + 1) * hd]
        v = kv_ref[0, :, d_model + h * hd:d_model + (h + 1) * hd]
        s = lax.dot_general(q, k, (((1,), (1,)), ((), ())), preferred_element_type=F32)
        p = jnp.exp(s - jnp.max(s, axis=-1, keepdims=True))
        p = p / jnp.sum(p, axis=-1, keepdims=True)
        o = jnp.dot(p.astype(BF16), v, preferred_element_type=F32)
        o_ref[0, :, h * hd:(h + 1) * hd] = o.astype(o_ref.dtype)


def _xattn_core(q, kv, tq=512):
    B, T, D = q.shape
    M = kv.shape[1]
    tq = min(tq, T)
    est = 2 * (2 * _nbytes((tq, D), BF16) + _nbytes((M, 2 * D), BF16)) + 6 * _nbytes((tq, M), F32) \
        + 2 * _nbytes((tq, D // XA_HEADS), F32)
    return pl.pallas_call(
        functools.partial(_xattn_body, d_model=D),
        grid=(B, T // tq),
        in_specs=[pl.BlockSpec((1, tq, D), lambda b, t: (b, t, 0)),
                  pl.BlockSpec((1, M, 2 * D), lambda b, t: (b, 0, 0))],
        out_specs=pl.BlockSpec((1, tq, D), lambda b, t: (b, t, 0)),
        out_shape=jax.ShapeDtypeStruct((B, T, D), BF16),
        compiler_params=_params(("parallel", "parallel"), est),
        name="xattn_core",
    )(q, kv)


def kernel(x, mem, lb_table, ev_norm, ev_w_in, ev_w_pool, ev_pool_scale, ev_hg_norm, ev_w_out,
           od_norm, od_w_in, od_b_f, od_w_out, xa_norm, xa_mem_norm, xa_wq, xa_wkv, xa_wo,
           ffn_norm, ffn_w_gate, ffn_w_up, ffn_w_down, final_norm):
    B, T, D = x.shape
    M = B * T
    depth = xa_norm.shape[0]
    mix_a = ev_pool_scale.shape[1]
    mix_b = lb_table.shape[1]
    fox_heads = od_b_f.shape[1]
    n_mem = mem.shape[1]
    xa_scale = (D // XA_HEADS) ** -0.5
    fox_tq = _pick_tile(T, 256)

    xs = x.reshape(M, D)
    mem2 = mem.reshape(B * n_mem, D)
    for l in range(depth):
        if l % 2 == 0:
            e = l // 2
            h = _rmsnorm(xs, ev_norm[e], BF16)
            z = _matmul(h, ev_w_in[e].astype(BF16), F32).reshape(B, T, -1)
            ya = _pool_mixer(z, ev_w_pool[e].astype(BF16), ev_pool_scale[e], mix_a)
            yb = _hgrn_mixer(z, lb_table, ev_hg_norm[e], l, mix_a, mix_b)
            y = jnp.concatenate([ya, yb], axis=-1).reshape(M, D)
            xs = _matmul_residual(y, ev_w_out[e].astype(BF16), xs)
        else:
            o = l // 2
            h = _rmsnorm(xs, od_norm[o], BF16)
            w_in = od_w_in[o]
            qscale = jnp.concatenate([jnp.full((D,), FOX_HEAD ** -0.5, F32), jnp.ones((2 * D,), F32)])
            qkv = _matmul(h, w_in[:, :3 * D].astype(BF16), BF16, colscale=qscale).reshape(B, T, 3 * D)
            wf = jnp.zeros((D, LANES), BF16).at[:, :fox_heads].set(w_in[:, 3 * D:].astype(BF16))
            bf = jnp.zeros((1, LANES), F32).at[0, :fox_heads].set(od_b_f[o].astype(F32))
            fc = _fox_gates(h.reshape(B, T, D), wf, bf)[..., :fox_heads]
            fc = fc.transpose(0, 2, 1)
            fcol = fc.reshape(B, fox_heads, T, 1)
            frow = fc.reshape(B, fox_heads, T // fox_tq, 1, fox_tq)
            y = _fox_attention(qkv, fcol, frow, fox_heads, fox_tq).reshape(M, D)
            xs = _matmul_residual(y, od_w_out[o].astype(BF16), xs)

        h = _rmsnorm(xs, xa_norm[l], BF16)
        mn = _rmsnorm(mem2, xa_mem_norm[l], BF16)
        q = _matmul(h, xa_wq[l].astype(BF16), BF16, colscale=jnp.full((D,), xa_scale, F32))
        kv = _matmul(mn, xa_wkv[l].astype(BF16), BF16)
        a = _xattn_core(q.reshape(B, T, D), kv.reshape(B, n_mem, 2 * D)).reshape(M, D)
        xs = _matmul_residual(a, xa_wo[l].astype(BF16), xs)

        h = _rmsnorm(xs, ffn_norm[l], BF16)
        act = _matmul_swiglu(h, ffn_w_gate[l].astype(BF16), ffn_w_up[l].astype(BF16))
        xs = _matmul_residual(act, ffn_w_down[l].astype(BF16), xs)
    return _rmsnorm(xs, final_norm, x.dtype).reshape(B, T, D)
```

```python
import functools

import jax
import jax.numpy as jnp
from jax import lax
from jax.experimental import pallas as pl
from jax.experimental.pallas import tpu as pltpu

F32 = jnp.float32
BF16 = jnp.bfloat16

EPS = 1e-6
N_MEM = 256
POOL_WINDOWS = (2, 4, 8, 16)
POOL_HALO = 16
HG_HEAD = 128
HG_BLOCK = 16
FOX_HEAD = 128
XA_HEADS = 4

LOG2E = 1.4426950408889634
LANES = 128
VMEM_CAP = 56 * 1024 * 1024
VMEM_FLOOR = 32 * 1024 * 1024


def _params(semantics, vmem_estimate):
    limit = int(min(max(vmem_estimate * 5 // 4, VMEM_FLOOR), VMEM_CAP))
    return pltpu.CompilerParams(dimension_semantics=semantics, vmem_limit_bytes=limit)


def _nbytes(shape, dtype):
    n = jnp.dtype(dtype).itemsize
    for s in shape:
        n *= s
    return n


def _rmsnorm_body(x_ref, g_ref, o_ref):
    x = x_ref[...]
    ms = jnp.mean(x * x, axis=-1, keepdims=True)
    o_ref[...] = (x * lax.rsqrt(ms + EPS) * g_ref[...]).astype(o_ref.dtype)


def _rmsnorm(x, g, out_dtype, tm=512):
    M, D = x.shape
    tm = min(tm, M)
    est = 2 * (_nbytes((tm, D), x.dtype) + _nbytes((tm, D), out_dtype)) + 3 * _nbytes((tm, D), F32)
    return pl.pallas_call(
        _rmsnorm_body,
        grid=(M // tm,),
        in_specs=[pl.BlockSpec((tm, D), lambda i: (i, 0)),
                  pl.BlockSpec((1, D), lambda i: (0, 0))],
        out_specs=pl.BlockSpec((tm, D), lambda i: (i, 0)),
        out_shape=jax.ShapeDtypeStruct((M, D), out_dtype),
        compiler_params=_params(("parallel",), est),
        name="rmsnorm",
    )(x, g.reshape(1, D).astype(F32))


def _mm_body(x_ref, w_ref, cs_ref, o_ref):
    acc = jnp.dot(x_ref[...], w_ref[...], preferred_element_type=F32)
    o_ref[...] = (acc * cs_ref[...]).astype(o_ref.dtype)


def _mm_res_body(x_ref, w_ref, r_ref, o_ref):
    acc = jnp.dot(x_ref[...], w_ref[...], preferred_element_type=F32)
    o_ref[...] = (r_ref[...] + acc).astype(o_ref.dtype)


def _mm_swiglu_body(x_ref, wg_ref, wu_ref, o_ref):
    x = x_ref[...]
    g = jnp.dot(x, wg_ref[...], preferred_element_type=F32)
    u = jnp.dot(x, wu_ref[...], preferred_element_type=F32)
    o_ref[...] = (g * jax.nn.sigmoid(g) * u).astype(o_ref.dtype)


def _pick_tile(n, pref):
    t = min(pref, n)
    while n % t:
        t //= 2
    return t


def _matmul(x, w, out_dtype, colscale=None, tm=1024, tn=1024):
    M, K = x.shape
    N = w.shape[1]
    tm, tn = _pick_tile(M, tm), _pick_tile(N, tn)
    if colscale is None:
        colscale = jnp.ones((N,), F32)
    est = 2 * (_nbytes((tm, K), x.dtype) + _nbytes((K, tn), w.dtype) + _nbytes((tm, tn), out_dtype)) \
        + 2 * _nbytes((tm, tn), F32)
    return pl.pallas_call(
        _mm_body,
        grid=(N // tn, M // tm),
        in_specs=[pl.BlockSpec((tm, K), lambda j, i: (i, 0)),
                  pl.BlockSpec((K, tn), lambda j, i: (0, j)),
                  pl.BlockSpec((1, tn), lambda j, i: (0, j))],
        out_specs=pl.BlockSpec((tm, tn), lambda j, i: (i, j)),
        out_shape=jax.ShapeDtypeStruct((M, N), out_dtype),
        compiler_params=_params(("parallel", "parallel"), est),
        name="matmul",
    )(x, w, colscale.reshape(1, N))


def _matmul_residual(x, w, r, tn=1024):
    M, K = x.shape
    N = w.shape[1]
    tm = 1024 if K <= 2048 else 512
    tm, tn = _pick_tile(M, tm), _pick_tile(N, tn)
    est = 2 * (_nbytes((tm, K), x.dtype) + _nbytes((K, tn), w.dtype) + 2 * _nbytes((tm, tn), F32)) \
        + 2 * _nbytes((tm, tn), F32)
    return pl.pallas_call(
        _mm_res_body,
        grid=(N // tn, M // tm),
        in_specs=[pl.BlockSpec((tm, K), lambda j, i: (i, 0)),
                  pl.BlockSpec((K, tn), lambda j, i: (0, j)),
                  pl.BlockSpec((tm, tn), lambda j, i: (i, j))],
        out_specs=pl.BlockSpec((tm, tn), lambda j, i: (i, j)),
        out_shape=jax.ShapeDtypeStruct((M, N), F32),
        compiler_params=_params(("parallel", "parallel"), est),
        name="matmul_residual",
    )(x, w, r)


def _matmul_swiglu(x, wg, wu, tm=1024, tn=512):
    M, K = x.shape
    N = wg.shape[1]
    tm, tn = _pick_tile(M, tm), _pick_tile(N, tn)
    est = 2 * (_nbytes((tm, K), x.dtype) + 2 * _nbytes((K, tn), wg.dtype) + _nbytes((tm, tn), BF16)) \
        + 4 * _nbytes((tm, tn), F32)
    return pl.pallas_call(
        _mm_swiglu_body,
        grid=(N // tn, M // tm),
        in_specs=[pl.BlockSpec((tm, K), lambda j, i: (i, 0)),
                  pl.BlockSpec((K, tn), lambda j, i: (0, j)),
                  pl.BlockSpec((K, tn), lambda j, i: (0, j))],
        out_specs=pl.BlockSpec((tm, tn), lambda j, i: (i, j)),
        out_shape=jax.ShapeDtypeStruct((M, N), BF16),
        compiler_params=_params(("parallel", "parallel"), est),
        name="matmul_swiglu",
    )(x, wg, wu)


def _pool_body(u_ref, halo_ref, w_ref, sc_ref, o_ref, ext_ref, *, tt, group):
    t = pl.program_id(1)
    u = u_ref[0]
    ext_ref[0:POOL_HALO, :] = jnp.where(t > 0, halo_ref[0], 0.0)
    ext_ref[POOL_HALO:POOL_HALO + tt, :] = u
    pos = t * tt + lax.broadcasted_iota(jnp.int32, (tt, 1), 0)
    for gi, win in enumerate(POOL_WINDOWS):
        c0, c1 = gi * group, (gi + 1) * group
        tok = u[:, c0:c1]
        acc = tok
        for d in range(1, win):
            acc = acc + ext_ref[POOL_HALO - d:POOL_HALO - d + tt, c0:c1]
        cnt = jnp.minimum(pos + 1, win).astype(F32)
        p = acc / cnt - tok
        y = jnp.dot(p.astype(BF16), w_ref[gi], preferred_element_type=F32)
        o_ref[0, :, c0:c1] = (y * sc_ref[:, c0:c1]).astype(o_ref.dtype)


def _pool_mixer(z, w_pool, scale, mix_a, tt=512):
    B, T, _ = z.shape
    tt = min(tt, T)
    group = mix_a // len(POOL_WINDOWS)
    halo_blocks = tt // POOL_HALO
    est = 2 * (_nbytes((tt, mix_a), F32) + _nbytes((tt, mix_a), BF16)) + 4 * _nbytes((tt, mix_a), F32)
    return pl.pallas_call(
        functools.partial(_pool_body, tt=tt, group=group),
        grid=(B, T // tt),
        in_specs=[pl.BlockSpec((1, tt, mix_a), lambda b, t: (b, t, 0)),
                  pl.BlockSpec((1, POOL_HALO, mix_a),
                               lambda b, t: (b, jnp.maximum(t * halo_blocks - 1, 0), 0)),
                  pl.BlockSpec((len(POOL_WINDOWS), group, group), lambda b, t: (0, 0, 0)),
                  pl.BlockSpec((1, mix_a), lambda b, t: (0, 0))],
        out_specs=pl.BlockSpec((1, tt, mix_a), lambda b, t: (b, t, 0)),
        out_shape=jax.ShapeDtypeStruct((B, T, mix_a), BF16),
        scratch_shapes=[pltpu.VMEM((tt + POOL_HALO, mix_a), F32)],
        compiler_params=_params(("parallel", "parallel"), est),
        name="pool_mixer",
    )(z, z, w_pool, scale.reshape(1, mix_a).astype(F32))


def _hgrn_body(lbt_ref, ng_ref, q_ref, f_ref, i_ref, g_ref, o_ref, st_ref, acc_ref, *, layer, tt):
    R = HG_BLOCK
    nb = tt // R
    t = pl.program_id(2)

    @pl.when(t == 0)
    def _():
        st_ref[...] = jnp.zeros_like(st_ref)

    lbt = lbt_ref[...]
    e = jnp.exp(lbt - jnp.max(lbt, axis=0, keepdims=True))
    sm = e / jnp.sum(e, axis=0, keepdims=True)
    lb = jnp.sum(sm[1:layer + 2], axis=0, keepdims=True)

    f = lb + (1.0 - lb) * jax.nn.sigmoid(f_ref[0])
    logf = jnp.log(f)
    kk = 1.0 - f
    q = q_ref[0]
    qf = q * jax.nn.sigmoid(q) * (HG_HEAD ** -0.5)
    v = i_ref[0]

    rin = lax.broadcasted_iota(jnp.int32, (tt, HG_HEAD), 0) & (R - 1)
    b = logf
    sh = 1
    while sh < R:
        b = b + jnp.where(rin >= sh, pltpu.roll(b, sh, axis=0), 0.0)
        sh *= 2

    b3 = b.reshape(nb, R, HG_HEAD)
    q3 = qf.reshape(nb, R, HG_HEAD)
    k3 = kk.reshape(nb, R, HG_HEAD)
    v3 = v.reshape(nb, R, HG_HEAD)

    tpos = lax.broadcasted_iota(jnp.int32, (1, R, 1), 1)
    od = jnp.zeros((nb, R, HG_HEAD), F32)
    for s in range(R):
        w = q3 * jnp.exp(b3 - b3[:, s:s + 1, :]) * k3[:, s:s + 1, :]
        a = jnp.sum(w, axis=-1, keepdims=True)
        a = jnp.where(tpos >= s, a, 0.0)
        od = od + a * v3[:, s:s + 1, :]

    b_last = b3[:, R - 1:R, :]
    q_in = (q3 * jnp.exp(b3)).astype(BF16)
    k_out = (k3 * jnp.exp(b_last - b3)).astype(BF16)
    dec = jnp.exp(b_last)
    v_bf = v3.astype(BF16)

    st = st_ref[...]
    for blk in range(nb):
        o_int = lax.dot_general(q_in[blk], st.astype(BF16), (((1,), (1,)), ((), ())),
                                preferred_element_type=F32)
        acc_ref[blk * R:(blk + 1) * R, :] = od[blk] + o_int
        upd = lax.dot_general(v_bf[blk], k_out[blk], (((0,), (0,)), ((), ())),
                              preferred_element_type=F32)
        st = st * dec[blk] + upd
    st_ref[...] = st

    o = acc_ref[...]
    ms = jnp.mean(o * o, axis=-1, keepdims=True)
    o = o * lax.rsqrt(ms + EPS) * ng_ref[...]
    g = g_ref[0]
    o_ref[0] = (o * (g * jax.nn.sigmoid(g))).astype(o_ref.dtype)


def _hgrn_mixer(z, lb_table, norm_g, layer, mix_a, mix_b, tt=256):
    B, T, _ = z.shape
    tt = min(tt, T)
    heads = mix_b // HG_HEAD
    c0 = mix_a // HG_HEAD
    est = 2 * 5 * _nbytes((tt, HG_HEAD), F32) + 24 * _nbytes((tt, HG_HEAD), F32)

    def col(off):
        return pl.BlockSpec((1, tt, HG_HEAD), lambda b, h, t, off=off: (b, t, c0 + off * heads + h))

    return pl.pallas_call(
        functools.partial(_hgrn_body, layer=layer, tt=tt),
        grid=(B, heads, T // tt),
        in_specs=[pl.BlockSpec((lb_table.shape[0], HG_HEAD), lambda b, h, t: (0, h)),
                  pl.BlockSpec((1, HG_HEAD), lambda b, h, t: (0, 0)),
                  col(0), col(1), col(2), col(3)],
        out_specs=pl.BlockSpec((1, tt, HG_HEAD), lambda b, h, t: (b, t, h)),
        out_shape=jax.ShapeDtypeStruct((B, T, mix_b), BF16),
        scratch_shapes=[pltpu.VMEM((HG_HEAD, HG_HEAD), F32),
                        pltpu.VMEM((tt, HG_HEAD), F32)],
        compiler_params=_params(("parallel", "parallel", "arbitrary"), est),
        name="hgrn2_mixer",
    )(lb_table.astype(F32), norm_g.reshape(1, HG_HEAD).astype(F32), z, z, z, z)


def _fgate_body(h_ref, w_ref, b_ref, o_ref, carry_ref, *, tt):
    t = pl.program_id(1)

    @pl.when(t == 0)
    def _():
        carry_ref[...] = jnp.zeros_like(carry_ref)

    fl = jnp.dot(h_ref[0], w_ref[...], preferred_element_type=F32) + b_ref[...]
    c = (jnp.minimum(fl, 0.0) - jnp.log1p(jnp.exp(-jnp.abs(fl)))) * LOG2E
    row = lax.broadcasted_iota(jnp.int32, c.shape, 0)
    sh = 1
    while sh < tt:
        c = c + jnp.where(row >= sh, pltpu.roll(c, sh, axis=0), 0.0)
        sh *= 2
    c = c + carry_ref[...]
    o_ref[0] = c
    carry_ref[...] = c[tt - 1:tt, :]


def _fox_gates(h, wf, bf, tt=512):
    B, T, D = h.shape
    tt = min(tt, T)
    est = 2 * (_nbytes((tt, D), BF16) + _nbytes((D, LANES), BF16) + _nbytes((tt, LANES), F32)) \
        + 8 * _nbytes((tt, LANES), F32)
    return pl.pallas_call(
        functools.partial(_fgate_body, tt=tt),
        grid=(B, T // tt),
        in_specs=[pl.BlockSpec((1, tt, D), lambda b, t: (b, t, 0)),
                  pl.BlockSpec((D, LANES), lambda b, t: (0, 0)),
                  pl.BlockSpec((1, LANES), lambda b, t: (0, 0))],
        out_specs=pl.BlockSpec((1, tt, LANES), lambda b, t: (b, t, 0)),
        out_shape=jax.ShapeDtypeStruct((B, T, LANES), F32),
        scratch_shapes=[pltpu.VMEM((1, LANES), F32)],
        compiler_params=_params(("parallel", "arbitrary"), est),
        name="fox_gates",
    )(h, wf, bf)


def _fox_body(q_ref, k_ref, v_ref, f_ref, fr_ref, o_ref, vt_ref, fk_ref, *, tq, seq):
    h = pl.program_id(1)
    i = pl.program_id(2)
    reps = tq // LANES

    @pl.when(i == 0)
    def _():
        for c in range(seq // tq):
            rows = slice(c * tq, (c + 1) * tq)
            vt_ref[:, rows] = v_ref[0, rows, :].astype(F32).T.astype(BF16)
            fblk = f_ref[0, rows, :]
            lane = lax.broadcasted_iota(jnp.int32, fblk.shape, 1)
            col = jnp.sum(jnp.where(lane == h, fblk, 0.0), axis=-1, keepdims=True)
            fk_ref[rows, :] = jnp.broadcast_to(col, fblk.shape)

    qt = q_ref[0].astype(F32).T.astype(BF16)
    fq = fr_ref[0, 0, i]

    def scores(j):
        start = pl.multiple_of(j * tq, tq)
        return jnp.dot(k_ref[0, pl.ds(start, tq), :], qt, preferred_element_type=F32)

    def update(j, st, m_prev, l_prev, acc, diagonal):
        start = pl.multiple_of(j * tq, tq)
        t = st - pltpu.repeat(fk_ref[pl.ds(start, tq), :], reps, axis=1)
        if diagonal:
            key = lax.broadcasted_iota(jnp.int32, (tq, tq), 0)
            qry = lax.broadcasted_iota(jnp.int32, (tq, tq), 1)
            t = jnp.where(key <= qry, t, -jnp.inf)
        m_new = jnp.maximum(m_prev, jnp.max(t, axis=0, keepdims=True) + fq)
        alpha = jnp.exp2(m_prev - m_new)
        p = jnp.exp2(t - (m_new - fq))
        l_new = alpha * l_prev + jnp.sum(p, axis=0, keepdims=True)
        pv = jnp.dot(vt_ref[:, pl.ds(start, tq)], p.astype(BF16), preferred_element_type=F32)
        return m_new, l_new, alpha * acc + pv

    def body(j, carry):
        st, m, l, acc = carry
        st_next = scores(j + 1)
        m, l, acc = update(j, st, m, l, acc, False)
        return st_next, m, l, acc

    init = (scores(0),
            jnp.full((1, tq), -jnp.inf, F32),
            jnp.zeros((1, tq), F32),
            jnp.zeros((FOX_HEAD, tq), F32))
    st, m, l, acc = lax.fori_loop(0, i, body, init)
    m, l, acc = update(i, st, m, l, acc, True)
    o_ref[0] = (acc / l).T.astype(o_ref.dtype)


def _fox_attention(qkv, f, frow, heads, tq):
    B, T, _ = qkv.shape
    nq = T // tq
    est = 2 * (2 * _nbytes((T, FOX_HEAD), BF16) + 2 * _nbytes((tq, FOX_HEAD), BF16)
               + _nbytes((T, LANES), F32) + _nbytes((nq, 8, tq), F32)) \
        + _nbytes((T, FOX_HEAD), BF16) + _nbytes((T, LANES), F32) + 8 * _nbytes((tq, tq), F32)
    return pl.pallas_call(
        functools.partial(_fox_body, tq=tq, seq=T),
        grid=(B, heads, nq),
        in_specs=[pl.BlockSpec((1, tq, FOX_HEAD), lambda b, h, i: (b, i, h)),
                  pl.BlockSpec((1, T, FOX_HEAD), lambda b, h, i: (b, 0, heads + h)),
                  pl.BlockSpec((1, T, FOX_HEAD), lambda b, h, i: (b, 0, 2 * heads + h)),
                  pl.BlockSpec((1, T, LANES), lambda b, h, i: (b, 0, 0)),
                  pl.BlockSpec((1, 1, nq, 1, tq), lambda b, h, i: (b, h, 0, 0, 0))],
        out_specs=pl.BlockSpec((1, tq, FOX_HEAD), lambda b, h, i: (b, i, h)),
        out_shape=jax.ShapeDtypeStruct((B, T, heads * FOX_HEAD), BF16),
        scratch_shapes=[pltpu.VMEM((FOX_HEAD, T), BF16),
                        pltpu.VMEM((T, LANES), F32)],
        compiler_params=_params(("parallel", "parallel", "arbitrary"), est),
        name="fox_attention",
    )(qkv, qkv, qkv, f, frow)


def _xattn_body(q_ref, kv_ref, o_ref, *, d_model):
    hd = d_model // XA_HEADS
    for h in range(XA_HEADS):
        q = q_ref[0, :, h * hd:(h + 1) * hd]
        k = kv_ref[0, :, h * hd:(h + 1) * hd]
        v = kv_ref[0, :, d_model + h * hd:d_model + (h + 1) * hd]
        s = lax.dot_general(q, k, (((1,), (1,)), ((), ())), preferred_element_type=F32)
        p = jnp.exp(s - jnp.max(s, axis=-1, keepdims=True))
        p = p / jnp.sum(p, axis=-1, keepdims=True)
        o = jnp.dot(p.astype(BF16), v, preferred_element_type=F32)
        o_ref[0, :, h * hd:(h + 1) * hd] = o.astype(o_ref.dtype)


def _xattn_core(q, kv, tq=512):
    B, T, D = q.shape
    M = kv.shape[1]
    tq = min(tq, T)
    est = 2 * (2 * _nbytes((tq, D), BF16) + _nbytes((M, 2 * D), BF16)) + 6 * _nbytes((tq, M), F32) \
        + 2 * _nbytes((tq, D // XA_HEADS), F32)
    return pl.pallas_call(
        functools.partial(_xattn_body, d_model=D),
        grid=(B, T // tq),
        in_specs=[pl.BlockSpec((1, tq, D), lambda b, t: (b, t, 0)),
                  pl.BlockSpec((1, M, 2 * D), lambda b, t: (b, 0, 0))],
        out_specs=pl.BlockSpec((1, tq, D), lambda b, t: (b, t, 0)),
        out_shape=jax.ShapeDtypeStruct((B, T, D), BF16),
        compiler_params=_params(("parallel", "parallel"), est),
        name="xattn_core",
    )(q, kv)


def kernel(x, mem, lb_table, ev_norm, ev_w_in, ev_w_pool, ev_pool_scale, ev_hg_norm, ev_w_out,
           od_norm, od_w_in, od_b_f, od_w_out, xa_norm, xa_mem_norm, xa_wq, xa_wkv, xa_wo,
           ffn_norm, ffn_w_gate, ffn_w_up, ffn_w_down, final_norm):
    B, T, D = x.shape
    M = B * T
    depth = xa_norm.shape[0]
    mix_a = ev_pool_scale.shape[1]
    mix_b = lb_table.shape[1]
    fox_heads = od_b_f.shape[1]
    n_mem = mem.shape[1]
    xa_scale = (D // XA_HEADS) ** -0.5
    fox_tq = _pick_tile(T, 512)

    xs = x.reshape(M, D)
    mem2 = mem.reshape(B * n_mem, D)
    for l in range(depth):
        if l % 2 == 0:
            e = l // 2
            h = _rmsnorm(xs, ev_norm[e], BF16)
            z = _matmul(h, ev_w_in[e].astype(BF16), F32).reshape(B, T, -1)
            ya = _pool_mixer(z, ev_w_pool[e].astype(BF16), ev_pool_scale[e], mix_a)
            yb = _hgrn_mixer(z, lb_table, ev_hg_norm[e], l, mix_a, mix_b)
            y = jnp.concatenate([ya, yb], axis=-1).reshape(M, D)
            xs = _matmul_residual(y, ev_w_out[e].astype(BF16), xs)
        else:
            o = l // 2
            h = _rmsnorm(xs, od_norm[o], BF16)
            w_in = od_w_in[o]
            qscale = jnp.concatenate([jnp.full((D,), LOG2E * FOX_HEAD ** -0.5, F32), jnp.ones((2 * D,), F32)])
            qkv = _matmul(h, w_in[:, :3 * D].astype(BF16), BF16, colscale=qscale).reshape(B, T, 3 * D)
            wf = jnp.zeros((D, LANES), BF16).at[:, :fox_heads].set(w_in[:, 3 * D:].astype(BF16))
            bf = jnp.zeros((1, LANES), F32).at[0, :fox_heads].set(od_b_f[o].astype(F32))
            f = _fox_gates(h.reshape(B, T, D), wf, bf)
            frow = f[..., :fox_heads].transpose(0, 2, 1).reshape(B, fox_heads, T // fox_tq, 1, fox_tq)
            y = _fox_attention(qkv, f, frow, fox_heads, fox_tq).reshape(M, D)
            xs = _matmul_residual(y, od_w_out[o].astype(BF16), xs)

        h = _rmsnorm(xs, xa_norm[l], BF16)
        mn = _rmsnorm(mem2, xa_mem_norm[l], BF16)
        q = _matmul(h, xa_wq[l].astype(BF16), BF16, colscale=jnp.full((D,), xa_scale, F32))
        kv = _matmul(mn, xa_wkv[l].astype(BF16), BF16)
        a = _xattn_core(q.reshape(B, T, D), kv.reshape(B, n_mem, 2 * D)).reshape(M, D)
        xs = _matmul_residual(a, xa_wo[l].astype(BF16), xs)

        h = _rmsnorm(xs, ffn_norm[l], BF16)
        act = _matmul_swiglu(h, ffn_w_gate[l].astype(BF16), ffn_w_up[l].astype(BF16))
        xs = _matmul_residual(act, ffn_w_down[l].astype(BF16), xs)
    return _rmsnorm(xs, final_norm, x.dtype).reshape(B, T, D)
```

```python
import functools

import jax
import jax.numpy as jnp
from jax import lax
from jax.experimental import pallas as pl
from jax.experimental.pallas import tpu as pltpu

F32 = jnp.float32
BF16 = jnp.bfloat16

EPS = 1e-6
N_MEM = 256
POOL_WINDOWS = (2, 4, 8, 16)
POOL_HALO = 16
HG_HEAD = 128
HG_CHUNK = 64
HG_DIAG = 8
HG_SUB = 256
FOX_HEAD = 128
XA_HEADS = 4

LOG2E = 1.4426950408889634
LANES = 128
VMEM_CAP = 56 * 1024 * 1024
VMEM_FLOOR = 32 * 1024 * 1024


def _params(semantics, vmem_estimate):
    limit = int(min(max(vmem_estimate * 5 // 4, VMEM_FLOOR), VMEM_CAP))
    return pltpu.CompilerParams(dimension_semantics=semantics, vmem_limit_bytes=limit)


def _nbytes(shape, dtype):
    n = jnp.dtype(dtype).itemsize
    for s in shape:
        n *= s
    return n


def _rmsnorm_body(x_ref, g_ref, o_ref):
    x = x_ref[...]
    ms = jnp.mean(x * x, axis=-1, keepdims=True)
    o_ref[...] = (x * lax.rsqrt(ms + EPS) * g_ref[...]).astype(o_ref.dtype)


def _rmsnorm(x, g, out_dtype, tm=512):
    M, D = x.shape
    tm = min(tm, M)
    est = 2 * (_nbytes((tm, D), x.dtype) + _nbytes((tm, D), out_dtype)) + 3 * _nbytes((tm, D), F32)
    return pl.pallas_call(
        _rmsnorm_body,
        grid=(M // tm,),
        in_specs=[pl.BlockSpec((tm, D), lambda i: (i, 0)),
                  pl.BlockSpec((1, D), lambda i: (0, 0))],
        out_specs=pl.BlockSpec((tm, D), lambda i: (i, 0)),
        out_shape=jax.ShapeDtypeStruct((M, D), out_dtype),
        compiler_params=_params(("parallel",), est),
        name="rmsnorm",
    )(x, g.reshape(1, D).astype(F32))


def _resident_weight(w_ref, wb_refs):
    if not wb_refs:
        return w_ref[...]
    wb_ref, = wb_refs

    @pl.when(pl.program_id(1) == 0)
    def _():
        wb_ref[...] = w_ref[...].astype(BF16)

    return wb_ref[...]


def _mm_body(x_ref, w_ref, cs_ref, o_ref, *wb):
    acc = jnp.dot(x_ref[...], _resident_weight(w_ref, wb), preferred_element_type=F32)
    o_ref[...] = (acc * cs_ref[...]).astype(o_ref.dtype)


def _mm_res_body(x_ref, w_ref, r_ref, o_ref, *wb):
    acc = jnp.dot(x_ref[...], _resident_weight(w_ref, wb), preferred_element_type=F32)
    o_ref[...] = (r_ref[...] + acc).astype(o_ref.dtype)


def _mm_swiglu_body(x_ref, wg_ref, wu_ref, o_ref, *wb):
    x = x_ref[...]
    g = jnp.dot(x, _resident_weight(wg_ref, wb[:1]), preferred_element_type=F32)
    u = jnp.dot(x, _resident_weight(wu_ref, wb[1:]), preferred_element_type=F32)
    o_ref[...] = (g * jax.nn.sigmoid(g) * u).astype(o_ref.dtype)


def _pick_tile(n, pref):
    t = min(pref, n)
    while n % t:
        t //= 2
    return t


def _weight_spec(layer, K, tn):
    return pl.BlockSpec((None, K, tn), lambda j, i: (layer, 0, j))


def _weight_scratch(w, K, tn):
    return [] if w.dtype == BF16 else [pltpu.VMEM((K, tn), BF16)]


def _weight_bytes(w, K, tn):
    return 2 * _nbytes((K, tn), w.dtype) + (0 if w.dtype == BF16 else _nbytes((K, tn), BF16))


def _matmul(x, w, layer, n_cols, out_dtype, colscale=None, tm=1024, tn=1024):
    M, K = x.shape
    tm, tn = _pick_tile(M, tm), _pick_tile(n_cols, tn)
    if colscale is None:
        colscale = jnp.ones((n_cols,), F32)
    est = 2 * (_nbytes((tm, K), x.dtype) + _nbytes((tm, tn), out_dtype)) + _weight_bytes(w, K, tn) \
        + 2 * _nbytes((tm, tn), F32)
    return pl.pallas_call(
        _mm_body,
        grid=(n_cols // tn, M // tm),
        in_specs=[pl.BlockSpec((tm, K), lambda j, i: (i, 0)),
                  _weight_spec(layer, K, tn),
                  pl.BlockSpec((1, tn), lambda j, i: (0, j))],
        out_specs=pl.BlockSpec((tm, tn), lambda j, i: (i, j)),
        out_shape=jax.ShapeDtypeStruct((M, n_cols), out_dtype),
        scratch_shapes=_weight_scratch(w, K, tn),
        compiler_params=_params(("parallel", "arbitrary"), est),
        name="matmul",
    )(x, w, colscale.reshape(1, n_cols))


def _matmul_residual(x, w, layer, r, tn=1024):
    M, K = x.shape
    N = w.shape[2]
    tm = 1024 if K <= 2048 else 512
    tm, tn = _pick_tile(M, tm), _pick_tile(N, tn)
    est = 2 * (_nbytes((tm, K), x.dtype) + 2 * _nbytes((tm, tn), F32)) + _weight_bytes(w, K, tn) \
        + 2 * _nbytes((tm, tn), F32)
    return pl.pallas_call(
        _mm_res_body,
        grid=(N // tn, M // tm),
        in_specs=[pl.BlockSpec((tm, K), lambda j, i: (i, 0)),
                  _weight_spec(layer, K, tn),
                  pl.BlockSpec((tm, tn), lambda j, i: (i, j))],
        out_specs=pl.BlockSpec((tm, tn), lambda j, i: (i, j)),
        out_shape=jax.ShapeDtypeStruct((M, N), F32),
        scratch_shapes=_weight_scratch(w, K, tn),
        compiler_params=_params(("parallel", "arbitrary"), est),
        name="matmul_residual",
    )(x, w, r)


def _matmul_swiglu(x, wg, wu, layer, tm=1024, tn=512):
    M, K = x.shape
    N = wg.shape[2]
    tm, tn = _pick_tile(M, tm), _pick_tile(N, tn)
    est = 2 * (_nbytes((tm, K), x.dtype) + _nbytes((tm, tn), BF16)) + 2 * _weight_bytes(wg, K, tn) \
        + 4 * _nbytes((tm, tn), F32)
    return pl.pallas_call(
        _mm_swiglu_body,
        grid=(N // tn, M // tm),
        in_specs=[pl.BlockSpec((tm, K), lambda j, i: (i, 0)),
                  _weight_spec(layer, K, tn),
                  _weight_spec(layer, K, tn)],
        out_specs=pl.BlockSpec((tm, tn), lambda j, i: (i, j)),
        out_shape=jax.ShapeDtypeStruct((M, N), BF16),
        scratch_shapes=_weight_scratch(wg, K, tn) + _weight_scratch(wu, K, tn),
        compiler_params=_params(("parallel", "arbitrary"), est),
        name="matmul_swiglu",
    )(x, wg, wu)


def _pool_body(u_ref, halo_ref, w_ref, sc_ref, o_ref, ext_ref, *, tt, group):
    t = pl.program_id(1)
    u = u_ref[0]
    ext_ref[0:POOL_HALO, :] = jnp.where(t > 0, halo_ref[0], 0.0)
    ext_ref[POOL_HALO:POOL_HALO + tt, :] = u
    pos = t * tt + lax.broadcasted_iota(jnp.int32, (tt, 1), 0)
    for gi, win in enumerate(POOL_WINDOWS):
        c0, c1 = gi * group, (gi + 1) * group
        tok = u[:, c0:c1]
        acc = tok
        for d in range(1, win):
            acc = acc + ext_ref[POOL_HALO - d:POOL_HALO - d + tt, c0:c1]
        cnt = jnp.minimum(pos + 1, win).astype(F32)
        p = acc / cnt - tok
        y = jnp.dot(p.astype(BF16), w_ref[gi], preferred_element_type=F32)
        o_ref[0, :, c0:c1] = (y * sc_ref[:, c0:c1]).astype(o_ref.dtype)


def _pool_mixer(z, w_pool, scale, mix_a, tt=512):
    B, T, _ = z.shape
    tt = min(tt, T)
    group = mix_a // len(POOL_WINDOWS)
    halo_blocks = tt // POOL_HALO
    est = 2 * (_nbytes((tt, mix_a), F32) + _nbytes((tt, mix_a), BF16)) + 4 * _nbytes((tt, mix_a), F32)
    return pl.pallas_call(
        functools.partial(_pool_body, tt=tt, group=group),
        grid=(B, T // tt),
        in_specs=[pl.BlockSpec((1, tt, mix_a), lambda b, t: (b, t, 0)),
                  pl.BlockSpec((1, POOL_HALO, mix_a),
                               lambda b, t: (b, jnp.maximum(t * halo_blocks - 1, 0), 0)),
                  pl.BlockSpec((len(POOL_WINDOWS), group, group), lambda b, t: (0, 0, 0)),
                  pl.BlockSpec((1, mix_a), lambda b, t: (0, 0))],
        out_specs=pl.BlockSpec((1, tt, mix_a), lambda b, t: (b, t, 0)),
        out_shape=jax.ShapeDtypeStruct((B, T, mix_a), BF16),
        scratch_shapes=[pltpu.VMEM((tt + POOL_HALO, mix_a), F32)],
        compiler_params=_params(("parallel", "parallel"), est),
        name="pool_mixer",
    )(z, z, w_pool, scale.reshape(1, mix_a).astype(F32))


def _hgrn_subtile(qf, kk, v, logf, st):
    n = qf.shape[0]
    C, R = HG_CHUNK, HG_DIAG
    row = lax.broadcasted_iota(jnp.int32, (n, HG_HEAD), 0)

    pos = row & (C - 1)
    b = logf
    sh = 1
    while sh < C:
        b = b + jnp.where(pos >= sh, pltpu.roll(b, sh, axis=0), 0.0)
        sh *= 2

    nb = n // R
    b3, q3, k3, v3 = (a.reshape(nb, R, HG_HEAD) for a in (b, qf, kk, v))
    tpos = lax.broadcasted_iota(jnp.int32, (1, R, 1), 1)
    od = jnp.zeros((nb, R, HG_HEAD), F32)
    for s in range(R):
        w = q3 * jnp.exp(b3 - b3[:, s:s + 1, :]) * k3[:, s:s + 1, :]
        a = jnp.sum(w, axis=-1, keepdims=True)
        a = jnp.where(tpos >= s, a, 0.0)
        od = od + a * v3[:, s:s + 1, :]
    o = od.reshape(n, HG_HEAD)

    same = lax.broadcasted_iota(jnp.int32, (n, n), 0) ^ lax.broadcasted_iota(jnp.int32, (n, n), 1)
    a_off = jnp.zeros((n, n), F32)
    h = R
    while h < C:
        g = 2 * h
        ref = b.reshape(n // g, g, HG_HEAD)[:, h - 1:h, :]
        ref = jnp.broadcast_to(ref, (n // g, g, HG_HEAD)).reshape(n, HG_HEAD)
        e = jnp.exp(-jnp.abs(b - ref))
        right = (row & (g - 1)) >= h
        qt = jnp.where(right, qf * e, 0.0).astype(BF16)
        kt = jnp.where(right, 0.0, kk * e).astype(BF16)
        a = lax.dot_general(qt, kt, (((1,), (1,)), ((), ())), preferred_element_type=F32)
        a_off = a_off + jnp.where(same < g, a, 0.0)
        h = g
    v_bf = v.astype(BF16)
    o = o + jnp.dot(a_off.astype(BF16), v_bf, preferred_element_type=F32)

    nc = n // C
    bc = b.reshape(nc, C, HG_HEAD)
    b_last = bc[:, C - 1:C, :]
    q_in = (qf * jnp.exp(b)).astype(BF16)
    k_out = (kk.reshape(nc, C, HG_HEAD) * jnp.exp(b_last - bc)).astype(BF16)
    dec = jnp.exp(b_last)
    pieces = []
    for c in range(nc):
        rows = slice(c * C, (c + 1) * C)
        o_int = lax.dot_general(q_in[rows], st.astype(BF16), (((1,), (1,)), ((), ())),
                                preferred_element_type=F32)
        pieces.append(o[rows] + o_int)
        upd = lax.dot_general(v_bf[rows], k_out[c], (((0,), (0,)), ((), ())),
                              preferred_element_type=F32)
        st = st * dec[c] + upd
    return jnp.concatenate(pieces, axis=0), st


def _hgrn_body(lbt_ref, ng_ref, q_ref, f_ref, i_ref, g_ref, o_ref, st_ref, *, layer, tt, sub):
    t = pl.program_id(2)

    @pl.when(t == 0)
    def _():
        st_ref[...] = jnp.zeros_like(st_ref)

    lbt = lbt_ref[...]
    e = jnp.exp(lbt - jnp.max(lbt, axis=0, keepdims=True))
    sm = e / jnp.sum(e, axis=0, keepdims=True)
    lb = jnp.sum(sm[1:layer + 2], axis=0, keepdims=True)

    st = st_ref[...]
    for c in range(tt // sub):
        rows = slice(c * sub, (c + 1) * sub)
        f = lb + (1.0 - lb) * jax.nn.sigmoid(f_ref[0, rows, :])
        q = q_ref[0, rows, :]
        qf = q * jax.nn.sigmoid(q) * (HG_HEAD ** -0.5)
        o, st = _hgrn_subtile(qf, 1.0 - f, i_ref[0, rows, :], jnp.log(f), st)
        ms = jnp.mean(o * o, axis=-1, keepdims=True)
        o = o * lax.rsqrt(ms + EPS) * ng_ref[...]
        g = g_ref[0, rows, :]
        o_ref[0, rows, :] = (o * (g * jax.nn.sigmoid(g))).astype(o_ref.dtype)
    st_ref[...] = st


def _hgrn_mixer(z, lb_table, norm_g, layer, mix_a, mix_b, tt=512):
    B, T, _ = z.shape
    tt = min(tt, T)
    sub = min(HG_SUB, tt)
    heads = mix_b // HG_HEAD
    c0 = mix_a // HG_HEAD
    est = 2 * 5 * _nbytes((tt, HG_HEAD), F32) + 32 * _nbytes((sub, HG_HEAD), F32) + 6 * _nbytes((sub, sub), F32)

    def col(off):
        return pl.BlockSpec((1, tt, HG_HEAD), lambda b, h, t, off=off: (b, t, c0 + off * heads + h))

    return pl.pallas_call(
        functools.partial(_hgrn_body, layer=layer, tt=tt, sub=sub),
        grid=(B, heads, T // tt),
        in_specs=[pl.BlockSpec((lb_table.shape[0], HG_HEAD), lambda b, h, t: (0, h)),
                  pl.BlockSpec((1, HG_HEAD), lambda b, h, t: (0, 0)),
                  col(0), col(1), col(2), col(3)],
        out_specs=pl.BlockSpec((1, tt, HG_HEAD), lambda b, h, t: (b, t, h)),
        out_shape=jax.ShapeDtypeStruct((B, T, mix_b), BF16),
        scratch_shapes=[pltpu.VMEM((HG_HEAD, HG_HEAD), F32)],
        compiler_params=_params(("parallel", "parallel", "arbitrary"), est),
        name="hgrn2_mixer",
    )(lb_table.astype(F32), norm_g.reshape(1, HG_HEAD).astype(F32), z, z, z, z)


def _fgate_body(h_ref, w_ref, b_ref, o_ref, carry_ref, *, tt):
    t = pl.program_id(1)

    @pl.when(t == 0)
    def _():
        carry_ref[...] = jnp.zeros_like(carry_ref)

    fl = jnp.dot(h_ref[0], w_ref[...], preferred_element_type=F32) + b_ref[...]
    c = (jnp.minimum(fl, 0.0) - jnp.log1p(jnp.exp(-jnp.abs(fl)))) * LOG2E
    row = lax.broadcasted_iota(jnp.int32, c.shape, 0)
    sh = 1
    while sh < tt:
        c = c + jnp.where(row >= sh, pltpu.roll(c, sh, axis=0), 0.0)
        sh *= 2
    c = c + carry_ref[...]
    o_ref[0] = c
    carry_ref[...] = c[tt - 1:tt, :]


def _fox_gates(h, wf, bf, tt=512):
    B, T, D = h.shape
    tt = min(tt, T)
    est = 2 * (_nbytes((tt, D), BF16) + _nbytes((D, LANES), BF16) + _nbytes((tt, LANES), F32)) \
        + 8 * _nbytes((tt, LANES), F32)
    return pl.pallas_call(
        functools.partial(_fgate_body, tt=tt),
        grid=(B, T // tt),
        in_specs=[pl.BlockSpec((1, tt, D), lambda b, t: (b, t, 0)),
                  pl.BlockSpec((D, LANES), lambda b, t: (0, 0)),
                  pl.BlockSpec((1, LANES), lambda b, t: (0, 0))],
        out_specs=pl.BlockSpec((1, tt, LANES), lambda b, t: (b, t, 0)),
        out_shape=jax.ShapeDtypeStruct((B, T, LANES), F32),
        scratch_shapes=[pltpu.VMEM((1, LANES), F32)],
        compiler_params=_params(("parallel", "arbitrary"), est),
        name="fox_gates",
    )(h, wf, bf)


def _fox_body(q_ref, k_ref, v_ref, f_ref, fr_ref, o_ref, vt_ref, fk_ref, p_ref, acc_ref, *, tq, seq):
    h = pl.program_id(1)
    i = pl.program_id(2)
    reps = tq // LANES

    @pl.when(i == 0)
    def _():
        for c in range(seq // tq):
            rows = slice(c * tq, (c + 1) * tq)
            vt_ref[:, rows] = v_ref[0, rows, :].astype(F32).T.astype(BF16)
            fblk = f_ref[0, rows, :]
            lane = lax.broadcasted_iota(jnp.int32, fblk.shape, 1)
            col = jnp.sum(jnp.where(lane == h, fblk, 0.0), axis=-1, keepdims=True)
            fk_ref[rows, :] = jnp.broadcast_to(col, fblk.shape)

    qt = q_ref[0].astype(F32).T.astype(BF16)
    fq = fr_ref[0, 0, i]

    def scores(j):
        start = pl.multiple_of(j * tq, tq)
        return jnp.dot(k_ref[0, pl.ds(start, tq), :], qt, preferred_element_type=F32)

    def softmax_block(j, st, m_prev, l_prev, diagonal):
        start = pl.multiple_of(j * tq, tq)
        t = st - jnp.concatenate([fk_ref[pl.ds(start, tq), :]] * reps, axis=1)
        if diagonal:
            key = lax.broadcasted_iota(jnp.int32, (tq, tq), 0)
            qry = lax.broadcasted_iota(jnp.int32, (tq, tq), 1)
            t = jnp.where(key <= qry, t, -jnp.inf)
        m_new = jnp.maximum(m_prev, jnp.max(t, axis=0, keepdims=True) + fq)
        alpha = jnp.exp2(m_prev - m_new)
        p = jnp.exp2(t - (m_new - fq))
        l_new = alpha * l_prev + jnp.sum(p, axis=0, keepdims=True)
        return p.astype(BF16), m_new, l_new, alpha

    def weighted_values(j, p):
        start = pl.multiple_of(j * tq, tq)
        return jnp.dot(vt_ref[:, pl.ds(start, tq)], p, preferred_element_type=F32)

    def block(j, m, l, diagonal):
        pv_prev = weighted_values(jnp.maximum(j - 1, 0), p_ref[...])
        p, m, l, alpha = softmax_block(j, scores(j), m, l, diagonal)
        p_ref[...] = p
        acc_ref[...] = (acc_ref[...] + pv_prev) * alpha
        return m, l

    p_ref[...] = jnp.zeros_like(p_ref)
    acc_ref[...] = jnp.zeros_like(acc_ref)
    m = jnp.full((1, tq), -jnp.inf, F32)
    l = jnp.zeros((1, tq), F32)
    m, l = lax.fori_loop(0, i, lambda j, c: block(j, c[0], c[1], False), (m, l))
    m, l = block(i, m, l, True)
    acc = acc_ref[...] + weighted_values(i, p_ref[...])
    o_ref[0] = (acc / l).T.astype(o_ref.dtype)


def _fox_attention(qkv, f, frow, heads, tq):
    B, T, _ = qkv.shape
    nq = T // tq
    est = 2 * (2 * _nbytes((T, FOX_HEAD), BF16) + 2 * _nbytes((tq, FOX_HEAD), BF16)
               + _nbytes((T, LANES), F32) + _nbytes((nq, 8, tq), F32)) \
        + _nbytes((T, FOX_HEAD), BF16) + _nbytes((T, LANES), F32) + 8 * _nbytes((tq, tq), F32)
    return pl.pallas_call(
        functools.partial(_fox_body, tq=tq, seq=T),
        grid=(B, heads, nq),
        in_specs=[pl.BlockSpec((1, tq, FOX_HEAD), lambda b, h, i: (b, i, h)),
                  pl.BlockSpec((1, T, FOX_HEAD), lambda b, h, i: (b, 0, heads + h)),
                  pl.BlockSpec((1, T, FOX_HEAD), lambda b, h, i: (b, 0, 2 * heads + h)),
                  pl.BlockSpec((1, T, LANES), lambda b, h, i: (b, 0, 0)),
                  pl.BlockSpec((1, 1, nq, 1, tq), lambda b, h, i: (b, h, 0, 0, 0))],
        out_specs=pl.BlockSpec((1, tq, FOX_HEAD), lambda b, h, i: (b, i, h)),
        out_shape=jax.ShapeDtypeStruct((B, T, heads * FOX_HEAD), BF16),
        scratch_shapes=[pltpu.VMEM((FOX_HEAD, T), BF16),
                        pltpu.VMEM((T, LANES), F32),
                        pltpu.VMEM((tq, tq), BF16),
                        pltpu.VMEM((FOX_HEAD, tq), F32)],
        compiler_params=_params(("parallel", "parallel", "arbitrary"), est),
        name="fox_attention",
    )(qkv, qkv, qkv, f, frow)


def _xattn_body(q_ref, kv_ref, o_ref, *, d_model):
    hd = d_model // XA_HEADS
    for h in range(XA_HEADS):
        q = q_ref[0, :, h * hd:(h + 1) * hd]
        k = kv_ref[0, :, h * hd:(h + 1) * hd]
        v = kv_ref[0, :, d_model + h * hd:d_model + (h + 1) * hd]
        s = lax.dot_general(q, k, (((1,), (1,)), ((), ())), preferred_element_type=F32)
        p = jnp.exp(s - jnp.max(s, axis=-1, keepdims=True))
        p = p / jnp.sum(p, axis=-1, keepdims=True)
        o = jnp.dot(p.astype(BF16), v, preferred_element_type=F32)
        o_ref[0, :, h * hd:(h + 1) * hd] = o.astype(o_ref.dtype)


def _xattn_core(q, kv, tq=512):
    B, T, D = q.shape
    M = kv.shape[1]
    tq = min(tq, T)
    est = 2 * (2 * _nbytes((tq, D), BF16) + _nbytes((M, 2 * D), BF16)) + 6 * _nbytes((tq, M), F32) \
        + 2 * _nbytes((tq, D // XA_HEADS), F32)
    return pl.pallas_call(
        functools.partial(_xattn_body, d_model=D),
        grid=(B, T // tq),
        in_specs=[pl.BlockSpec((1, tq, D), lambda b, t: (b, t, 0)),
                  pl.BlockSpec((1, M, 2 * D), lambda b, t: (b, 0, 0))],
        out_specs=pl.BlockSpec((1, tq, D), lambda b, t: (b, t, 0)),
        out_shape=jax.ShapeDtypeStruct((B, T, D), BF16),
        compiler_params=_params(("parallel", "parallel"), est),
        name="xattn_core",
    )(q, kv)


def kernel(x, mem, lb_table, ev_norm, ev_w_in, ev_w_pool, ev_pool_scale, ev_hg_norm, ev_w_out,
           od_norm, od_w_in, od_b_f, od_w_out, xa_norm, xa_mem_norm, xa_wq, xa_wkv, xa_wo,
           ffn_norm, ffn_w_gate, ffn_w_up, ffn_w_down, final_norm):
    B, T, D = x.shape
    M = B * T
    depth = xa_norm.shape[0]
    mix_a = ev_pool_scale.shape[1]
    mix_b = lb_table.shape[1]
    fox_heads = od_b_f.shape[1]
    n_mem = mem.shape[1]
    xa_scale = (D // XA_HEADS) ** -0.5
    fox_tq = _pick_tile(T, 512)

    xs = x.reshape(M, D)
    mem2 = mem.reshape(B * n_mem, D)
    w_down = ffn_w_down.astype(BF16)
    for l in range(depth):
        if l % 2 == 0:
            e = l // 2
            h = _rmsnorm(xs, ev_norm[e], BF16)
            z = _matmul(h, ev_w_in, e, ev_w_in.shape[2], F32).reshape(B, T, -1)
            ya = _pool_mixer(z, ev_w_pool[e].astype(BF16), ev_pool_scale[e], mix_a)
            yb = _hgrn_mixer(z, lb_table, ev_hg_norm[e], l, mix_a, mix_b)
            y = jnp.concatenate([ya, yb], axis=-1).reshape(M, D)
            xs = _matmul_residual(y, ev_w_out, e, xs)
        else:
            o = l // 2
            h = _rmsnorm(xs, od_norm[o], BF16)
            qscale = jnp.concatenate([jnp.full((D,), LOG2E * FOX_HEAD ** -0.5, F32), jnp.ones((2 * D,), F32)])
            qkv = _matmul(h, od_w_in, o, 3 * D, BF16, colscale=qscale).reshape(B, T, 3 * D)
            wf = jnp.zeros((D, LANES), BF16).at[:, :fox_heads].set(od_w_in[o, :, 3 * D:].astype(BF16))
            bf = jnp.zeros((1, LANES), F32).at[0, :fox_heads].set(od_b_f[o].astype(F32))
            f = _fox_gates(h.reshape(B, T, D), wf, bf)
            frow = f[..., :fox_heads].transpose(0, 2, 1).reshape(B, fox_heads, T // fox_tq, 1, fox_tq)
            y = _fox_attention(qkv, f, frow, fox_heads, fox_tq).reshape(M, D)
            xs = _matmul_residual(y, od_w_out, o, xs)

        h = _rmsnorm(xs, xa_norm[l], BF16)
        mn = _rmsnorm(mem2, xa_mem_norm[l], BF16)
        q = _matmul(h, xa_wq, l, D, BF16, colscale=jnp.full((D,), xa_scale, F32))
        kv = _matmul(mn, xa_wkv, l, 2 * D, BF16)
        a = _xattn_core(q.reshape(B, T, D), kv.reshape(B, n_mem, 2 * D)).reshape(M, D)
        xs = _matmul_residual(a, xa_wo, l, xs)

        h = _rmsnorm(xs, ffn_norm[l], BF16)
        act = _matmul_swiglu(h, ffn_w_gate, ffn_w_up, l)
        xs = _matmul_residual(act, w_down, l, xs)
    return _rmsnorm(xs, final_norm, x.dtype).reshape(B, T, D)
```

```python
import functools

import jax
import jax.numpy as jnp
from jax import lax
from jax.experimental import pallas as pl
from jax.experimental.pallas import tpu as pltpu

F32 = jnp.float32
BF16 = jnp.bfloat16

EPS = 1e-6
N_MEM = 256
POOL_WINDOWS = (2, 4, 8, 16)
POOL_HALO = 16
HG_HEAD = 128
HG_CHUNK = 64
HG_DIAG = 8
HG_SUB = 256
FOX_HEAD = 128
FOX_GROUP = 4
XA_HEADS = 4

LOG2E = 1.4426950408889634
LANES = 128
VMEM_CAP = 56 * 1024 * 1024
VMEM_FLOOR = 32 * 1024 * 1024


def _params(semantics, vmem_estimate):
    limit = int(min(max(vmem_estimate * 5 // 4, VMEM_FLOOR), VMEM_CAP))
    return pltpu.CompilerParams(dimension_semantics=semantics, vmem_limit_bytes=limit)


def _nbytes(shape, dtype):
    n = jnp.dtype(dtype).itemsize
    for s in shape:
        n *= s
    return n


def _rmsnorm_body(x_ref, g_ref, o_ref):
    x = x_ref[...]
    ms = jnp.mean(x * x, axis=-1, keepdims=True)
    o_ref[...] = (x * lax.rsqrt(ms + EPS) * g_ref[...]).astype(o_ref.dtype)


def _rmsnorm(x, g, out_dtype, tm=512):
    M, D = x.shape
    tm = min(tm, M)
    est = 2 * (_nbytes((tm, D), x.dtype) + _nbytes((tm, D), out_dtype)) + 3 * _nbytes((tm, D), F32)
    return pl.pallas_call(
        _rmsnorm_body,
        grid=(M // tm,),
        in_specs=[pl.BlockSpec((tm, D), lambda i: (i, 0)),
                  pl.BlockSpec((1, D), lambda i: (0, 0))],
        out_specs=pl.BlockSpec((tm, D), lambda i: (i, 0)),
        out_shape=jax.ShapeDtypeStruct((M, D), out_dtype),
        compiler_params=_params(("parallel",), est),
        name="rmsnorm",
    )(x, g.reshape(1, D).astype(F32))


def _with_bf16_weights(w_refs, wb_refs, compute):
    if not wb_refs:
        compute(*(w[...] for w in w_refs))
        return
    i = pl.program_id(1)

    @pl.when(i == 0)
    def _():
        for w, wb in zip(w_refs, wb_refs):
            wb[...] = w[...].astype(BF16)

    @pl.when(i > 0)
    def _():
        compute(*(wb[...] for wb in wb_refs))


def _mm_body(x_ref, w_ref, cs_ref, o_ref, *wb):
    def compute(w):
        acc = jnp.dot(x_ref[...], w, preferred_element_type=F32)
        o_ref[...] = (acc * cs_ref[...]).astype(o_ref.dtype)

    _with_bf16_weights((w_ref,), wb, compute)


def _mm_res_body(x_ref, w_ref, r_ref, o_ref, *wb):
    def compute(w):
        acc = jnp.dot(x_ref[...], w, preferred_element_type=F32)
        o_ref[...] = (r_ref[...] + acc).astype(o_ref.dtype)

    _with_bf16_weights((w_ref,), wb, compute)


def _mm_swiglu_body(x_ref, wg_ref, wu_ref, o_ref, *wb):
    def compute(wg, wu):
        x = x_ref[...]
        g = jnp.dot(x, wg, preferred_element_type=F32)
        u = jnp.dot(x, wu, preferred_element_type=F32)
        o_ref[...] = (g * jax.nn.sigmoid(g) * u).astype(o_ref.dtype)

    _with_bf16_weights((wg_ref, wu_ref), wb, compute)


def _pick_tile(n, pref):
    t = min(pref, n)
    while n % t:
        t //= 2
    return t


def _weight_spec(layer, K, tn):
    return pl.BlockSpec((None, K, tn), lambda j, i: (layer, 0, j))


def _weight_scratch(w, K, tn):
    return [] if w.dtype == BF16 else [pltpu.VMEM((K, tn), BF16)]


def _weight_bytes(w, K, tn):
    return 2 * _nbytes((K, tn), w.dtype) + (0 if w.dtype == BF16 else _nbytes((K, tn), BF16))


def _row_steps(w, M, tm):
    extra = 0 if w.dtype == BF16 else 1
    return M // tm + extra, lambda i: jnp.maximum(i - extra, 0)


def _matmul(x, w, layer, n_cols, out_dtype, colscale=None, tm=1024, tn=1024):
    M, K = x.shape
    tm, tn = _pick_tile(M, tm), _pick_tile(n_cols, tn)
    if colscale is None:
        colscale = jnp.ones((n_cols,), F32)
    steps, row = _row_steps(w, M, tm)
    est = 2 * (_nbytes((tm, K), x.dtype) + _nbytes((tm, tn), out_dtype)) + _weight_bytes(w, K, tn) \
        + 2 * _nbytes((tm, tn), F32)
    return pl.pallas_call(
        _mm_body,
        grid=(n_cols // tn, steps),
        in_specs=[pl.BlockSpec((tm, K), lambda j, i: (row(i), 0)),
                  _weight_spec(layer, K, tn),
                  pl.BlockSpec((1, tn), lambda j, i: (0, j))],
        out_specs=pl.BlockSpec((tm, tn), lambda j, i: (row(i), j)),
        out_shape=jax.ShapeDtypeStruct((M, n_cols), out_dtype),
        scratch_shapes=_weight_scratch(w, K, tn),
        compiler_params=_params(("parallel", "arbitrary"), est),
        name="matmul",
    )(x, w, colscale.reshape(1, n_cols))


def _matmul_residual(x, w, layer, r, tn=1024):
    M, K = x.shape
    N = w.shape[2]
    tm = 1024 if K <= 2048 else 512
    tm, tn = _pick_tile(M, tm), _pick_tile(N, tn)
    steps, row = _row_steps(w, M, tm)
    est = 2 * (_nbytes((tm, K), x.dtype) + 2 * _nbytes((tm, tn), F32)) + _weight_bytes(w, K, tn) \
        + 2 * _nbytes((tm, tn), F32)
    return pl.pallas_call(
        _mm_res_body,
        grid=(N // tn, steps),
        in_specs=[pl.BlockSpec((tm, K), lambda j, i: (row(i), 0)),
                  _weight_spec(layer, K, tn),
                  pl.BlockSpec((tm, tn), lambda j, i: (row(i), j))],
        out_specs=pl.BlockSpec((tm, tn), lambda j, i: (row(i), j)),
        out_shape=jax.ShapeDtypeStruct((M, N), F32),
        scratch_shapes=_weight_scratch(w, K, tn),
        compiler_params=_params(("parallel", "arbitrary"), est),
        name="matmul_residual",
    )(x, w, r)


def _matmul_swiglu(x, wg, wu, layer, tm=1024, tn=512):
    M, K = x.shape
    N = wg.shape[2]
    tm, tn = _pick_tile(M, tm), _pick_tile(N, tn)
    steps, row = _row_steps(wg, M, tm)
    est = 2 * (_nbytes((tm, K), x.dtype) + _nbytes((tm, tn), BF16)) + 2 * _weight_bytes(wg, K, tn) \
        + 4 * _nbytes((tm, tn), F32)
    return pl.pallas_call(
        _mm_swiglu_body,
        grid=(N // tn, steps),
        in_specs=[pl.BlockSpec((tm, K), lambda j, i: (row(i), 0)),
                  _weight_spec(layer, K, tn),
                  _weight_spec(layer, K, tn)],
        out_specs=pl.BlockSpec((tm, tn), lambda j, i: (row(i), j)),
        out_shape=jax.ShapeDtypeStruct((M, N), BF16),
        scratch_shapes=_weight_scratch(wg, K, tn) + _weight_scratch(wu, K, tn),
        compiler_params=_params(("parallel", "arbitrary"), est),
        name="matmul_swiglu",
    )(x, wg, wu)


def _pool_body(u_ref, halo_ref, w_ref, sc_ref, o_ref, ext_ref, *, tt, group):
    t = pl.program_id(1)
    u = u_ref[0]
    ext_ref[0:POOL_HALO, :] = jnp.where(t > 0, halo_ref[0], 0.0)
    ext_ref[POOL_HALO:POOL_HALO + tt, :] = u
    pos = t * tt + lax.broadcasted_iota(jnp.int32, (tt, 1), 0)
    for gi, win in enumerate(POOL_WINDOWS):
        c0, c1 = gi * group, (gi + 1) * group
        tok = u[:, c0:c1]
        acc = tok
        for d in range(1, win):
            acc = acc + ext_ref[POOL_HALO - d:POOL_HALO - d + tt, c0:c1]
        cnt = jnp.minimum(pos + 1, win).astype(F32)
        p = acc / cnt - tok
        y = jnp.dot(p.astype(BF16), w_ref[gi], preferred_element_type=F32)
        o_ref[0, :, c0:c1] = (y * sc_ref[:, c0:c1]).astype(o_ref.dtype)


def _pool_mixer(z, w_pool, scale, mix_a, tt=512):
    B, T, _ = z.shape
    tt = min(tt, T)
    group = mix_a // len(POOL_WINDOWS)
    halo_blocks = tt // POOL_HALO
    est = 2 * (_nbytes((tt, mix_a), F32) + _nbytes((tt, mix_a), BF16)) + 4 * _nbytes((tt, mix_a), F32)
    return pl.pallas_call(
        functools.partial(_pool_body, tt=tt, group=group),
        grid=(B, T // tt),
        in_specs=[pl.BlockSpec((1, tt, mix_a), lambda b, t: (b, t, 0)),
                  pl.BlockSpec((1, POOL_HALO, mix_a),
                               lambda b, t: (b, jnp.maximum(t * halo_blocks - 1, 0), 0)),
                  pl.BlockSpec((len(POOL_WINDOWS), group, group), lambda b, t: (0, 0, 0)),
                  pl.BlockSpec((1, mix_a), lambda b, t: (0, 0))],
        out_specs=pl.BlockSpec((1, tt, mix_a), lambda b, t: (b, t, 0)),
        out_shape=jax.ShapeDtypeStruct((B, T, mix_a), BF16),
        scratch_shapes=[pltpu.VMEM((tt + POOL_HALO, mix_a), F32)],
        compiler_params=_params(("parallel", "parallel"), est),
        name="pool_mixer",
    )(z, z, w_pool, scale.reshape(1, mix_a).astype(F32))


def _hgrn_subtile(qf, kk, v, logf, st):
    n = qf.shape[0]
    C, R = HG_CHUNK, HG_DIAG
    row = lax.broadcasted_iota(jnp.int32, (n, HG_HEAD), 0)

    pos = row & (C - 1)
    b = logf
    sh = 1
    while sh < C:
        b = b + jnp.where(pos >= sh, pltpu.roll(b, sh, axis=0), 0.0)
        sh *= 2

    nb = n // R
    b3, q3, k3, v3 = (a.reshape(nb, R, HG_HEAD) for a in (b, qf, kk, v))
    tpos = lax.broadcasted_iota(jnp.int32, (1, R, 1), 1)
    od = jnp.zeros((nb, R, HG_HEAD), F32)
    for s in range(R):
        w = q3 * jnp.exp(b3 - b3[:, s:s + 1, :]) * k3[:, s:s + 1, :]
        a = jnp.sum(w, axis=-1, keepdims=True)
        a = jnp.where(tpos >= s, a, 0.0)
        od = od + a * v3[:, s:s + 1, :]
    o = od.reshape(n, HG_HEAD)

    same = lax.broadcasted_iota(jnp.int32, (n, n), 0) ^ lax.broadcasted_iota(jnp.int32, (n, n), 1)
    a_off = jnp.zeros((n, n), F32)
    h = R
    while h < C:
        g = 2 * h
        ref = b.reshape(n // g, g, HG_HEAD)[:, h - 1:h, :]
        ref = jnp.broadcast_to(ref, (n // g, g, HG_HEAD)).reshape(n, HG_HEAD)
        e = jnp.exp(-jnp.abs(b - ref))
        right = (row & (g - 1)) >= h
        qt = jnp.where(right, qf * e, 0.0).astype(BF16)
        kt = jnp.where(right, 0.0, kk * e).astype(BF16)
        a = lax.dot_general(qt, kt, (((1,), (1,)), ((), ())), preferred_element_type=F32)
        a_off = a_off + jnp.where(same < g, a, 0.0)
        h = g
    v_bf = v.astype(BF16)
    o = o + jnp.dot(a_off.astype(BF16), v_bf, preferred_element_type=F32)

    nc = n // C
    bc = b.reshape(nc, C, HG_HEAD)
    b_last = bc[:, C - 1:C, :]
    q_in = (qf * jnp.exp(b)).astype(BF16)
    k_out = (kk.reshape(nc, C, HG_HEAD) * jnp.exp(b_last - bc)).astype(BF16)
    dec = jnp.exp(b_last)
    pieces = []
    for c in range(nc):
        rows = slice(c * C, (c + 1) * C)
        o_int = lax.dot_general(q_in[rows], st.astype(BF16), (((1,), (1,)), ((), ())),
                                preferred_element_type=F32)
        pieces.append(o[rows] + o_int)
        upd = lax.dot_general(v_bf[rows], k_out[c], (((0,), (0,)), ((), ())),
                              preferred_element_type=F32)
        st = st * dec[c] + upd
    return jnp.concatenate(pieces, axis=0), st


def _hgrn_body(lbt_ref, ng_ref, q_ref, f_ref, i_ref, g_ref, o_ref, st_ref, *, layer, tt, sub):
    t = pl.program_id(2)

    @pl.when(t == 0)
    def _():
        st_ref[...] = jnp.zeros_like(st_ref)

    lbt = lbt_ref[...]
    e = jnp.exp(lbt - jnp.max(lbt, axis=0, keepdims=True))
    sm = e / jnp.sum(e, axis=0, keepdims=True)
    lb = jnp.sum(sm[1:layer + 2], axis=0, keepdims=True)

    st = st_ref[...]
    for c in range(tt // sub):
        rows = slice(c * sub, (c + 1) * sub)
        f = lb + (1.0 - lb) * jax.nn.sigmoid(f_ref[0, rows, :])
        q = q_ref[0, rows, :]
        qf = q * jax.nn.sigmoid(q) * (HG_HEAD ** -0.5)
        o, st = _hgrn_subtile(qf, 1.0 - f, i_ref[0, rows, :], jnp.log(f), st)
        ms = jnp.mean(o * o, axis=-1, keepdims=True)
        o = o * lax.rsqrt(ms + EPS) * ng_ref[...]
        g = g_ref[0, rows, :]
        o_ref[0, rows, :] = (o * (g * jax.nn.sigmoid(g))).astype(o_ref.dtype)
    st_ref[...] = st


def _hgrn_mixer(z, lb_table, norm_g, layer, mix_a, mix_b, tt=512):
    B, T, _ = z.shape
    tt = min(tt, T)
    sub = min(HG_SUB, tt)
    heads = mix_b // HG_HEAD
    c0 = mix_a // HG_HEAD
    est = 2 * 5 * _nbytes((tt, HG_HEAD), F32) + 32 * _nbytes((sub, HG_HEAD), F32) + 6 * _nbytes((sub, sub), F32)

    def col(off):
        return pl.BlockSpec((1, tt, HG_HEAD), lambda b, h, t, off=off: (b, t, c0 + off * heads + h))

    return pl.pallas_call(
        functools.partial(_hgrn_body, layer=layer, tt=tt, sub=sub),
        grid=(B, heads, T // tt),
        in_specs=[pl.BlockSpec((lb_table.shape[0], HG_HEAD), lambda b, h, t: (0, h)),
                  pl.BlockSpec((1, HG_HEAD), lambda b, h, t: (0, 0)),
                  col(0), col(1), col(2), col(3)],
        out_specs=pl.BlockSpec((1, tt, HG_HEAD), lambda b, h, t: (b, t, h)),
        out_shape=jax.ShapeDtypeStruct((B, T, mix_b), BF16),
        scratch_shapes=[pltpu.VMEM((HG_HEAD, HG_HEAD), F32)],
        compiler_params=_params(("parallel", "parallel", "arbitrary"), est),
        name="hgrn2_mixer",
    )(lb_table.astype(F32), norm_g.reshape(1, HG_HEAD).astype(F32), z, z, z, z)


def _fgate_body(h_ref, w_ref, b_ref, o_ref, carry_ref, *, tt):
    t = pl.program_id(1)

    @pl.when(t == 0)
    def _():
        carry_ref[...] = jnp.zeros_like(carry_ref)

    fl = jnp.dot(h_ref[0], w_ref[...], preferred_element_type=F32) + b_ref[...]
    c = (jnp.minimum(fl, 0.0) - jnp.log1p(jnp.exp(-jnp.abs(fl)))) * LOG2E
    row = lax.broadcasted_iota(jnp.int32, c.shape, 0)
    sh = 1
    while sh < tt:
        c = c + jnp.where(row >= sh, pltpu.roll(c, sh, axis=0), 0.0)
        sh *= 2
    c = c + carry_ref[...]
    o_ref[0] = c
    carry_ref[...] = c[tt - 1:tt, :]


def _fox_gates(h, wf, bf, tt=512):
    B, T, D = h.shape
    tt = min(tt, T)
    est = 2 * (_nbytes((tt, D), BF16) + _nbytes((D, LANES), BF16) + _nbytes((tt, LANES), F32)) \
        + 8 * _nbytes((tt, LANES), F32)
    return pl.pallas_call(
        functools.partial(_fgate_body, tt=tt),
        grid=(B, T // tt),
        in_specs=[pl.BlockSpec((1, tt, D), lambda b, t: (b, t, 0)),
                  pl.BlockSpec((D, LANES), lambda b, t: (0, 0)),
                  pl.BlockSpec((1, LANES), lambda b, t: (0, 0))],
        out_specs=pl.BlockSpec((1, tt, LANES), lambda b, t: (b, t, 0)),
        out_shape=jax.ShapeDtypeStruct((B, T, LANES), F32),
        scratch_shapes=[pltpu.VMEM((1, LANES), F32)],
        compiler_params=_params(("parallel", "arbitrary"), est),
        name="fox_gates",
    )(h, wf, bf)


def _fox_body(q_ref, k_ref, v_ref, f_ref, fr_ref, o_ref, vt_ref, fk_ref, p_ref, acc_ref, *, tq, seq, group):
    hg = pl.program_id(1)
    i = pl.program_id(2)
    reps = tq // LANES
    heads = range(group)

    def head_cols(g):
        return slice(g * FOX_HEAD, (g + 1) * FOX_HEAD)

    @pl.when(i == 0)
    def _():
        for c in range(seq // tq):
            rows = slice(c * tq, (c + 1) * tq)
            fblk = f_ref[0, rows, :]
            lane = lax.broadcasted_iota(jnp.int32, fblk.shape, 1)
            for g in heads:
                vt_ref[g, :, rows] = v_ref[0, rows, head_cols(g)].astype(F32).T.astype(BF16)
                col = jnp.sum(jnp.where(lane == hg * group + g, fblk, 0.0), axis=-1, keepdims=True)
                fk_ref[g, rows, :] = jnp.broadcast_to(col, fblk.shape)

    qt = [q_ref[0, :, head_cols(g)].astype(F32).T.astype(BF16) for g in heads]
    fq = [fr_ref[0, g, i] for g in heads]

    def scores(g, j):
        start = pl.multiple_of(j * tq, tq)
        return jnp.dot(k_ref[0, pl.ds(start, tq), head_cols(g)], qt[g], preferred_element_type=F32)

    def softmax_block(g, j, st, m_prev, l_prev, diagonal):
        start = pl.multiple_of(j * tq, tq)
        t = st - jnp.concatenate([fk_ref[g, pl.ds(start, tq), :]] * reps, axis=1)
        if diagonal:
            key = lax.broadcasted_iota(jnp.int32, (tq, tq), 0)
            qry = lax.broadcasted_iota(jnp.int32, (tq, tq), 1)
            t = jnp.where(key <= qry, t, -jnp.inf)
        m_new = jnp.maximum(m_prev, jnp.max(t, axis=0, keepdims=True) + fq[g])
        alpha = jnp.exp2(m_prev - m_new)
        p = jnp.exp2(t - (m_new - fq[g]))
        l_new = alpha * l_prev + jnp.sum(p, axis=0, keepdims=True)
        return p.astype(BF16), m_new, l_new, alpha

    def weighted_values(g, j, p):
        start = pl.multiple_of(j * tq, tq)
        return jnp.dot(vt_ref[g, :, pl.ds(start, tq)], p, preferred_element_type=F32)

    def block(j, stats, diagonal):
        out = []
        for g in heads:
            m, l = stats[g]
            pv_prev = weighted_values(g, jnp.maximum(j - 1, 0), p_ref[g])
            p, m, l, alpha = softmax_block(g, j, scores(g, j), m, l, diagonal)
            p_ref[g] = p
            acc_ref[g] = (acc_ref[g] + pv_prev) * alpha
            out.append((m, l))
        return tuple(out)

    p_ref[...] = jnp.zeros_like(p_ref)
    acc_ref[...] = jnp.zeros_like(acc_ref)
    stats = tuple((jnp.full((1, tq), -jnp.inf, F32), jnp.zeros((1, tq), F32)) for _ in heads)
    stats = lax.fori_loop(0, i, lambda j, c: block(j, c, False), stats)
    stats = block(i, stats, True)
    for g in heads:
        acc = acc_ref[g] + weighted_values(g, i, p_ref[g])
        o_ref[0, :, head_cols(g)] = (acc / stats[g][1]).T.astype(o_ref.dtype)


def _fox_attention(qkv, f, frow, heads, tq, group=FOX_GROUP):
    B, T, _ = qkv.shape
    nq = T // tq
    gw = group * FOX_HEAD
    ngrp = heads // group
    est = 2 * (2 * _nbytes((T, gw), BF16) + 2 * _nbytes((tq, gw), BF16)
               + _nbytes((T, LANES), F32) + group * _nbytes((nq, 8, tq), F32)) \
        + group * (_nbytes((T, FOX_HEAD), BF16) + _nbytes((T, LANES), F32) + 5 * _nbytes((tq, tq), F32))
    return pl.pallas_call(
        functools.partial(_fox_body, tq=tq, seq=T, group=group),
        grid=(B, ngrp, nq),
        in_specs=[pl.BlockSpec((1, tq, gw), lambda b, h, i: (b, i, h)),
                  pl.BlockSpec((1, T, gw), lambda b, h, i: (b, 0, ngrp + h)),
                  pl.BlockSpec((1, T, gw), lambda b, h, i: (b, 0, 2 * ngrp + h)),
                  pl.BlockSpec((1, T, LANES), lambda b, h, i: (b, 0, 0)),
                  pl.BlockSpec((1, group, nq, 1, tq), lambda b, h, i: (b, h, 0, 0, 0))],
        out_specs=pl.BlockSpec((1, tq, gw), lambda b, h, i: (b, i, h)),
        out_shape=jax.ShapeDtypeStruct((B, T, heads * FOX_HEAD), BF16),
        scratch_shapes=[pltpu.VMEM((group, FOX_HEAD, T), BF16),
                        pltpu.VMEM((group, T, LANES), F32),
                        pltpu.VMEM((group, tq, tq), BF16),
                        pltpu.VMEM((group, FOX_HEAD, tq), F32)],
        compiler_params=_params(("parallel", "parallel", "arbitrary"), est),
        name="fox_attention",
    )(qkv, qkv, qkv, f, frow)


def _xattn_body(q_ref, kv_ref, o_ref, *, d_model):
    hd = d_model // XA_HEADS
    for h in range(XA_HEADS):
        q = q_ref[0, :, h * hd:(h + 1) * hd]
        k = kv_ref[0, :, h * hd:(h + 1) * hd]
        v = kv_ref[0, :, d_model + h * hd:d_model + (h + 1) * hd]
        s = lax.dot_general(q, k, (((1,), (1,)), ((), ())), preferred_element_type=F32)
        p = jnp.exp(s - jnp.max(s, axis=-1, keepdims=True))
        p = p / jnp.sum(p, axis=-1, keepdims=True)
        o = jnp.dot(p.astype(BF16), v, preferred_element_type=F32)
        o_ref[0, :, h * hd:(h + 1) * hd] = o.astype(o_ref.dtype)


def _xattn_core(q, kv, tq=512):
    B, T, D = q.shape
    M = kv.shape[1]
    tq = min(tq, T)
    est = 2 * (2 * _nbytes((tq, D), BF16) + _nbytes((M, 2 * D), BF16)) + 6 * _nbytes((tq, M), F32) \
        + 2 * _nbytes((tq, D // XA_HEADS), F32)
    return pl.pallas_call(
        functools.partial(_xattn_body, d_model=D),
        grid=(B, T // tq),
        in_specs=[pl.BlockSpec((1, tq, D), lambda b, t: (b, t, 0)),
                  pl.BlockSpec((1, M, 2 * D), lambda b, t: (b, 0, 0))],
        out_specs=pl.BlockSpec((1, tq, D), lambda b, t: (b, t, 0)),
        out_shape=jax.ShapeDtypeStruct((B, T, D), BF16),
        compiler_params=_params(("parallel", "parallel"), est),
        name="xattn_core",
    )(q, kv)


def kernel(x, mem, lb_table, ev_norm, ev_w_in, ev_w_pool, ev_pool_scale, ev_hg_norm, ev_w_out,
           od_norm, od_w_in, od_b_f, od_w_out, xa_norm, xa_mem_norm, xa_wq, xa_wkv, xa_wo,
           ffn_norm, ffn_w_gate, ffn_w_up, ffn_w_down, final_norm):
    B, T, D = x.shape
    M = B * T
    depth = xa_norm.shape[0]
    mix_a = ev_pool_scale.shape[1]
    mix_b = lb_table.shape[1]
    fox_heads = od_b_f.shape[1]
    n_mem = mem.shape[1]
    xa_scale = (D // XA_HEADS) ** -0.5
    fox_tq = _pick_tile(T, 512)

    xs = x.reshape(M, D)
    mem2 = mem.reshape(B * n_mem, D)
    w_down = ffn_w_down.astype(BF16)
    for l in range(depth):
        if l % 2 == 0:
            e = l // 2
            h = _rmsnorm(xs, ev_norm[e], BF16)
            z = _matmul(h, ev_w_in, e, ev_w_in.shape[2], F32).reshape(B, T, -1)
            ya = _pool_mixer(z, ev_w_pool[e].astype(BF16), ev_pool_scale[e], mix_a)
            yb = _hgrn_mixer(z, lb_table, ev_hg_norm[e], l, mix_a, mix_b)
            y = jnp.concatenate([ya, yb], axis=-1).reshape(M, D)
            xs = _matmul_residual(y, ev_w_out, e, xs)
        else:
            o = l // 2
            h = _rmsnorm(xs, od_norm[o], BF16)
            qscale = jnp.concatenate([jnp.full((D,), LOG2E * FOX_HEAD ** -0.5, F32), jnp.ones((2 * D,), F32)])
            qkv = _matmul(h, od_w_in, o, 3 * D, BF16, colscale=qscale).reshape(B, T, 3 * D)
            wf = jnp.zeros((D, LANES), BF16).at[:, :fox_heads].set(od_w_in[o, :, 3 * D:].astype(BF16))
            bf = jnp.zeros((1, LANES), F32).at[0, :fox_heads].set(od_b_f[o].astype(F32))
            f = _fox_gates(h.reshape(B, T, D), wf, bf)
            frow = f[..., :fox_heads].transpose(0, 2, 1).reshape(B, fox_heads, T // fox_tq, 1, fox_tq)
            y = _fox_attention(qkv, f, frow, fox_heads, fox_tq).reshape(M, D)
            xs = _matmul_residual(y, od_w_out, o, xs)

        h = _rmsnorm(xs, xa_norm[l], BF16)
        mn = _rmsnorm(mem2, xa_mem_norm[l], BF16)
        q = _matmul(h, xa_wq, l, D, BF16, colscale=jnp.full((D,), xa_scale, F32))
        kv = _matmul(mn, xa_wkv, l, 2 * D, BF16)
        a = _xattn_core(q.reshape(B, T, D), kv.reshape(B, n_mem, 2 * D)).reshape(M, D)
        xs = _matmul_residual(a, xa_wo, l, xs)

        h = _rmsnorm(xs, ffn_norm[l], BF16)
        act = _matmul_swiglu(h, ffn_w_gate, ffn_w_up, l)
        xs = _matmul_residual(act, w_down, l, xs)
    return _rmsnorm(xs, final_norm, x.dtype).reshape(B, T, D)
```

```python
import functools

import jax
import jax.numpy as jnp
from jax import lax
from jax.experimental import pallas as pl
from jax.experimental.pallas import tpu as pltpu

F32 = jnp.float32
BF16 = jnp.bfloat16

EPS = 1e-6
N_MEM = 256
POOL_WINDOWS = (2, 4, 8, 16)
POOL_HALO = 16
HG_HEAD = 128
HG_CHUNK = 64
HG_DIAG = 8
HG_SUB = 256
FOX_HEAD = 128
FOX_GROUP = 4
XA_HEADS = 4

LOG2E = 1.4426950408889634
LANES = 128
VMEM_CAP = 56 * 1024 * 1024
VMEM_FLOOR = 32 * 1024 * 1024


def _params(semantics, vmem_estimate):
    limit = int(min(max(vmem_estimate * 5 // 4, VMEM_FLOOR), VMEM_CAP))
    return pltpu.CompilerParams(dimension_semantics=semantics, vmem_limit_bytes=limit)


def _nbytes(shape, dtype):
    n = jnp.dtype(dtype).itemsize
    for s in shape:
        n *= s
    return n


def _rmsnorm_body(x_ref, g_ref, o_ref):
    x = x_ref[...]
    ms = jnp.mean(x * x, axis=-1, keepdims=True)
    o_ref[...] = (x * lax.rsqrt(ms + EPS) * g_ref[...]).astype(o_ref.dtype)


def _rmsnorm(x, g, out_dtype, tm=512):
    M, D = x.shape
    tm = min(tm, M)
    est = 2 * (_nbytes((tm, D), x.dtype) + _nbytes((tm, D), out_dtype)) + 3 * _nbytes((tm, D), F32)
    return pl.pallas_call(
        _rmsnorm_body,
        grid=(M // tm,),
        in_specs=[pl.BlockSpec((tm, D), lambda i: (i, 0)),
                  pl.BlockSpec((1, D), lambda i: (0, 0))],
        out_specs=pl.BlockSpec((tm, D), lambda i: (i, 0)),
        out_shape=jax.ShapeDtypeStruct((M, D), out_dtype),
        compiler_params=_params(("parallel",), est),
        name="rmsnorm",
    )(x, g.reshape(1, D).astype(F32))


def _with_bf16_weights(w_refs, wb_refs, compute):
    if not wb_refs:
        compute(*(w[...] for w in w_refs))
        return
    i = pl.program_id(1)

    @pl.when(i == 0)
    def _():
        for w, wb in zip(w_refs, wb_refs):
            wb[...] = w[...].astype(BF16)

    @pl.when(i > 0)
    def _():
        compute(*(wb[...] for wb in wb_refs))


def _mm_body(x_ref, w_ref, cs_ref, o_ref, *wb):
    def compute(w):
        acc = jnp.dot(x_ref[...], w, preferred_element_type=F32)
        o_ref[...] = (acc * cs_ref[...]).astype(o_ref.dtype)

    _with_bf16_weights((w_ref,), wb, compute)


def _mm_res_body(x_ref, w_ref, r_ref, o_ref, *wb):
    def compute(w):
        acc = jnp.dot(x_ref[...], w, preferred_element_type=F32)
        o_ref[...] = (r_ref[...] + acc).astype(o_ref.dtype)

    _with_bf16_weights((w_ref,), wb, compute)


def _mm_swiglu_body(x_ref, wg_ref, wu_ref, o_ref, *wb):
    def compute(wg, wu):
        x = x_ref[...]
        g = jnp.dot(x, wg, preferred_element_type=F32)
        u = jnp.dot(x, wu, preferred_element_type=F32)
        o_ref[...] = (g * jax.nn.sigmoid(g) * u).astype(o_ref.dtype)

    _with_bf16_weights((wg_ref, wu_ref), wb, compute)


def _pick_tile(n, pref):
    t = min(pref, n)
    while n % t:
        t //= 2
    return t


def _weight_spec(layer, K, tn):
    return pl.BlockSpec((None, K, tn), lambda j, i: (layer, 0, j))


def _weight_scratch(w, K, tn):
    return [] if w.dtype == BF16 else [pltpu.VMEM((K, tn), BF16)]


def _weight_bytes(w, K, tn):
    return 2 * _nbytes((K, tn), w.dtype) + (0 if w.dtype == BF16 else _nbytes((K, tn), BF16))


def _row_steps(w, M, tm):
    extra = 0 if w.dtype == BF16 else 1
    return M // tm + extra, lambda i: jnp.maximum(i - extra, 0)


def _matmul(x, w, layer, n_cols, out_dtype, colscale=None, tm=1024, tn=1024):
    M, K = x.shape
    tm, tn = _pick_tile(M, tm), _pick_tile(n_cols, tn)
    if colscale is None:
        colscale = jnp.ones((n_cols,), F32)
    steps, row = _row_steps(w, M, tm)
    est = 2 * (_nbytes((tm, K), x.dtype) + _nbytes((tm, tn), out_dtype)) + _weight_bytes(w, K, tn) \
        + 2 * _nbytes((tm, tn), F32)
    return pl.pallas_call(
        _mm_body,
        grid=(n_cols // tn, steps),
        in_specs=[pl.BlockSpec((tm, K), lambda j, i: (row(i), 0)),
                  _weight_spec(layer, K, tn),
                  pl.BlockSpec((1, tn), lambda j, i: (0, j))],
        out_specs=pl.BlockSpec((tm, tn), lambda j, i: (row(i), j)),
        out_shape=jax.ShapeDtypeStruct((M, n_cols), out_dtype),
        scratch_shapes=_weight_scratch(w, K, tn),
        compiler_params=_params(("parallel", "arbitrary"), est),
        name="matmul",
    )(x, w, colscale.reshape(1, n_cols))


def _matmul_residual(x, w, layer, r, tn=1024):
    M, K = x.shape
    N = w.shape[2]
    tm = 1024 if K <= 2048 else 512
    tm, tn = _pick_tile(M, tm), _pick_tile(N, tn)
    steps, row = _row_steps(w, M, tm)
    est = 2 * (_nbytes((tm, K), x.dtype) + 2 * _nbytes((tm, tn), F32)) + _weight_bytes(w, K, tn) \
        + 2 * _nbytes((tm, tn), F32)
    return pl.pallas_call(
        _mm_res_body,
        grid=(N // tn, steps),
        in_specs=[pl.BlockSpec((tm, K), lambda j, i: (row(i), 0)),
                  _weight_spec(layer, K, tn),
                  pl.BlockSpec((tm, tn), lambda j, i: (row(i), j))],
        out_specs=pl.BlockSpec((tm, tn), lambda j, i: (row(i), j)),
        out_shape=jax.ShapeDtypeStruct((M, N), F32),
        scratch_shapes=_weight_scratch(w, K, tn),
        compiler_params=_params(("parallel", "arbitrary"), est),
        name="matmul_residual",
    )(x, w, r)


def _matmul_swiglu(x, wg, wu, layer, tm=1024, tn=512):
    M, K = x.shape
    N = wg.shape[2]
    tm, tn = _pick_tile(M, tm), _pick_tile(N, tn)
    steps, row = _row_steps(wg, M, tm)
    est = 2 * (_nbytes((tm, K), x.dtype) + _nbytes((tm, tn), BF16)) + 2 * _weight_bytes(wg, K, tn) \
        + 4 * _nbytes((tm, tn), F32)
    return pl.pallas_call(
        _mm_swiglu_body,
        grid=(N // tn, steps),
        in_specs=[pl.BlockSpec((tm, K), lambda j, i: (row(i), 0)),
                  _weight_spec(layer, K, tn),
                  _weight_spec(layer, K, tn)],
        out_specs=pl.BlockSpec((tm, tn), lambda j, i: (row(i), j)),
        out_shape=jax.ShapeDtypeStruct((M, N), BF16),
        scratch_shapes=_weight_scratch(wg, K, tn) + _weight_scratch(wu, K, tn),
        compiler_params=_params(("parallel", "arbitrary"), est),
        name="matmul_swiglu",
    )(x, wg, wu)


def _mm_res_norm_body(*refs, n_parts):
    x_refs = refs[:n_parts]
    w_ref, r_ref, g_ref, o_ref, h_ref = refs[n_parts:]
    acc = r_ref[...]
    k0 = 0
    for x_ref in x_refs:
        kw = x_ref.shape[1]
        acc = acc + jnp.dot(x_ref[...], w_ref[k0:k0 + kw, :], preferred_element_type=F32)
        k0 += kw
    o_ref[...] = acc
    ms = jnp.mean(acc * acc, axis=-1, keepdims=True)
    h_ref[...] = (acc * lax.rsqrt(ms + EPS) * g_ref[...]).astype(h_ref.dtype)


def _matmul_residual_norm(x_parts, w, layer, r, gain, tm=512):
    M, N = r.shape
    K = w.shape[1]
    tm = _pick_tile(M, tm)
    est = 2 * (_nbytes((tm, K), BF16) + _nbytes((K, N), w.dtype) + 2 * _nbytes((tm, N), F32) + _nbytes((tm, N), BF16)) \
        + 3 * _nbytes((tm, N), F32)
    return pl.pallas_call(
        functools.partial(_mm_res_norm_body, n_parts=len(x_parts)),
        grid=(M // tm,),
        in_specs=[pl.BlockSpec((tm, xp.shape[1]), lambda i: (i, 0)) for xp in x_parts]
        + [pl.BlockSpec((None, K, N), lambda i: (layer, 0, 0)),
           pl.BlockSpec((tm, N), lambda i: (i, 0)),
           pl.BlockSpec((1, N), lambda i: (0, 0))],
        out_specs=[pl.BlockSpec((tm, N), lambda i: (i, 0)),
                   pl.BlockSpec((tm, N), lambda i: (i, 0))],
        out_shape=[jax.ShapeDtypeStruct((M, N), F32),
                   jax.ShapeDtypeStruct((M, N), BF16)],
        compiler_params=_params(("parallel",), est),
        name="matmul_residual_norm",
    )(*x_parts, w, r, gain.reshape(1, N).astype(F32))


def _pool_body(u_ref, halo_ref, w_ref, sc_ref, o_ref, ext_ref, *, tt, group):
    t = pl.program_id(1)
    u = u_ref[0]
    ext_ref[0:POOL_HALO, :] = jnp.where(t > 0, halo_ref[0], 0.0)
    ext_ref[POOL_HALO:POOL_HALO + tt, :] = u
    pos = t * tt + lax.broadcasted_iota(jnp.int32, (tt, 1), 0)
    for gi, win in enumerate(POOL_WINDOWS):
        c0, c1 = gi * group, (gi + 1) * group
        tok = u[:, c0:c1]
        acc = tok
        for d in range(1, win):
            acc = acc + ext_ref[POOL_HALO - d:POOL_HALO - d + tt, c0:c1]
        cnt = jnp.minimum(pos + 1, win).astype(F32)
        p = acc / cnt - tok
        y = jnp.dot(p.astype(BF16), w_ref[gi], preferred_element_type=F32)
        o_ref[0, :, c0:c1] = (y * sc_ref[:, c0:c1]).astype(o_ref.dtype)


def _pool_mixer(z, w_pool, scale, mix_a, tt=512):
    B, T, _ = z.shape
    tt = min(tt, T)
    group = mix_a // len(POOL_WINDOWS)
    halo_blocks = tt // POOL_HALO
    est = 2 * (_nbytes((tt, mix_a), F32) + _nbytes((tt, mix_a), BF16)) + 4 * _nbytes((tt, mix_a), F32)
    return pl.pallas_call(
        functools.partial(_pool_body, tt=tt, group=group),
        grid=(B, T // tt),
        in_specs=[pl.BlockSpec((1, tt, mix_a), lambda b, t: (b, t, 0)),
                  pl.BlockSpec((1, POOL_HALO, mix_a),
                               lambda b, t: (b, jnp.maximum(t * halo_blocks - 1, 0), 0)),
                  pl.BlockSpec((len(POOL_WINDOWS), group, group), lambda b, t: (0, 0, 0)),
                  pl.BlockSpec((1, mix_a), lambda b, t: (0, 0))],
        out_specs=pl.BlockSpec((1, tt, mix_a), lambda b, t: (b, t, 0)),
        out_shape=jax.ShapeDtypeStruct((B, T, mix_a), BF16),
        scratch_shapes=[pltpu.VMEM((tt + POOL_HALO, mix_a), F32)],
        compiler_params=_params(("parallel", "parallel"), est),
        name="pool_mixer",
    )(z, z, w_pool, scale.reshape(1, mix_a).astype(F32))


def _hgrn_subtile(qf, kk, v, logf, st):
    n = qf.shape[0]
    C, R = HG_CHUNK, HG_DIAG
    row = lax.broadcasted_iota(jnp.int32, (n, HG_HEAD), 0)

    pos = row & (C - 1)
    b = logf
    sh = 1
    while sh < C:
        b = b + jnp.where(pos >= sh, pltpu.roll(b, sh, axis=0), 0.0)
        sh *= 2

    nb = n // R
    b3, q3, k3, v3 = (a.reshape(nb, R, HG_HEAD) for a in (b, qf, kk, v))
    tpos = lax.broadcasted_iota(jnp.int32, (1, R, 1), 1)
    od = jnp.zeros((nb, R, HG_HEAD), F32)
    for s in range(R):
        w = q3 * jnp.exp(b3 - b3[:, s:s + 1, :]) * k3[:, s:s + 1, :]
        a = jnp.sum(w, axis=-1, keepdims=True)
        a = jnp.where(tpos >= s, a, 0.0)
        od = od + a * v3[:, s:s + 1, :]
    o = od.reshape(n, HG_HEAD)

    same = lax.broadcasted_iota(jnp.int32, (n, n), 0) ^ lax.broadcasted_iota(jnp.int32, (n, n), 1)
    a_off = jnp.zeros((n, n), F32)
    h = R
    while h < C:
        g = 2 * h
        ref = b.reshape(n // g, g, HG_HEAD)[:, h - 1:h, :]
        ref = jnp.broadcast_to(ref, (n // g, g, HG_HEAD)).reshape(n, HG_HEAD)
        e = jnp.exp(-jnp.abs(b - ref))
        right = (row & (g - 1)) >= h
        qt = jnp.where(right, qf * e, 0.0).astype(BF16)
        kt = jnp.where(right, 0.0, kk * e).astype(BF16)
        a = lax.dot_general(qt, kt, (((1,), (1,)), ((), ())), preferred_element_type=F32)
        a_off = a_off + jnp.where(same < g, a, 0.0)
        h = g
    v_bf = v.astype(BF16)
    o = o + jnp.dot(a_off.astype(BF16), v_bf, preferred_element_type=F32)

    nc = n // C
    bc = b.reshape(nc, C, HG_HEAD)
    b_last = bc[:, C - 1:C, :]
    q_in = (qf * jnp.exp(b)).astype(BF16)
    k_out = (kk.reshape(nc, C, HG_HEAD) * jnp.exp(b_last - bc)).astype(BF16)
    dec = jnp.exp(b_last)
    pieces = []
    for c in range(nc):
        rows = slice(c * C, (c + 1) * C)
        o_int = lax.dot_general(q_in[rows], st.astype(BF16), (((1,), (1,)), ((), ())),
                                preferred_element_type=F32)
        pieces.append(o[rows] + o_int)
        upd = lax.dot_general(v_bf[rows], k_out[c], (((0,), (0,)), ((), ())),
                              preferred_element_type=F32)
        st = st * dec[c] + upd
    return jnp.concatenate(pieces, axis=0), st


def _hgrn_body(lbt_ref, ng_ref, q_ref, f_ref, i_ref, g_ref, o_ref, st_ref, *, layer, tt, sub):
    t = pl.program_id(2)

    @pl.when(t == 0)
    def _():
        st_ref[...] = jnp.zeros_like(st_ref)

    lbt = lbt_ref[...]
    e = jnp.exp(lbt - jnp.max(lbt, axis=0, keepdims=True))
    sm = e / jnp.sum(e, axis=0, keepdims=True)
    lb = jnp.sum(sm[1:layer + 2], axis=0, keepdims=True)

    st = st_ref[...]
    for c in range(tt // sub):
        rows = slice(c * sub, (c + 1) * sub)
        f = lb + (1.0 - lb) * jax.nn.sigmoid(f_ref[0, rows, :])
        q = q_ref[0, rows, :]
        qf = q * jax.nn.sigmoid(q) * (HG_HEAD ** -0.5)
        o, st = _hgrn_subtile(qf, 1.0 - f, i_ref[0, rows, :], jnp.log(f), st)
        ms = jnp.mean(o * o, axis=-1, keepdims=True)
        o = o * lax.rsqrt(ms + EPS) * ng_ref[...]
        g = g_ref[0, rows, :]
        o_ref[0, rows, :] = (o * (g * jax.nn.sigmoid(g))).astype(o_ref.dtype)
    st_ref[...] = st


def _hgrn_mixer(z, lb_table, norm_g, layer, mix_a, mix_b, tt=512):
    B, T, _ = z.shape
    tt = min(tt, T)
    sub = min(HG_SUB, tt)
    heads = mix_b // HG_HEAD
    c0 = mix_a // HG_HEAD
    est = 2 * 5 * _nbytes((tt, HG_HEAD), F32) + 32 * _nbytes((sub, HG_HEAD), F32) + 6 * _nbytes((sub, sub), F32)

    def col(off):
        return pl.BlockSpec((1, tt, HG_HEAD), lambda b, h, t, off=off: (b, t, c0 + off * heads + h))

    return pl.pallas_call(
        functools.partial(_hgrn_body, layer=layer, tt=tt, sub=sub),
        grid=(B, heads, T // tt),
        in_specs=[pl.BlockSpec((lb_table.shape[0], HG_HEAD), lambda b, h, t: (0, h)),
                  pl.BlockSpec((1, HG_HEAD), lambda b, h, t: (0, 0)),
                  col(0), col(1), col(2), col(3)],
        out_specs=pl.BlockSpec((1, tt, HG_HEAD), lambda b, h, t: (b, t, h)),
        out_shape=jax.ShapeDtypeStruct((B, T, mix_b), BF16),
        scratch_shapes=[pltpu.VMEM((HG_HEAD, HG_HEAD), F32)],
        compiler_params=_params(("parallel", "parallel", "arbitrary"), est),
        name="hgrn2_mixer",
    )(lb_table.astype(F32), norm_g.reshape(1, HG_HEAD).astype(F32), z, z, z, z)


def _fgate_body(h_ref, w_ref, b_ref, o_ref, carry_ref, *, tt):
    t = pl.program_id(1)

    @pl.when(t == 0)
    def _():
        carry_ref[...] = jnp.zeros_like(carry_ref)

    fl = jnp.dot(h_ref[0], w_ref[...], preferred_element_type=F32) + b_ref[...]
    c = (jnp.minimum(fl, 0.0) - jnp.log1p(jnp.exp(-jnp.abs(fl)))) * LOG2E
    row = lax.broadcasted_iota(jnp.int32, c.shape, 0)
    sh = 1
    while sh < tt:
        c = c + jnp.where(row >= sh, pltpu.roll(c, sh, axis=0), 0.0)
        sh *= 2
    c = c + carry_ref[...]
    o_ref[0] = c
    carry_ref[...] = c[tt - 1:tt, :]


def _fox_gates(h, wf, bf, tt=512):
    B, T, D = h.shape
    tt = min(tt, T)
    est = 2 * (_nbytes((tt, D), BF16) + _nbytes((D, LANES), BF16) + _nbytes((tt, LANES), F32)) \
        + 8 * _nbytes((tt, LANES), F32)
    return pl.pallas_call(
        functools.partial(_fgate_body, tt=tt),
        grid=(B, T // tt),
        in_specs=[pl.BlockSpec((1, tt, D), lambda b, t: (b, t, 0)),
                  pl.BlockSpec((D, LANES), lambda b, t: (0, 0)),
                  pl.BlockSpec((1, LANES), lambda b, t: (0, 0))],
        out_specs=pl.BlockSpec((1, tt, LANES), lambda b, t: (b, t, 0)),
        out_shape=jax.ShapeDtypeStruct((B, T, LANES), F32),
        scratch_shapes=[pltpu.VMEM((1, LANES), F32)],
        compiler_params=_params(("parallel", "arbitrary"), est),
        name="fox_gates",
    )(h, wf, bf)


def _fox_body(q_ref, k_ref, v_ref, f_ref, fr_ref, o_ref, vt_ref, fk_ref, p_ref, acc_ref, *, tq, seq, group):
    hg = pl.program_id(1)
    i = pl.program_id(2)
    reps = tq // LANES
    heads = range(group)

    def head_cols(g):
        return slice(g * FOX_HEAD, (g + 1) * FOX_HEAD)

    @pl.when(i == 0)
    def _():
        for c in range(seq // tq):
            rows = slice(c * tq, (c + 1) * tq)
            fblk = f_ref[0, rows, :]
            lane = lax.broadcasted_iota(jnp.int32, fblk.shape, 1)
            for g in heads:
                vt_ref[g, :, rows] = v_ref[0, rows, head_cols(g)].astype(F32).T.astype(BF16)
                col = jnp.sum(jnp.where(lane == hg * group + g, fblk, 0.0), axis=-1, keepdims=True)
                fk_ref[g, rows, :] = jnp.broadcast_to(col, fblk.shape)

    qt = [q_ref[0, :, head_cols(g)].astype(F32).T.astype(BF16) for g in heads]
    fq = [fr_ref[0, g, i] for g in heads]

    def scores(g, j):
        start = pl.multiple_of(j * tq, tq)
        return jnp.dot(k_ref[0, pl.ds(start, tq), head_cols(g)], qt[g], preferred_element_type=F32)

    def softmax_block(g, j, st, m_prev, l_prev, diagonal):
        start = pl.multiple_of(j * tq, tq)
        t = st - jnp.concatenate([fk_ref[g, pl.ds(start, tq), :]] * reps, axis=1)
        if diagonal:
            key = lax.broadcasted_iota(jnp.int32, (tq, tq), 0)
            qry = lax.broadcasted_iota(jnp.int32, (tq, tq), 1)
            t = jnp.where(key <= qry, t, -jnp.inf)
        m_new = jnp.maximum(m_prev, jnp.max(t, axis=0, keepdims=True) + fq[g])
        alpha = jnp.exp2(m_prev - m_new)
        p = jnp.exp2(t - (m_new - fq[g]))
        l_new = alpha * l_prev + jnp.sum(p, axis=0, keepdims=True)
        return p.astype(BF16), m_new, l_new, alpha

    def weighted_values(g, j, p):
        start = pl.multiple_of(j * tq, tq)
        return jnp.dot(vt_ref[g, :, pl.ds(start, tq)], p, preferred_element_type=F32)

    def block(j, stats, diagonal):
        out = []
        for g in heads:
            m, l = stats[g]
            pv_prev = weighted_values(g, jnp.maximum(j - 1, 0), p_ref[g])
            p, m, l, alpha = softmax_block(g, j, scores(g, j), m, l, diagonal)
            p_ref[g] = p
            acc_ref[g] = (acc_ref[g] + pv_prev) * alpha
            out.append((m, l))
        return tuple(out)

    p_ref[...] = jnp.zeros_like(p_ref)
    acc_ref[...] = jnp.zeros_like(acc_ref)
    stats = tuple((jnp.full((1, tq), -jnp.inf, F32), jnp.zeros((1, tq), F32)) for _ in heads)
    stats = lax.fori_loop(0, i, lambda j, c: block(j, c, False), stats)
    stats = block(i, stats, True)
    for g in heads:
        acc = acc_ref[g] + weighted_values(g, i, p_ref[g])
        o_ref[0, :, head_cols(g)] = (acc / stats[g][1]).T.astype(o_ref.dtype)


def _fox_attention(qkv, f, frow, heads, tq, group=FOX_GROUP):
    B, T, _ = qkv.shape
    nq = T // tq
    gw = group * FOX_HEAD
    ngrp = heads // group
    est = 2 * (2 * _nbytes((T, gw), BF16) + 2 * _nbytes((tq, gw), BF16)
               + _nbytes((T, LANES), F32) + group * _nbytes((nq, 8, tq), F32)) \
        + group * (_nbytes((T, FOX_HEAD), BF16) + _nbytes((T, LANES), F32) + 5 * _nbytes((tq, tq), F32))
    return pl.pallas_call(
        functools.partial(_fox_body, tq=tq, seq=T, group=group),
        grid=(B, ngrp, nq),
        in_specs=[pl.BlockSpec((1, tq, gw), lambda b, h, i: (b, i, h)),
                  pl.BlockSpec((1, T, gw), lambda b, h, i: (b, 0, ngrp + h)),
                  pl.BlockSpec((1, T, gw), lambda b, h, i: (b, 0, 2 * ngrp + h)),
                  pl.BlockSpec((1, T, LANES), lambda b, h, i: (b, 0, 0)),
                  pl.BlockSpec((1, group, nq, 1, tq), lambda b, h, i: (b, h, 0, 0, 0))],
        out_specs=pl.BlockSpec((1, tq, gw), lambda b, h, i: (b, i, h)),
        out_shape=jax.ShapeDtypeStruct((B, T, heads * FOX_HEAD), BF16),
        scratch_shapes=[pltpu.VMEM((group, FOX_HEAD, T), BF16),
                        pltpu.VMEM((group, T, LANES), F32),
                        pltpu.VMEM((group, tq, tq), BF16),
                        pltpu.VMEM((group, FOX_HEAD, tq), F32)],
        compiler_params=_params(("parallel", "parallel", "arbitrary"), est),
        name="fox_attention",
    )(qkv, qkv, qkv, f, frow)


def _xattn_body(q_ref, kv_ref, o_ref, *, d_model):
    hd = d_model // XA_HEADS
    for h in range(XA_HEADS):
        q = q_ref[0, :, h * hd:(h + 1) * hd]
        k = kv_ref[0, :, h * hd:(h + 1) * hd]
        v = kv_ref[0, :, d_model + h * hd:d_model + (h + 1) * hd]
        s = lax.dot_general(q, k, (((1,), (1,)), ((), ())), preferred_element_type=F32)
        p = jnp.exp(s - jnp.max(s, axis=-1, keepdims=True))
        p = p / jnp.sum(p, axis=-1, keepdims=True)
        o = jnp.dot(p.astype(BF16), v, preferred_element_type=F32)
        o_ref[0, :, h * hd:(h + 1) * hd] = o.astype(o_ref.dtype)


def _xattn_core(q, kv, tq=512):
    B, T, D = q.shape
    M = kv.shape[1]
    tq = min(tq, T)
    est = 2 * (2 * _nbytes((tq, D), BF16) + _nbytes((M, 2 * D), BF16)) + 6 * _nbytes((tq, M), F32) \
        + 2 * _nbytes((tq, D // XA_HEADS), F32)
    return pl.pallas_call(
        functools.partial(_xattn_body, d_model=D),
        grid=(B, T // tq),
        in_specs=[pl.BlockSpec((1, tq, D), lambda b, t: (b, t, 0)),
                  pl.BlockSpec((1, M, 2 * D), lambda b, t: (b, 0, 0))],
        out_specs=pl.BlockSpec((1, tq, D), lambda b, t: (b, t, 0)),
        out_shape=jax.ShapeDtypeStruct((B, T, D), BF16),
        compiler_params=_params(("parallel", "parallel"), est),
        name="xattn_core",
    )(q, kv)


def kernel(x, mem, lb_table, ev_norm, ev_w_in, ev_w_pool, ev_pool_scale, ev_hg_norm, ev_w_out,
           od_norm, od_w_in, od_b_f, od_w_out, xa_norm, xa_mem_norm, xa_wq, xa_wkv, xa_wo,
           ffn_norm, ffn_w_gate, ffn_w_up, ffn_w_down, final_norm):
    B, T, D = x.shape
    M = B * T
    depth = xa_norm.shape[0]
    mix_a = ev_pool_scale.shape[1]
    mix_b = lb_table.shape[1]
    fox_heads = od_b_f.shape[1]
    n_mem = mem.shape[1]
    xa_scale = (D // XA_HEADS) ** -0.5
    fox_tq = _pick_tile(T, 512)

    xs = x.reshape(M, D)
    mem2 = mem.reshape(B * n_mem, D)
    w_down = ffn_w_down.astype(BF16)
    ev_wo, od_wo, xa_wo_b = ev_w_out.astype(BF16), od_w_out.astype(BF16), xa_wo.astype(BF16)
    for l in range(depth):
        if l % 2 == 0:
            e = l // 2
            h = _rmsnorm(xs, ev_norm[e], BF16)
            z = _matmul(h, ev_w_in, e, ev_w_in.shape[2], F32).reshape(B, T, -1)
            ya = _pool_mixer(z, ev_w_pool[e].astype(BF16), ev_pool_scale[e], mix_a)
            yb = _hgrn_mixer(z, lb_table, ev_hg_norm[e], l, mix_a, mix_b)
            xs, h = _matmul_residual_norm([ya.reshape(M, mix_a), yb.reshape(M, mix_b)], ev_wo, e, xs, xa_norm[l])
        else:
            o = l // 2
            h = _rmsnorm(xs, od_norm[o], BF16)
            qscale = jnp.concatenate([jnp.full((D,), LOG2E * FOX_HEAD ** -0.5, F32), jnp.ones((2 * D,), F32)])
            qkv = _matmul(h, od_w_in, o, 3 * D, BF16, colscale=qscale).reshape(B, T, 3 * D)
            wf = jnp.zeros((D, LANES), BF16).at[:, :fox_heads].set(od_w_in[o, :, 3 * D:].astype(BF16))
            bf = jnp.zeros((1, LANES), F32).at[0, :fox_heads].set(od_b_f[o].astype(F32))
            f = _fox_gates(h.reshape(B, T, D), wf, bf)
            frow = f[..., :fox_heads].transpose(0, 2, 1).reshape(B, fox_heads, T // fox_tq, 1, fox_tq)
            y = _fox_attention(qkv, f, frow, fox_heads, fox_tq).reshape(M, D)
            xs, h = _matmul_residual_norm([y], od_wo, o, xs, xa_norm[l])

        mn = _rmsnorm(mem2, xa_mem_norm[l], BF16)
        q = _matmul(h, xa_wq, l, D, BF16, colscale=jnp.full((D,), xa_scale, F32))
        kv = _matmul(mn, xa_wkv, l, 2 * D, BF16)
        a = _xattn_core(q.reshape(B, T, D), kv.reshape(B, n_mem, 2 * D)).reshape(M, D)
        xs, h = _matmul_residual_norm([a], xa_wo_b, l, xs, ffn_norm[l])

        act = _matmul_swiglu(h, ffn_w_gate, ffn_w_up, l)
        xs = _matmul_residual(act, w_down, l, xs)
    return _rmsnorm(xs, final_norm, x.dtype).reshape(B, T, D)
```

```python
import functools

import jax
import jax.numpy as jnp
from jax import lax
from jax.experimental import pallas as pl
from jax.experimental.pallas import tpu as pltpu

F32 = jnp.float32
BF16 = jnp.bfloat16

EPS = 1e-6
N_MEM = 256
POOL_WINDOWS = (2, 4, 8, 16)
POOL_HALO = 16
HG_HEAD = 128
HG_CHUNK = 64
HG_DIAG = 8
HG_SUB = 256
FOX_HEAD = 128
FOX_GROUP = 4
XA_HEADS = 4

LOG2E = 1.4426950408889634
LANES = 128
VMEM_CAP = 56 * 1024 * 1024
VMEM_FLOOR = 32 * 1024 * 1024


def _params(semantics, vmem_estimate):
    limit = int(min(max(vmem_estimate * 5 // 4, VMEM_FLOOR), VMEM_CAP))
    return pltpu.CompilerParams(dimension_semantics=semantics, vmem_limit_bytes=limit)


def _nbytes(shape, dtype):
    n = jnp.dtype(dtype).itemsize
    for s in shape:
        n *= s
    return n


def _rmsnorm_body(x_ref, g_ref, o_ref):
    x = x_ref[...]
    ms = jnp.mean(x * x, axis=-1, keepdims=True)
    o_ref[...] = (x * lax.rsqrt(ms + EPS) * g_ref[...]).astype(o_ref.dtype)


def _rmsnorm(x, g, out_dtype, tm=512):
    M, D = x.shape
    tm = min(tm, M)
    est = 2 * (_nbytes((tm, D), x.dtype) + _nbytes((tm, D), out_dtype)) + 3 * _nbytes((tm, D), F32)
    return pl.pallas_call(
        _rmsnorm_body,
        grid=(M // tm,),
        in_specs=[pl.BlockSpec((tm, D), lambda i: (i, 0)),
                  pl.BlockSpec((1, D), lambda i: (0, 0))],
        out_specs=pl.BlockSpec((tm, D), lambda i: (i, 0)),
        out_shape=jax.ShapeDtypeStruct((M, D), out_dtype),
        compiler_params=_params(("parallel",), est),
        name="rmsnorm",
    )(x, g.reshape(1, D).astype(F32))


def _with_bf16_weights(w_refs, wb_refs, compute):
    if not wb_refs:
        compute(*(w[...] for w in w_refs))
        return
    i = pl.program_id(1)

    @pl.when(i == 0)
    def _():
        for w, wb in zip(w_refs, wb_refs):
            wb[...] = w[...].astype(BF16)

    @pl.when(i > 0)
    def _():
        compute(*(wb[...] for wb in wb_refs))


def _mm_body(x_ref, w_ref, cs_ref, o_ref, *wb):
    def compute(w):
        acc = jnp.dot(x_ref[...], w, preferred_element_type=F32)
        o_ref[...] = (acc * cs_ref[...]).astype(o_ref.dtype)

    _with_bf16_weights((w_ref,), wb, compute)


def _mm_res_body(x_ref, w_ref, r_ref, o_ref, *wb):
    def compute(w):
        acc = jnp.dot(x_ref[...], w, preferred_element_type=F32)
        o_ref[...] = (r_ref[...] + acc).astype(o_ref.dtype)

    _with_bf16_weights((w_ref,), wb, compute)


def _mm_swiglu_body(x_ref, wg_ref, wu_ref, o_ref, *wb):
    def compute(wg, wu):
        x = x_ref[...]
        g = jnp.dot(x, wg, preferred_element_type=F32)
        u = jnp.dot(x, wu, preferred_element_type=F32)
        o_ref[...] = (g * jax.nn.sigmoid(g) * u).astype(o_ref.dtype)

    _with_bf16_weights((wg_ref, wu_ref), wb, compute)


def _pick_tile(n, pref):
    t = min(pref, n)
    while n % t:
        t //= 2
    return t


def _weight_spec(layer, K, tn):
    return pl.BlockSpec((None, K, tn), lambda j, i: (layer, 0, j))


def _weight_scratch(w, K, tn):
    return [] if w.dtype == BF16 else [pltpu.VMEM((K, tn), BF16)]


def _weight_bytes(w, K, tn):
    return 2 * _nbytes((K, tn), w.dtype) + (0 if w.dtype == BF16 else _nbytes((K, tn), BF16))


def _row_steps(w, M, tm):
    extra = 0 if w.dtype == BF16 else 1
    return M // tm + extra, lambda i: jnp.maximum(i - extra, 0)


def _matmul(x, w, layer, n_cols, out_dtype, colscale=None, tm=1024, tn=1024):
    M, K = x.shape
    tm, tn = _pick_tile(M, tm), _pick_tile(n_cols, tn)
    if colscale is None:
        colscale = jnp.ones((n_cols,), F32)
    steps, row = _row_steps(w, M, tm)
    est = 2 * (_nbytes((tm, K), x.dtype) + _nbytes((tm, tn), out_dtype)) + _weight_bytes(w, K, tn) \
        + 2 * _nbytes((tm, tn), F32)
    return pl.pallas_call(
        _mm_body,
        grid=(n_cols // tn, steps),
        in_specs=[pl.BlockSpec((tm, K), lambda j, i: (row(i), 0)),
                  _weight_spec(layer, K, tn),
                  pl.BlockSpec((1, tn), lambda j, i: (0, j))],
        out_specs=pl.BlockSpec((tm, tn), lambda j, i: (row(i), j)),
        out_shape=jax.ShapeDtypeStruct((M, n_cols), out_dtype),
        scratch_shapes=_weight_scratch(w, K, tn),
        compiler_params=_params(("parallel", "arbitrary"), est),
        name="matmul",
    )(x, w, colscale.reshape(1, n_cols))


def _matmul_residual(x, w, layer, r, tn=1024):
    M, K = x.shape
    N = w.shape[2]
    tm = 1024 if K <= 2048 else 512
    tm, tn = _pick_tile(M, tm), _pick_tile(N, tn)
    steps, row = _row_steps(w, M, tm)
    est = 2 * (_nbytes((tm, K), x.dtype) + 2 * _nbytes((tm, tn), F32)) + _weight_bytes(w, K, tn) \
        + 2 * _nbytes((tm, tn), F32)
    return pl.pallas_call(
        _mm_res_body,
        grid=(N // tn, steps),
        in_specs=[pl.BlockSpec((tm, K), lambda j, i: (row(i), 0)),
                  _weight_spec(layer, K, tn),
                  pl.BlockSpec((tm, tn), lambda j, i: (row(i), j))],
        out_specs=pl.BlockSpec((tm, tn), lambda j, i: (row(i), j)),
        out_shape=jax.ShapeDtypeStruct((M, N), F32),
        scratch_shapes=_weight_scratch(w, K, tn),
        compiler_params=_params(("parallel", "arbitrary"), est),
        name="matmul_residual",
    )(x, w, r)


def _matmul_swiglu(x, wg, wu, layer, tm=1024, tn=512):
    M, K = x.shape
    N = wg.shape[2]
    tm, tn = _pick_tile(M, tm), _pick_tile(N, tn)
    steps, row = _row_steps(wg, M, tm)
    est = 2 * (_nbytes((tm, K), x.dtype) + _nbytes((tm, tn), BF16)) + 2 * _weight_bytes(wg, K, tn) \
        + 4 * _nbytes((tm, tn), F32)
    return pl.pallas_call(
        _mm_swiglu_body,
        grid=(N // tn, steps),
        in_specs=[pl.BlockSpec((tm, K), lambda j, i: (row(i), 0)),
                  _weight_spec(layer, K, tn),
                  _weight_spec(layer, K, tn)],
        out_specs=pl.BlockSpec((tm, tn), lambda j, i: (row(i), j)),
        out_shape=jax.ShapeDtypeStruct((M, N), BF16),
        scratch_shapes=_weight_scratch(wg, K, tn) + _weight_scratch(wu, K, tn),
        compiler_params=_params(("parallel", "arbitrary"), est),
        name="matmul_swiglu",
    )(x, wg, wu)


def _mm_res_norm_body(*refs, n_parts):
    x_refs = refs[:n_parts]
    w_ref, r_ref, g_ref, o_ref, h_ref = refs[n_parts:]
    acc = r_ref[...]
    k0 = 0
    for x_ref in x_refs:
        kw = x_ref.shape[1]
        acc = acc + jnp.dot(x_ref[...], w_ref[k0:k0 + kw, :], preferred_element_type=F32)
        k0 += kw
    o_ref[...] = acc
    ms = jnp.mean(acc * acc, axis=-1, keepdims=True)
    h_ref[...] = (acc * lax.rsqrt(ms + EPS) * g_ref[...]).astype(h_ref.dtype)


def _matmul_residual_norm(x_parts, w, layer, r, gain, tm=512):
    M, N = r.shape
    K = w.shape[1]
    tm = _pick_tile(M, tm)
    est = 2 * (_nbytes((tm, K), BF16) + _nbytes((K, N), w.dtype) + 2 * _nbytes((tm, N), F32) + _nbytes((tm, N), BF16)) \
        + 3 * _nbytes((tm, N), F32)
    return pl.pallas_call(
        functools.partial(_mm_res_norm_body, n_parts=len(x_parts)),
        grid=(M // tm,),
        in_specs=[pl.BlockSpec((tm, xp.shape[1]), lambda i: (i, 0)) for xp in x_parts]
        + [pl.BlockSpec((None, K, N), lambda i: (layer, 0, 0)),
           pl.BlockSpec((tm, N), lambda i: (i, 0)),
           pl.BlockSpec((1, N), lambda i: (0, 0))],
        out_specs=[pl.BlockSpec((tm, N), lambda i: (i, 0)),
                   pl.BlockSpec((tm, N), lambda i: (i, 0))],
        out_shape=[jax.ShapeDtypeStruct((M, N), F32),
                   jax.ShapeDtypeStruct((M, N), BF16)],
        compiler_params=_params(("parallel",), est),
        name="matmul_residual_norm",
    )(*x_parts, w, r, gain.reshape(1, N).astype(F32))


def _pool_body(u_ref, halo_ref, w_ref, sc_ref, o_ref, ext_ref, *, tt, group):
    t = pl.program_id(1)
    u = u_ref[0]
    ext_ref[0:POOL_HALO, :] = jnp.where(t > 0, halo_ref[0], 0.0)
    ext_ref[POOL_HALO:POOL_HALO + tt, :] = u
    pos = t * tt + lax.broadcasted_iota(jnp.int32, (tt, 1), 0)
    for gi, win in enumerate(POOL_WINDOWS):
        c0, c1 = gi * group, (gi + 1) * group
        tok = u[:, c0:c1]
        acc = tok
        for d in range(1, win):
            acc = acc + ext_ref[POOL_HALO - d:POOL_HALO - d + tt, c0:c1]
        cnt = jnp.minimum(pos + 1, win).astype(F32)
        p = acc / cnt - tok
        y = jnp.dot(p.astype(BF16), w_ref[gi], preferred_element_type=F32)
        o_ref[0, :, c0:c1] = (y * sc_ref[:, c0:c1]).astype(o_ref.dtype)


def _pool_mixer(z, w_pool, scale, mix_a, tt=512):
    B, T, _ = z.shape
    tt = min(tt, T)
    group = mix_a // len(POOL_WINDOWS)
    halo_blocks = tt // POOL_HALO
    est = 2 * (_nbytes((tt, mix_a), F32) + _nbytes((tt, mix_a), BF16)) + 4 * _nbytes((tt, mix_a), F32)
    return pl.pallas_call(
        functools.partial(_pool_body, tt=tt, group=group),
        grid=(B, T // tt),
        in_specs=[pl.BlockSpec((1, tt, mix_a), lambda b, t: (b, t, 0)),
                  pl.BlockSpec((1, POOL_HALO, mix_a),
                               lambda b, t: (b, jnp.maximum(t * halo_blocks - 1, 0), 0)),
                  pl.BlockSpec((len(POOL_WINDOWS), group, group), lambda b, t: (0, 0, 0)),
                  pl.BlockSpec((1, mix_a), lambda b, t: (0, 0))],
        out_specs=pl.BlockSpec((1, tt, mix_a), lambda b, t: (b, t, 0)),
        out_shape=jax.ShapeDtypeStruct((B, T, mix_a), BF16),
        scratch_shapes=[pltpu.VMEM((tt + POOL_HALO, mix_a), F32)],
        compiler_params=_params(("parallel", "parallel"), est),
        name="pool_mixer",
    )(z, z, w_pool, scale.reshape(1, mix_a).astype(F32))


def _hgrn_subtile(qf, kk, v, logf, st):
    n = qf.shape[0]
    C, R = HG_CHUNK, HG_DIAG
    row = lax.broadcasted_iota(jnp.int32, (n, HG_HEAD), 0)

    pos = row & (C - 1)
    b = logf
    sh = 1
    while sh < C:
        b = b + jnp.where(pos >= sh, pltpu.roll(b, sh, axis=0), 0.0)
        sh *= 2

    nb = n // R
    b3, q3, k3, v3 = (a.reshape(nb, R, HG_HEAD) for a in (b, qf, kk, v))
    tpos = lax.broadcasted_iota(jnp.int32, (1, R, 1), 1)
    od = jnp.zeros((nb, R, HG_HEAD), F32)
    for s in range(R):
        w = q3 * jnp.exp(b3 - b3[:, s:s + 1, :]) * k3[:, s:s + 1, :]
        a = jnp.sum(w, axis=-1, keepdims=True)
        a = jnp.where(tpos >= s, a, 0.0)
        od = od + a * v3[:, s:s + 1, :]
    o = od.reshape(n, HG_HEAD)

    same = lax.broadcasted_iota(jnp.int32, (n, n), 0) ^ lax.broadcasted_iota(jnp.int32, (n, n), 1)
    a_off = jnp.zeros((n, n), F32)
    h = R
    while h < C:
        g = 2 * h
        ref = b.reshape(n // g, g, HG_HEAD)[:, h - 1:h, :]
        ref = jnp.broadcast_to(ref, (n // g, g, HG_HEAD)).reshape(n, HG_HEAD)
        e = jnp.exp(-jnp.abs(b - ref))
        right = (row & (g - 1)) >= h
        qt = jnp.where(right, qf * e, 0.0).astype(BF16)
        kt = jnp.where(right, 0.0, kk * e).astype(BF16)
        a = lax.dot_general(qt, kt, (((1,), (1,)), ((), ())), preferred_element_type=F32)
        a_off = a_off + jnp.where(same < g, a, 0.0)
        h = g
    v_bf = v.astype(BF16)
    o = o + jnp.dot(a_off.astype(BF16), v_bf, preferred_element_type=F32)

    nc = n // C
    bc = b.reshape(nc, C, HG_HEAD)
    b_last = bc[:, C - 1:C, :]
    q_in = (qf * jnp.exp(b)).astype(BF16)
    k_out = (kk.reshape(nc, C, HG_HEAD) * jnp.exp(b_last - bc)).astype(BF16)
    dec = jnp.exp(b_last)
    pieces = []
    for c in range(nc):
        rows = slice(c * C, (c + 1) * C)
        o_int = lax.dot_general(q_in[rows], st.astype(BF16), (((1,), (1,)), ((), ())),
                                preferred_element_type=F32)
        pieces.append(o[rows] + o_int)
        upd = lax.dot_general(v_bf[rows], k_out[c], (((0,), (0,)), ((), ())),
                              preferred_element_type=F32)
        st = st * dec[c] + upd
    return jnp.concatenate(pieces, axis=0), st


def _hgrn_body(lbt_ref, ng_ref, q_ref, f_ref, i_ref, g_ref, o_ref, st_ref, *, layer, tt, sub):
    t = pl.program_id(2)

    @pl.when(t == 0)
    def _():
        st_ref[...] = jnp.zeros_like(st_ref)

    lbt = lbt_ref[...]
    e = jnp.exp(lbt - jnp.max(lbt, axis=0, keepdims=True))
    sm = e / jnp.sum(e, axis=0, keepdims=True)
    lb = jnp.sum(sm[1:layer + 2], axis=0, keepdims=True)

    st = st_ref[...]
    for c in range(tt // sub):
        rows = slice(c * sub, (c + 1) * sub)
        f = lb + (1.0 - lb) * jax.nn.sigmoid(f_ref[0, rows, :])
        q = q_ref[0, rows, :]
        qf = q * jax.nn.sigmoid(q) * (HG_HEAD ** -0.5)
        o, st = _hgrn_subtile(qf, 1.0 - f, i_ref[0, rows, :], jnp.log(f), st)
        ms = jnp.mean(o * o, axis=-1, keepdims=True)
        o = o * lax.rsqrt(ms + EPS) * ng_ref[...]
        g = g_ref[0, rows, :]
        o_ref[0, rows, :] = (o * (g * jax.nn.sigmoid(g))).astype(o_ref.dtype)
    st_ref[...] = st


def _hgrn_mixer(z, lb_table, norm_g, layer, mix_a, mix_b, tt=512):
    B, T, _ = z.shape
    tt = min(tt, T)
    sub = min(HG_SUB, tt)
    heads = mix_b // HG_HEAD
    c0 = mix_a // HG_HEAD
    est = 2 * 5 * _nbytes((tt, HG_HEAD), F32) + 32 * _nbytes((sub, HG_HEAD), F32) + 6 * _nbytes((sub, sub), F32)

    def col(off):
        return pl.BlockSpec((1, tt, HG_HEAD), lambda b, h, t, off=off: (b, t, c0 + off * heads + h))

    return pl.pallas_call(
        functools.partial(_hgrn_body, layer=layer, tt=tt, sub=sub),
        grid=(B, heads, T // tt),
        in_specs=[pl.BlockSpec((lb_table.shape[0], HG_HEAD), lambda b, h, t: (0, h)),
                  pl.BlockSpec((1, HG_HEAD), lambda b, h, t: (0, 0)),
                  col(0), col(1), col(2), col(3)],
        out_specs=pl.BlockSpec((1, tt, HG_HEAD), lambda b, h, t: (b, t, h)),
        out_shape=jax.ShapeDtypeStruct((B, T, mix_b), BF16),
        scratch_shapes=[pltpu.VMEM((HG_HEAD, HG_HEAD), F32)],
        compiler_params=_params(("parallel", "parallel", "arbitrary"), est),
        name="hgrn2_mixer",
    )(lb_table.astype(F32), norm_g.reshape(1, HG_HEAD).astype(F32), z, z, z, z)


def _fgate_body(h_ref, w_ref, b_ref, o_ref, carry_ref, *, tt):
    t = pl.program_id(1)

    @pl.when(t == 0)
    def _():
        carry_ref[...] = jnp.zeros_like(carry_ref)

    fl = jnp.dot(h_ref[0], w_ref[...], preferred_element_type=F32) + b_ref[...]
    c = (jnp.minimum(fl, 0.0) - jnp.log1p(jnp.exp(-jnp.abs(fl)))) * LOG2E
    row = lax.broadcasted_iota(jnp.int32, c.shape, 0)
    sh = 1
    while sh < tt:
        c = c + jnp.where(row >= sh, pltpu.roll(c, sh, axis=0), 0.0)
        sh *= 2
    c = c + carry_ref[...]
    o_ref[0] = c
    carry_ref[...] = c[tt - 1:tt, :]


def _fox_gates(h, wf, bf, tt=512):
    B, T, D = h.shape
    tt = min(tt, T)
    est = 2 * (_nbytes((tt, D), BF16) + _nbytes((D, LANES), BF16) + _nbytes((tt, LANES), F32)) \
        + 8 * _nbytes((tt, LANES), F32)
    return pl.pallas_call(
        functools.partial(_fgate_body, tt=tt),
        grid=(B, T // tt),
        in_specs=[pl.BlockSpec((1, tt, D), lambda b, t: (b, t, 0)),
                  pl.BlockSpec((D, LANES), lambda b, t: (0, 0)),
                  pl.BlockSpec((1, LANES), lambda b, t: (0, 0))],
        out_specs=pl.BlockSpec((1, tt, LANES), lambda b, t: (b, t, 0)),
        out_shape=jax.ShapeDtypeStruct((B, T, LANES), F32),
        scratch_shapes=[pltpu.VMEM((1, LANES), F32)],
        compiler_params=_params(("parallel", "arbitrary"), est),
        name="fox_gates",
    )(h, wf, bf)


def _fox_body(q_ref, k_ref, v_ref, f_ref, fr_ref, o_ref, vt_ref, fk_ref, p_ref, acc_ref, *, tq, seq, group):
    hg = pl.program_id(1)
    i = pl.program_id(2)
    reps = tq // LANES
    heads = range(group)

    def head_cols(g):
        return slice(g * FOX_HEAD, (g + 1) * FOX_HEAD)

    @pl.when(i == 0)
    def _():
        for c in range(seq // tq):
            rows = slice(c * tq, (c + 1) * tq)
            fblk = f_ref[0, rows, :]
            lane = lax.broadcasted_iota(jnp.int32, fblk.shape, 1)
            for g in heads:
                vt_ref[g, :, rows] = v_ref[0, rows, head_cols(g)].T
                col = jnp.sum(jnp.where(lane == hg * group + g, fblk, 0.0), axis=-1, keepdims=True)
                fk_ref[g, rows, :] = jnp.broadcast_to(col, fblk.shape)

    qt = [q_ref[0, :, head_cols(g)].T for g in heads]
    fq = [fr_ref[0, g, i] for g in heads]

    def scores(g, j):
        start = pl.multiple_of(j * tq, tq)
        return jnp.dot(k_ref[0, pl.ds(start, tq), head_cols(g)], qt[g], preferred_element_type=F32)

    def softmax_block(g, j, st, m_prev, l_prev, diagonal):
        start = pl.multiple_of(j * tq, tq)
        t = st - jnp.concatenate([fk_ref[g, pl.ds(start, tq), :]] * reps, axis=1)
        if diagonal:
            key = lax.broadcasted_iota(jnp.int32, (tq, tq), 0)
            qry = lax.broadcasted_iota(jnp.int32, (tq, tq), 1)
            t = jnp.where(key <= qry, t, -jnp.inf)
        m_new = jnp.maximum(m_prev, jnp.max(t, axis=0, keepdims=True) + fq[g])
        alpha = jnp.exp2(m_prev - m_new)
        p = jnp.exp2(t - (m_new - fq[g]))
        l_new = alpha * l_prev + jnp.sum(p, axis=0, keepdims=True)
        return p.astype(BF16), m_new, l_new, alpha

    def weighted_values(g, j, p):
        start = pl.multiple_of(j * tq, tq)
        return jnp.dot(vt_ref[g, :, pl.ds(start, tq)], p, preferred_element_type=F32)

    def block(j, j_prev, stats, first):
        out = []
        for g in heads:
            m, l = stats[g]
            st = scores(g, j)
            if first:
                p, m, l, _ = softmax_block(g, j, st, m, l, True)
                acc_ref[g] = jnp.zeros((FOX_HEAD, tq), F32)
            else:
                pv_prev = weighted_values(g, j_prev, p_ref[g])
                p, m, l, alpha = softmax_block(g, j, st, m, l, False)
                acc_ref[g] = (acc_ref[g] + pv_prev) * alpha
            p_ref[g] = p
            out.append((m, l))
        return tuple(out)

    stats = tuple((jnp.full((1, tq), -jnp.inf, F32), jnp.zeros((1, tq), F32)) for _ in heads)
    stats = block(i, i, stats, True)
    stats = lax.fori_loop(0, i, lambda j, c: block(j, jnp.where(j == 0, i, j - 1), c, False), stats)
    j_last = jnp.where(i == 0, i, i - 1)
    for g in heads:
        acc = acc_ref[g] + weighted_values(g, j_last, p_ref[g])
        o_ref[0, :, head_cols(g)] = (acc / stats[g][1]).astype(o_ref.dtype).T


def _fox_attention(qkv, f, frow, heads, tq, group=FOX_GROUP):
    B, T, _ = qkv.shape
    nq = T // tq
    gw = group * FOX_HEAD
    ngrp = heads // group
    est = 2 * (2 * _nbytes((T, gw), BF16) + 2 * _nbytes((tq, gw), BF16)
               + _nbytes((T, LANES), F32) + group * _nbytes((nq, 8, tq), F32)) \
        + group * (_nbytes((T, FOX_HEAD), BF16) + _nbytes((T, LANES), F32) + 5 * _nbytes((tq, tq), F32))
    return pl.pallas_call(
        functools.partial(_fox_body, tq=tq, seq=T, group=group),
        grid=(B, ngrp, nq),
        in_specs=[pl.BlockSpec((1, tq, gw), lambda b, h, i: (b, i, h)),
                  pl.BlockSpec((1, T, gw), lambda b, h, i: (b, 0, ngrp + h)),
                  pl.BlockSpec((1, T, gw), lambda b, h, i: (b, 0, 2 * ngrp + h)),
                  pl.BlockSpec((1, T, LANES), lambda b, h, i: (b, 0, 0)),
                  pl.BlockSpec((1, group, nq, 1, tq), lambda b, h, i: (b, h, 0, 0, 0))],
        out_specs=pl.BlockSpec((1, tq, gw), lambda b, h, i: (b, i, h)),
        out_shape=jax.ShapeDtypeStruct((B, T, heads * FOX_HEAD), BF16),
        scratch_shapes=[pltpu.VMEM((group, FOX_HEAD, T), BF16),
                        pltpu.VMEM((group, T, LANES), F32),
                        pltpu.VMEM((group, tq, tq), BF16),
                        pltpu.VMEM((group, FOX_HEAD, tq), F32)],
        compiler_params=_params(("parallel", "parallel", "arbitrary"), est),
        name="fox_attention",
    )(qkv, qkv, qkv, f, frow)


def _xattn_body(q_ref, kv_ref, o_ref, *, d_model):
    hd = d_model // XA_HEADS
    for h in range(XA_HEADS):
        q = q_ref[0, :, h * hd:(h + 1) * hd]
        k = kv_ref[0, :, h * hd:(h + 1) * hd]
        v = kv_ref[0, :, d_model + h * hd:d_model + (h + 1) * hd]
        s = lax.dot_general(q, k, (((1,), (1,)), ((), ())), preferred_element_type=F32)
        p = jnp.exp(s - jnp.max(s, axis=-1, keepdims=True))
        p = p / jnp.sum(p, axis=-1, keepdims=True)
        o = jnp.dot(p.astype(BF16), v, preferred_element_type=F32)
        o_ref[0, :, h * hd:(h + 1) * hd] = o.astype(o_ref.dtype)


def _xattn_core(q, kv, tq=512):
    B, T, D = q.shape
    M = kv.shape[1]
    tq = min(tq, T)
    est = 2 * (2 * _nbytes((tq, D), BF16) + _nbytes((M, 2 * D), BF16)) + 6 * _nbytes((tq, M), F32) \
        + 2 * _nbytes((tq, D // XA_HEADS), F32)
    return pl.pallas_call(
        functools.partial(_xattn_body, d_model=D),
        grid=(B, T // tq),
        in_specs=[pl.BlockSpec((1, tq, D), lambda b, t: (b, t, 0)),
                  pl.BlockSpec((1, M, 2 * D), lambda b, t: (b, 0, 0))],
        out_specs=pl.BlockSpec((1, tq, D), lambda b, t: (b, t, 0)),
        out_shape=jax.ShapeDtypeStruct((B, T, D), BF16),
        compiler_params=_params(("parallel", "parallel"), est),
        name="xattn_core",
    )(q, kv)


def kernel(x, mem, lb_table, ev_norm, ev_w_in, ev_w_pool, ev_pool_scale, ev_hg_norm, ev_w_out,
           od_norm, od_w_in, od_b_f, od_w_out, xa_norm, xa_mem_norm, xa_wq, xa_wkv, xa_wo,
           ffn_norm, ffn_w_gate, ffn_w_up, ffn_w_down, final_norm):
    B, T, D = x.shape
    M = B * T
    depth = xa_norm.shape[0]
    mix_a = ev_pool_scale.shape[1]
    mix_b = lb_table.shape[1]
    fox_heads = od_b_f.shape[1]
    n_mem = mem.shape[1]
    xa_scale = (D // XA_HEADS) ** -0.5
    fox_tq = _pick_tile(T, 512)

    xs = x.reshape(M, D)
    mem2 = mem.reshape(B * n_mem, D)
    w_down = ffn_w_down.astype(BF16)
    ev_wo, od_wo, xa_wo_b = ev_w_out.astype(BF16), od_w_out.astype(BF16), xa_wo.astype(BF16)
    for l in range(depth):
        if l % 2 == 0:
            e = l // 2
            h = _rmsnorm(xs, ev_norm[e], BF16)
            z = _matmul(h, ev_w_in, e, ev_w_in.shape[2], F32).reshape(B, T, -1)
            ya = _pool_mixer(z, ev_w_pool[e].astype(BF16), ev_pool_scale[e], mix_a)
            yb = _hgrn_mixer(z, lb_table, ev_hg_norm[e], l, mix_a, mix_b)
            xs, h = _matmul_residual_norm([ya.reshape(M, mix_a), yb.reshape(M, mix_b)], ev_wo, e, xs, xa_norm[l])
        else:
            o = l // 2
            h = _rmsnorm(xs, od_norm[o], BF16)
            qscale = jnp.concatenate([jnp.full((D,), LOG2E * FOX_HEAD ** -0.5, F32), jnp.ones((2 * D,), F32)])
            qkv = _matmul(h, od_w_in, o, 3 * D, BF16, colscale=qscale).reshape(B, T, 3 * D)
            wf = jnp.zeros((D, LANES), BF16).at[:, :fox_heads].set(od_w_in[o, :, 3 * D:].astype(BF16))
            bf = jnp.zeros((1, LANES), F32).at[0, :fox_heads].set(od_b_f[o].astype(F32))
            f = _fox_gates(h.reshape(B, T, D), wf, bf)
            frow = f[..., :fox_heads].transpose(0, 2, 1).reshape(B, fox_heads, T // fox_tq, 1, fox_tq)
            y = _fox_attention(qkv, f, frow, fox_heads, fox_tq).reshape(M, D)
            xs, h = _matmul_residual_norm([y], od_wo, o, xs, xa_norm[l])

        mn = _rmsnorm(mem2, xa_mem_norm[l], BF16)
        q = _matmul(h, xa_wq, l, D, BF16, colscale=jnp.full((D,), xa_scale, F32))
        kv = _matmul(mn, xa_wkv, l, 2 * D, BF16)
        a = _xattn_core(q.reshape(B, T, D), kv.reshape(B, n_mem, 2 * D)).reshape(M, D)
        xs, h = _matmul_residual_norm([a], xa_wo_b, l, xs, ffn_norm[l])

        act = _matmul_swiglu(h, ffn_w_gate, ffn_w_up, l)
        xs = _matmul_residual(act, w_down, l, xs)
    return _rmsnorm(xs, final_norm, x.dtype).reshape(B, T, D)
```

```python
import functools

import jax
import jax.numpy as jnp
from jax import lax
from jax.experimental import pallas as pl
from jax.experimental.pallas import tpu as pltpu

F32 = jnp.float32
BF16 = jnp.bfloat16

EPS = 1e-6
N_MEM = 256
POOL_WINDOWS = (2, 4, 8, 16)
POOL_HALO = 16
HG_HEAD = 128
HG_CHUNK = 64
HG_DIAG = 8
HG_SUB = 256
FOX_HEAD = 128
FOX_GROUP = 4
XA_HEADS = 4

LOG2E = 1.4426950408889634
LANES = 128
VMEM_CAP = 56 * 1024 * 1024
VMEM_FLOOR = 32 * 1024 * 1024


def _params(semantics, vmem_estimate):
    limit = int(min(max(vmem_estimate * 5 // 4, VMEM_FLOOR), VMEM_CAP))
    return pltpu.CompilerParams(dimension_semantics=semantics, vmem_limit_bytes=limit)


def _nbytes(shape, dtype):
    n = jnp.dtype(dtype).itemsize
    for s in shape:
        n *= s
    return n


def _rmsnorm_body(x_ref, g_ref, o_ref):
    x = x_ref[...]
    ms = jnp.mean(x * x, axis=-1, keepdims=True)
    o_ref[...] = (x * lax.rsqrt(ms + EPS) * g_ref[...]).astype(o_ref.dtype)


def _rmsnorm(x, g, out_dtype, tm=512):
    M, D = x.shape
    tm = min(tm, M)
    est = 2 * (_nbytes((tm, D), x.dtype) + _nbytes((tm, D), out_dtype)) + 3 * _nbytes((tm, D), F32)
    return pl.pallas_call(
        _rmsnorm_body,
        grid=(M // tm,),
        in_specs=[pl.BlockSpec((tm, D), lambda i: (i, 0)),
                  pl.BlockSpec((1, D), lambda i: (0, 0))],
        out_specs=pl.BlockSpec((tm, D), lambda i: (i, 0)),
        out_shape=jax.ShapeDtypeStruct((M, D), out_dtype),
        compiler_params=_params(("parallel",), est),
        name="rmsnorm",
    )(x, g.reshape(1, D).astype(F32))


def _with_bf16_weights(w_refs, wb_refs, compute):
    if not wb_refs:
        compute(*(w[...] for w in w_refs))
        return
    i = pl.program_id(1)

    @pl.when(i == 0)
    def _():
        for w, wb in zip(w_refs, wb_refs):
            wb[...] = w[...].astype(BF16)

    @pl.when(i > 0)
    def _():
        compute(*(wb[...] for wb in wb_refs))


def _row_rsqrt(ss_ref, width):
    return lax.rsqrt(jnp.sum(ss_ref[...], axis=0) * (1.0 / width) + EPS)


def _mm_body(x_ref, w_ref, cs_ref, *rest, norm_width):
    if norm_width:
        ss_ref, o_ref, *wb = rest
    else:
        o_ref, *wb = rest

    def compute(w):
        acc = jnp.dot(x_ref[...], w, preferred_element_type=F32)
        if norm_width:
            acc = acc * _row_rsqrt(ss_ref, norm_width)
        o_ref[...] = (acc * cs_ref[...]).astype(o_ref.dtype)

    _with_bf16_weights((w_ref,), wb, compute)


def _mm_res_body(x_ref, w_ref, r_ref, *rest, with_gain):
    if with_gain:
        g_ref, o_ref, hp_ref, ss_ref, *wb = rest
    else:
        o_ref, *wb = rest

    def compute(w):
        acc = r_ref[...] + jnp.dot(x_ref[...], w, preferred_element_type=F32)
        o_ref[...] = acc
        if with_gain:
            hp_ref[...] = (acc * g_ref[...]).astype(hp_ref.dtype)
            ss_ref[...] = jnp.sum(acc * acc, axis=-1, keepdims=True)

    _with_bf16_weights((w_ref,), wb, compute)


def _mm_swiglu_body(x_ref, wg_ref, wu_ref, o_ref, *wb):
    def compute(wg, wu):
        x = x_ref[...]
        g = jnp.dot(x, wg, preferred_element_type=F32)
        u = jnp.dot(x, wu, preferred_element_type=F32)
        o_ref[...] = (g * jax.nn.sigmoid(g) * u).astype(o_ref.dtype)

    _with_bf16_weights((wg_ref, wu_ref), wb, compute)


def _pick_tile(n, pref):
    t = min(pref, n)
    while n % t:
        t //= 2
    return t


def _weight_spec(layer, K, tn):
    return pl.BlockSpec((None, K, tn), lambda j, i: (layer, 0, j))


def _weight_scratch(w, K, tn):
    return [] if w.dtype == BF16 else [pltpu.VMEM((K, tn), BF16)]


def _weight_bytes(w, K, tn):
    return 2 * _nbytes((K, tn), w.dtype) + (0 if w.dtype == BF16 else _nbytes((K, tn), BF16))


def _row_steps(w, M, tm):
    extra = 0 if w.dtype == BF16 else 1
    return M // tm + extra, lambda i: jnp.maximum(i - extra, 0)


def _matmul(x, w, layer, n_cols, out_dtype, colscale=None, row_ss=None, tm=1024, tn=1024):
    M, K = x.shape
    tm, tn = _pick_tile(M, tm), _pick_tile(n_cols, tn)
    if colscale is None:
        colscale = jnp.ones((n_cols,), F32)
    steps, row = _row_steps(w, M, tm)
    ss_specs = [] if row_ss is None else [pl.BlockSpec((row_ss.shape[0], tm, 1), lambda j, i: (0, row(i), 0))]
    ss_args = [] if row_ss is None else [row_ss]
    est = 2 * (_nbytes((tm, K), x.dtype) + _nbytes((tm, tn), out_dtype)) + _weight_bytes(w, K, tn) \
        + 2 * _nbytes((tm, tn), F32)
    return pl.pallas_call(
        functools.partial(_mm_body, norm_width=0 if row_ss is None else K),
        grid=(n_cols // tn, steps),
        in_specs=[pl.BlockSpec((tm, K), lambda j, i: (row(i), 0)),
                  _weight_spec(layer, K, tn),
                  pl.BlockSpec((1, tn), lambda j, i: (0, j))] + ss_specs,
        out_specs=pl.BlockSpec((tm, tn), lambda j, i: (row(i), j)),
        out_shape=jax.ShapeDtypeStruct((M, n_cols), out_dtype),
        scratch_shapes=_weight_scratch(w, K, tn),
        compiler_params=_params(("parallel", "arbitrary"), est),
        name="matmul",
    )(x, w, colscale.reshape(1, n_cols), *ss_args)


def _matmul_residual(x, w, layer, r, gain=None, tn=1024):
    M, K = x.shape
    N = w.shape[2]
    tm = 1024 if K <= 2048 else 512
    tm, tn = _pick_tile(M, tm), _pick_tile(N, tn)
    steps, row = _row_steps(w, M, tm)
    est = 2 * (_nbytes((tm, K), x.dtype) + 2 * _nbytes((tm, tn), F32)) + _weight_bytes(w, K, tn) \
        + 2 * _nbytes((tm, tn), F32)
    tile = pl.BlockSpec((tm, tn), lambda j, i: (row(i), j))
    in_specs = [pl.BlockSpec((tm, K), lambda j, i: (row(i), 0)), _weight_spec(layer, K, tn), tile]
    out_specs, out_shape, args = tile, jax.ShapeDtypeStruct((M, N), F32), [x, w, r]
    if gain is not None:
        in_specs.append(pl.BlockSpec((1, tn), lambda j, i: (0, j)))
        args.append(gain.reshape(1, N).astype(F32))
        out_specs = [tile, tile, pl.BlockSpec((None, tm, 1), lambda j, i: (j, row(i), 0))]
        out_shape = [out_shape, jax.ShapeDtypeStruct((M, N), BF16), jax.ShapeDtypeStruct((N // tn, M, 1), F32)]
    return pl.pallas_call(
        functools.partial(_mm_res_body, with_gain=gain is not None),
        grid=(N // tn, steps),
        in_specs=in_specs,
        out_specs=out_specs,
        out_shape=out_shape,
        scratch_shapes=_weight_scratch(w, K, tn),
        compiler_params=_params(("parallel", "arbitrary"), est),
        name="matmul_residual",
    )(*args)


def _matmul_swiglu(x, wg, wu, layer, tm=1024, tn=512):
    M, K = x.shape
    N = wg.shape[2]
    tm, tn = _pick_tile(M, tm), _pick_tile(N, tn)
    steps, row = _row_steps(wg, M, tm)
    est = 2 * (_nbytes((tm, K), x.dtype) + _nbytes((tm, tn), BF16)) + 2 * _weight_bytes(wg, K, tn) \
        + 4 * _nbytes((tm, tn), F32)
    return pl.pallas_call(
        _mm_swiglu_body,
        grid=(N // tn, steps),
        in_specs=[pl.BlockSpec((tm, K), lambda j, i: (row(i), 0)),
                  _weight_spec(layer, K, tn),
                  _weight_spec(layer, K, tn)],
        out_specs=pl.BlockSpec((tm, tn), lambda j, i: (row(i), j)),
        out_shape=jax.ShapeDtypeStruct((M, N), BF16),
        scratch_shapes=_weight_scratch(wg, K, tn) + _weight_scratch(wu, K, tn),
        compiler_params=_params(("parallel", "arbitrary"), est),
        name="matmul_swiglu",
    )(x, wg, wu)


def _mm_res_norm_body(*refs, n_parts):
    x_refs = refs[:n_parts]
    w_ref, r_ref, g_ref, o_ref, h_ref = refs[n_parts:]
    acc = r_ref[...]
    k0 = 0
    for x_ref in x_refs:
        kw = x_ref.shape[1]
        acc = acc + jnp.dot(x_ref[...], w_ref[k0:k0 + kw, :], preferred_element_type=F32)
        k0 += kw
    o_ref[...] = acc
    ms = jnp.mean(acc * acc, axis=-1, keepdims=True)
    h_ref[...] = (acc * lax.rsqrt(ms + EPS) * g_ref[...]).astype(h_ref.dtype)


def _matmul_residual_norm(x_parts, w, layer, r, gain, tm=512):
    M, N = r.shape
    K = w.shape[1]
    tm = _pick_tile(M, tm)
    est = 2 * (_nbytes((tm, K), BF16) + _nbytes((K, N), w.dtype) + 2 * _nbytes((tm, N), F32) + _nbytes((tm, N), BF16)) \
        + 3 * _nbytes((tm, N), F32)
    return pl.pallas_call(
        functools.partial(_mm_res_norm_body, n_parts=len(x_parts)),
        grid=(M // tm,),
        in_specs=[pl.BlockSpec((tm, xp.shape[1]), lambda i: (i, 0)) for xp in x_parts]
        + [pl.BlockSpec((None, K, N), lambda i: (layer, 0, 0)),
           pl.BlockSpec((tm, N), lambda i: (i, 0)),
           pl.BlockSpec((1, N), lambda i: (0, 0))],
        out_specs=[pl.BlockSpec((tm, N), lambda i: (i, 0)),
                   pl.BlockSpec((tm, N), lambda i: (i, 0))],
        out_shape=[jax.ShapeDtypeStruct((M, N), F32),
                   jax.ShapeDtypeStruct((M, N), BF16)],
        compiler_params=_params(("parallel",), est),
        name="matmul_residual_norm",
    )(*x_parts, w, r, gain.reshape(1, N).astype(F32))


def _pool_body(u_ref, halo_ref, w_ref, sc_ref, o_ref, ext_ref, *, tt, group):
    t = pl.program_id(1)
    u = u_ref[0].astype(F32)
    ext_ref[0:POOL_HALO, :] = jnp.where(t > 0, halo_ref[0].astype(F32), 0.0)
    ext_ref[POOL_HALO:POOL_HALO + tt, :] = u
    pos = t * tt + lax.broadcasted_iota(jnp.int32, (tt, 1), 0)
    for gi, win in enumerate(POOL_WINDOWS):
        c0, c1 = gi * group, (gi + 1) * group
        tok = u[:, c0:c1]
        acc = tok
        for d in range(1, win):
            acc = acc + ext_ref[POOL_HALO - d:POOL_HALO - d + tt, c0:c1]
        cnt = jnp.minimum(pos + 1, win).astype(F32)
        p = acc / cnt - tok
        y = jnp.dot(p.astype(BF16), w_ref[gi], preferred_element_type=F32)
        o_ref[0, :, c0:c1] = (y * sc_ref[:, c0:c1]).astype(o_ref.dtype)


def _pool_mixer(z, w_pool, scale, mix_a, tt=512):
    B, T, _ = z.shape
    tt = min(tt, T)
    group = mix_a // len(POOL_WINDOWS)
    halo_blocks = tt // POOL_HALO
    est = 2 * (_nbytes((tt, mix_a), F32) + _nbytes((tt, mix_a), BF16)) + 4 * _nbytes((tt, mix_a), F32)
    return pl.pallas_call(
        functools.partial(_pool_body, tt=tt, group=group),
        grid=(B, T // tt),
        in_specs=[pl.BlockSpec((1, tt, mix_a), lambda b, t: (b, t, 0)),
                  pl.BlockSpec((1, POOL_HALO, mix_a),
                               lambda b, t: (b, jnp.maximum(t * halo_blocks - 1, 0), 0)),
                  pl.BlockSpec((len(POOL_WINDOWS), group, group), lambda b, t: (0, 0, 0)),
                  pl.BlockSpec((1, mix_a), lambda b, t: (0, 0))],
        out_specs=pl.BlockSpec((1, tt, mix_a), lambda b, t: (b, t, 0)),
        out_shape=jax.ShapeDtypeStruct((B, T, mix_a), BF16),
        scratch_shapes=[pltpu.VMEM((tt + POOL_HALO, mix_a), F32)],
        compiler_params=_params(("parallel", "parallel"), est),
        name="pool_mixer",
    )(z, z, w_pool, scale.reshape(1, mix_a).astype(F32))


def _hgrn_subtile(qf, kk, v, logf, st):
    n = qf.shape[0]
    C, R = HG_CHUNK, HG_DIAG
    row = lax.broadcasted_iota(jnp.int32, (n, HG_HEAD), 0)

    pos = row & (C - 1)
    b = logf
    sh = 1
    while sh < C:
        b = b + jnp.where(pos >= sh, pltpu.roll(b, sh, axis=0), 0.0)
        sh *= 2

    nb = n // R
    b3, q3, k3, v3 = (a.reshape(nb, R, HG_HEAD) for a in (b, qf, kk, v))
    tpos = lax.broadcasted_iota(jnp.int32, (1, R, 1), 1)
    od = jnp.zeros((nb, R, HG_HEAD), F32)
    for s in range(R):
        w = q3 * jnp.exp(b3 - b3[:, s:s + 1, :]) * k3[:, s:s + 1, :]
        a = jnp.sum(w, axis=-1, keepdims=True)
        a = jnp.where(tpos >= s, a, 0.0)
        od = od + a * v3[:, s:s + 1, :]
    o = od.reshape(n, HG_HEAD)

    same = lax.broadcasted_iota(jnp.int32, (n, n), 0) ^ lax.broadcasted_iota(jnp.int32, (n, n), 1)
    a_off = jnp.zeros((n, n), F32)
    h = R
    while h < C:
        g = 2 * h
        ref = b.reshape(n // g, g, HG_HEAD)[:, h - 1:h, :]
        ref = jnp.broadcast_to(ref, (n // g, g, HG_HEAD)).reshape(n, HG_HEAD)
        e = jnp.exp(-jnp.abs(b - ref))
        right = (row & (g - 1)) >= h
        qt = jnp.where(right, qf * e, 0.0).astype(BF16)
        kt = jnp.where(right, 0.0, kk * e).astype(BF16)
        a = lax.dot_general(qt, kt, (((1,), (1,)), ((), ())), preferred_element_type=F32)
        a_off = a_off + jnp.where(same < g, a, 0.0)
        h = g
    v_bf = v.astype(BF16)
    o = o + jnp.dot(a_off.astype(BF16), v_bf, preferred_element_type=F32)

    nc = n // C
    bc = b.reshape(nc, C, HG_HEAD)
    b_last = bc[:, C - 1:C, :]
    q_in = (qf * jnp.exp(b)).astype(BF16)
    k_out = (kk.reshape(nc, C, HG_HEAD) * jnp.exp(b_last - bc)).astype(BF16)
    dec = jnp.exp(b_last)
    pieces = []
    for c in range(nc):
        rows = slice(c * C, (c + 1) * C)
        o_int = lax.dot_general(q_in[rows], st.astype(BF16), (((1,), (1,)), ((), ())),
                                preferred_element_type=F32)
        pieces.append(o[rows] + o_int)
        upd = lax.dot_general(v_bf[rows], k_out[c], (((0,), (0,)), ((), ())),
                              preferred_element_type=F32)
        st = st * dec[c] + upd
    return jnp.concatenate(pieces, axis=0), st


def _hgrn_body(lbt_ref, ng_ref, q_ref, f_ref, i_ref, g_ref, o_ref, st_ref, *, layer, tt, sub):
    t = pl.program_id(2)

    @pl.when(t == 0)
    def _():
        st_ref[...] = jnp.zeros_like(st_ref)

    lbt = lbt_ref[...]
    e = jnp.exp(lbt - jnp.max(lbt, axis=0, keepdims=True))
    sm = e / jnp.sum(e, axis=0, keepdims=True)
    lb = jnp.sum(sm[1:layer + 2], axis=0, keepdims=True)

    st = st_ref[...]
    for c in range(tt // sub):
        rows = slice(c * sub, (c + 1) * sub)
        f = lb + (1.0 - lb) * jax.nn.sigmoid(f_ref[0, rows, :].astype(F32))
        q = q_ref[0, rows, :].astype(F32)
        qf = q * jax.nn.sigmoid(q) * (HG_HEAD ** -0.5)
        o, st = _hgrn_subtile(qf, 1.0 - f, i_ref[0, rows, :].astype(F32), jnp.log(f), st)
        ms = jnp.mean(o * o, axis=-1, keepdims=True)
        o = o * lax.rsqrt(ms + EPS) * ng_ref[...]
        g = g_ref[0, rows, :].astype(F32)
        o_ref[0, rows, :] = (o * (g * jax.nn.sigmoid(g))).astype(o_ref.dtype)
    st_ref[...] = st


def _hgrn_mixer(z, lb_table, norm_g, layer, mix_a, mix_b, tt=512):
    B, T, _ = z.shape
    tt = min(tt, T)
    sub = min(HG_SUB, tt)
    heads = mix_b // HG_HEAD
    c0 = mix_a // HG_HEAD
    est = 2 * 5 * _nbytes((tt, HG_HEAD), F32) + 32 * _nbytes((sub, HG_HEAD), F32) + 6 * _nbytes((sub, sub), F32)

    def col(off):
        return pl.BlockSpec((1, tt, HG_HEAD), lambda b, h, t, off=off: (b, t, c0 + off * heads + h))

    return pl.pallas_call(
        functools.partial(_hgrn_body, layer=layer, tt=tt, sub=sub),
        grid=(B, heads, T // tt),
        in_specs=[pl.BlockSpec((lb_table.shape[0], HG_HEAD), lambda b, h, t: (0, h)),
                  pl.BlockSpec((1, HG_HEAD), lambda b, h, t: (0, 0)),
                  col(0), col(1), col(2), col(3)],
        out_specs=pl.BlockSpec((1, tt, HG_HEAD), lambda b, h, t: (b, t, h)),
        out_shape=jax.ShapeDtypeStruct((B, T, mix_b), BF16),
        scratch_shapes=[pltpu.VMEM((HG_HEAD, HG_HEAD), F32)],
        compiler_params=_params(("parallel", "parallel", "arbitrary"), est),
        name="hgrn2_mixer",
    )(lb_table.astype(F32), norm_g.reshape(1, HG_HEAD).astype(F32), z, z, z, z)


def _fgate_body(h_ref, w_ref, b_ref, *rest, tt, norm_width):
    if norm_width:
        ss_ref, o_ref, carry_ref = rest
    else:
        o_ref, carry_ref = rest
    t = pl.program_id(1)

    @pl.when(t == 0)
    def _():
        carry_ref[...] = jnp.zeros_like(carry_ref)

    fl = jnp.dot(h_ref[0], w_ref[...], preferred_element_type=F32)
    if norm_width:
        fl = fl * _row_rsqrt(ss_ref, norm_width)
    fl = fl + b_ref[...]
    c = (jnp.minimum(fl, 0.0) - jnp.log1p(jnp.exp(-jnp.abs(fl)))) * LOG2E
    row = lax.broadcasted_iota(jnp.int32, c.shape, 0)
    sh = 1
    while sh < tt:
        c = c + jnp.where(row >= sh, pltpu.roll(c, sh, axis=0), 0.0)
        sh *= 2
    c = c + carry_ref[...]
    o_ref[0] = c
    carry_ref[...] = c[tt - 1:tt, :]


def _fox_gates(h, wf, bf, row_ss=None, tt=512):
    B, T, D = h.shape
    tt = min(tt, T)
    ss_specs = [] if row_ss is None else [pl.BlockSpec((row_ss.shape[0], None, tt, 1), lambda b, t: (0, b, t, 0))]
    ss_args = [] if row_ss is None else [row_ss]
    est = 2 * (_nbytes((tt, D), BF16) + _nbytes((D, LANES), BF16) + _nbytes((tt, LANES), F32)) \
        + 8 * _nbytes((tt, LANES), F32)
    return pl.pallas_call(
        functools.partial(_fgate_body, tt=tt, norm_width=0 if row_ss is None else D),
        grid=(B, T // tt),
        in_specs=[pl.BlockSpec((1, tt, D), lambda b, t: (b, t, 0)),
                  pl.BlockSpec((D, LANES), lambda b, t: (0, 0)),
                  pl.BlockSpec((1, LANES), lambda b, t: (0, 0))] + ss_specs,
        out_specs=pl.BlockSpec((1, tt, LANES), lambda b, t: (b, t, 0)),
        out_shape=jax.ShapeDtypeStruct((B, T, LANES), F32),
        scratch_shapes=[pltpu.VMEM((1, LANES), F32)],
        compiler_params=_params(("parallel", "arbitrary"), est),
        name="fox_gates",
    )(h, wf, bf, *ss_args)


def _fox_body(q_ref, k_ref, v_ref, f_ref, fr_ref, o_ref, vt_ref, fk_ref, p_ref, acc_ref, *, tq, seq, group):
    hg = pl.program_id(1)
    i = pl.program_id(2)
    reps = tq // LANES
    heads = range(group)

    def head_cols(g):
        return slice(g * FOX_HEAD, (g + 1) * FOX_HEAD)

    @pl.when(i == 0)
    def _():
        for c in range(seq // tq):
            rows = slice(c * tq, (c + 1) * tq)
            fblk = f_ref[0, rows, :]
            lane = lax.broadcasted_iota(jnp.int32, fblk.shape, 1)
            for g in heads:
                vt_ref[g, :, rows] = v_ref[0, rows, head_cols(g)].T
                col = jnp.sum(jnp.where(lane == hg * group + g, fblk, 0.0), axis=-1, keepdims=True)
                fk_ref[g, rows, :] = jnp.broadcast_to(col, fblk.shape)

    qt = [q_ref[0, :, head_cols(g)].T for g in heads]
    fq = [fr_ref[0, g, i] for g in heads]

    def scores(g, j):
        start = pl.multiple_of(j * tq, tq)
        return jnp.dot(k_ref[0, pl.ds(start, tq), head_cols(g)], qt[g], preferred_element_type=F32)

    def softmax_block(g, j, st, m_prev, l_prev, diagonal):
        start = pl.multiple_of(j * tq, tq)
        t = st - jnp.concatenate([fk_ref[g, pl.ds(start, tq), :]] * reps, axis=1)
        if diagonal:
            key = lax.broadcasted_iota(jnp.int32, (tq, tq), 0)
            qry = lax.broadcasted_iota(jnp.int32, (tq, tq), 1)
            t = jnp.where(key <= qry, t, -jnp.inf)
        m_new = jnp.maximum(m_prev, jnp.max(t, axis=0, keepdims=True) + fq[g])
        alpha = jnp.exp2(m_prev - m_new)
        p = jnp.exp2(t - (m_new - fq[g]))
        l_new = alpha * l_prev + jnp.sum(p, axis=0, keepdims=True)
        return p.astype(BF16), m_new, l_new, alpha

    def weighted_values(g, j, p):
        start = pl.multiple_of(j * tq, tq)
        return jnp.dot(vt_ref[g, :, pl.ds(start, tq)], p, preferred_element_type=F32)

    def block(j, j_prev, stats, first):
        out = []
        for g in heads:
            m, l = stats[g]
            st = scores(g, j)
            if first:
                p, m, l, _ = softmax_block(g, j, st, m, l, True)
                acc_ref[g] = jnp.zeros((FOX_HEAD, tq), F32)
            else:
                pv_prev = weighted_values(g, j_prev, p_ref[g])
                p, m, l, alpha = softmax_block(g, j, st, m, l, False)
                acc_ref[g] = (acc_ref[g] + pv_prev) * alpha
            p_ref[g] = p
            out.append((m, l))
        return tuple(out)

    stats = tuple((jnp.full((1, tq), -jnp.inf, F32), jnp.zeros((1, tq), F32)) for _ in heads)
    stats = block(i, i, stats, True)
    stats = lax.fori_loop(0, i, lambda j, c: block(j, jnp.where(j == 0, i, j - 1), c, False), stats)
    j_last = jnp.where(i == 0, i, i - 1)
    for g in heads:
        acc = acc_ref[g] + weighted_values(g, j_last, p_ref[g])
        o_ref[0, :, head_cols(g)] = (acc / stats[g][1]).astype(o_ref.dtype).T


def _fox_attention(qkv, f, frow, heads, tq, group=FOX_GROUP):
    B, T, _ = qkv.shape
    nq = T // tq
    gw = group * FOX_HEAD
    ngrp = heads // group
    est = 2 * (2 * _nbytes((T, gw), BF16) + 2 * _nbytes((tq, gw), BF16)
               + _nbytes((T, LANES), F32) + group * _nbytes((nq, 8, tq), F32)) \
        + group * (_nbytes((T, FOX_HEAD), BF16) + _nbytes((T, LANES), F32) + 5 * _nbytes((tq, tq), F32))
    return pl.pallas_call(
        functools.partial(_fox_body, tq=tq, seq=T, group=group),
        grid=(B, ngrp, nq),
        in_specs=[pl.BlockSpec((1, tq, gw), lambda b, h, i: (b, i, h)),
                  pl.BlockSpec((1, T, gw), lambda b, h, i: (b, 0, ngrp + h)),
                  pl.BlockSpec((1, T, gw), lambda b, h, i: (b, 0, 2 * ngrp + h)),
                  pl.BlockSpec((1, T, LANES), lambda b, h, i: (b, 0, 0)),
                  pl.BlockSpec((1, group, nq, 1, tq), lambda b, h, i: (b, h, 0, 0, 0))],
        out_specs=pl.BlockSpec((1, tq, gw), lambda b, h, i: (b, i, h)),
        out_shape=jax.ShapeDtypeStruct((B, T, heads * FOX_HEAD), BF16),
        scratch_shapes=[pltpu.VMEM((group, FOX_HEAD, T), BF16),
                        pltpu.VMEM((group, T, LANES), F32),
                        pltpu.VMEM((group, tq, tq), BF16),
                        pltpu.VMEM((group, FOX_HEAD, tq), F32)],
        compiler_params=_params(("parallel", "parallel", "arbitrary"), est),
        name="fox_attention",
    )(qkv, qkv, qkv, f, frow)


def _xattn_body(q_ref, kv_ref, o_ref, *, d_model):
    hd = d_model // XA_HEADS
    for h in range(XA_HEADS):
        q = q_ref[0, :, h * hd:(h + 1) * hd]
        k = kv_ref[0, :, h * hd:(h + 1) * hd]
        v = kv_ref[0, :, d_model + h * hd:d_model + (h + 1) * hd]
        s = lax.dot_general(q, k, (((1,), (1,)), ((), ())), preferred_element_type=F32)
        p = jnp.exp(s - jnp.max(s, axis=-1, keepdims=True))
        p = p / jnp.sum(p, axis=-1, keepdims=True)
        o = jnp.dot(p.astype(BF16), v, preferred_element_type=F32)
        o_ref[0, :, h * hd:(h + 1) * hd] = o.astype(o_ref.dtype)


def _xattn_core(q, kv, tq=512):
    B, T, D = q.shape
    M = kv.shape[1]
    tq = min(tq, T)
    est = 2 * (2 * _nbytes((tq, D), BF16) + _nbytes((M, 2 * D), BF16)) + 6 * _nbytes((tq, M), F32) \
        + 2 * _nbytes((tq, D // XA_HEADS), F32)
    return pl.pallas_call(
        functools.partial(_xattn_body, d_model=D),
        grid=(B, T // tq),
        in_specs=[pl.BlockSpec((1, tq, D), lambda b, t: (b, t, 0)),
                  pl.BlockSpec((1, M, 2 * D), lambda b, t: (b, 0, 0))],
        out_specs=pl.BlockSpec((1, tq, D), lambda b, t: (b, t, 0)),
        out_shape=jax.ShapeDtypeStruct((B, T, D), BF16),
        compiler_params=_params(("parallel", "parallel"), est),
        name="xattn_core",
    )(q, kv)


def kernel(x, mem, lb_table, ev_norm, ev_w_in, ev_w_pool, ev_pool_scale, ev_hg_norm, ev_w_out,
           od_norm, od_w_in, od_b_f, od_w_out, xa_norm, xa_mem_norm, xa_wq, xa_wkv, xa_wo,
           ffn_norm, ffn_w_gate, ffn_w_up, ffn_w_down, final_norm):
    B, T, D = x.shape
    M = B * T
    depth = xa_norm.shape[0]
    mix_a = ev_pool_scale.shape[1]
    mix_b = lb_table.shape[1]
    fox_heads = od_b_f.shape[1]
    n_mem = mem.shape[1]
    xa_scale = (D // XA_HEADS) ** -0.5
    fox_tq = _pick_tile(T, 512)

    xs = x.reshape(M, D)
    mem2 = mem.reshape(B * n_mem, D)
    w_down = ffn_w_down.astype(BF16)
    ev_wo, od_wo, xa_wo_b = ev_w_out.astype(BF16), od_w_out.astype(BF16), xa_wo.astype(BF16)
    pre = None
    for l in range(depth):
        if pre is None:
            h, ss = _rmsnorm(xs, ev_norm[l // 2] if l % 2 == 0 else od_norm[l // 2], BF16), None
        else:
            h, ss = pre
        if l % 2 == 0:
            e = l // 2
            z = _matmul(h, ev_w_in, e, ev_w_in.shape[2], BF16, row_ss=ss).reshape(B, T, -1)
            ya = _pool_mixer(z, ev_w_pool[e].astype(BF16), ev_pool_scale[e], mix_a)
            yb = _hgrn_mixer(z, lb_table, ev_hg_norm[e], l, mix_a, mix_b)
            xs, h = _matmul_residual_norm([ya.reshape(M, mix_a), yb.reshape(M, mix_b)], ev_wo, e, xs, xa_norm[l])
        else:
            o = l // 2
            qscale = jnp.concatenate([jnp.full((D,), LOG2E * FOX_HEAD ** -0.5, F32), jnp.ones((2 * D,), F32)])
            qkv = _matmul(h, od_w_in, o, 3 * D, BF16, colscale=qscale, row_ss=ss).reshape(B, T, 3 * D)
            wf = jnp.zeros((D, LANES), BF16).at[:, :fox_heads].set(od_w_in[o, :, 3 * D:].astype(BF16))
            bf = jnp.zeros((1, LANES), F32).at[0, :fox_heads].set(od_b_f[o].astype(F32))
            ss4 = None if ss is None else ss.reshape(ss.shape[0], B, T, 1)
            f = _fox_gates(h.reshape(B, T, D), wf, bf, ss4)
            frow = f[..., :fox_heads].transpose(0, 2, 1).reshape(B, fox_heads, T // fox_tq, 1, fox_tq)
            y = _fox_attention(qkv, f, frow, fox_heads, fox_tq).reshape(M, D)
            xs, h = _matmul_residual_norm([y], od_wo, o, xs, xa_norm[l])

        mn = _rmsnorm(mem2, xa_mem_norm[l], BF16)
        q = _matmul(h, xa_wq, l, D, BF16, colscale=jnp.full((D,), xa_scale, F32))
        kv = _matmul(mn, xa_wkv, l, 2 * D, BF16)
        a = _xattn_core(q.reshape(B, T, D), kv.reshape(B, n_mem, 2 * D)).reshape(M, D)
        xs, h = _matmul_residual_norm([a], xa_wo_b, l, xs, ffn_norm[l])

        act = _matmul_swiglu(h, ffn_w_gate, ffn_w_up, l)
        if l + 1 < depth:
            nxt = l + 1
            gain = ev_norm[nxt // 2] if nxt % 2 == 0 else od_norm[nxt // 2]
            xs, hp, ss = _matmul_residual(act, w_down, l, xs, gain=gain)
            pre = (hp, ss)
        else:
            xs = _matmul_residual(act, w_down, l, xs)
    return _rmsnorm(xs, final_norm, x.dtype).reshape(B, T, D)
```

```python
import functools

import jax
import jax.numpy as jnp
from jax import lax
from jax.experimental import pallas as pl
from jax.experimental.pallas import tpu as pltpu

F32 = jnp.float32
BF16 = jnp.bfloat16

EPS = 1e-6
N_MEM = 256
POOL_WINDOWS = (2, 4, 8, 16)
POOL_HALO = 16
HG_HEAD = 128
HG_CHUNK = 64
HG_DIAG = 8
HG_SUB = 256
FOX_HEAD = 128
FOX_GROUP = 4
XA_HEADS = 4

LOG2E = 1.4426950408889634
LANES = 128
VMEM_CAP = 56 * 1024 * 1024
VMEM_FLOOR = 32 * 1024 * 1024


def _params(semantics, vmem_estimate):
    limit = int(min(max(vmem_estimate * 5 // 4, VMEM_FLOOR), VMEM_CAP))
    return pltpu.CompilerParams(dimension_semantics=semantics, vmem_limit_bytes=limit)


def _nbytes(shape, dtype):
    n = jnp.dtype(dtype).itemsize
    for s in shape:
        n *= s
    return n


def _rmsnorm_body(x_ref, g_ref, o_ref):
    x = x_ref[...]
    ms = jnp.mean(x * x, axis=-1, keepdims=True)
    o_ref[...] = (x * lax.rsqrt(ms + EPS) * g_ref[...]).astype(o_ref.dtype)


def _rmsnorm(x, g, out_dtype, tm=512):
    M, D = x.shape
    tm = min(tm, M)
    est = 2 * (_nbytes((tm, D), x.dtype) + _nbytes((tm, D), out_dtype)) + 3 * _nbytes((tm, D), F32)
    return pl.pallas_call(
        _rmsnorm_body,
        grid=(M // tm,),
        in_specs=[pl.BlockSpec((tm, D), lambda i: (i, 0)),
                  pl.BlockSpec((1, D), lambda i: (0, 0))],
        out_specs=pl.BlockSpec((tm, D), lambda i: (i, 0)),
        out_shape=jax.ShapeDtypeStruct((M, D), out_dtype),
        compiler_params=_params(("parallel",), est),
        name="rmsnorm",
    )(x, g.reshape(1, D).astype(F32))


def _with_bf16_weights(w_refs, wb_refs, compute):
    if not wb_refs:
        compute(*(w[...] for w in w_refs))
        return
    i = pl.program_id(1)

    @pl.when(i == 0)
    def _():
        for w, wb in zip(w_refs, wb_refs):
            wb[...] = w[...].astype(BF16)

    @pl.when(i > 0)
    def _():
        compute(*(wb[...] for wb in wb_refs))


def _row_rsqrt(ss_ref, width):
    return lax.rsqrt(jnp.sum(ss_ref[...], axis=0) * (1.0 / width) + EPS)


def _mm_body(x_ref, w_ref, cs_ref, *rest, norm_width):
    if norm_width:
        ss_ref, o_ref, *wb = rest
    else:
        o_ref, *wb = rest

    def compute(w):
        acc = jnp.dot(x_ref[...], w, preferred_element_type=F32)
        if norm_width:
            acc = acc * _row_rsqrt(ss_ref, norm_width)
        o_ref[...] = (acc * cs_ref[...]).astype(o_ref.dtype)

    _with_bf16_weights((w_ref,), wb, compute)


def _mm_res_body(x_ref, w_ref, r_ref, *rest, with_gain):
    if with_gain:
        g_ref, o_ref, hp_ref, ss_ref, *wb = rest
    else:
        o_ref, *wb = rest

    def compute(w):
        acc = r_ref[...] + jnp.dot(x_ref[...], w, preferred_element_type=F32)
        o_ref[...] = acc
        if with_gain:
            hp_ref[...] = (acc * g_ref[...]).astype(hp_ref.dtype)
            ss_ref[...] = jnp.sum(acc * acc, axis=-1, keepdims=True)

    _with_bf16_weights((w_ref,), wb, compute)


def _mm_swiglu_body(x_ref, wg_ref, wu_ref, o_ref, *wb):
    def compute(wg, wu):
        x = x_ref[...]
        g = jnp.dot(x, wg, preferred_element_type=F32)
        u = jnp.dot(x, wu, preferred_element_type=F32)
        o_ref[...] = (g * jax.nn.sigmoid(g) * u).astype(o_ref.dtype)

    _with_bf16_weights((wg_ref, wu_ref), wb, compute)


def _pick_tile(n, pref):
    t = min(pref, n)
    while n % t:
        t //= 2
    return t


def _weight_spec(layer, K, tn):
    return pl.BlockSpec((None, K, tn), lambda j, i: (layer, 0, j))


def _weight_scratch(w, K, tn):
    return [] if w.dtype == BF16 else [pltpu.VMEM((K, tn), BF16)]


def _weight_bytes(w, K, tn):
    return 2 * _nbytes((K, tn), w.dtype) + (0 if w.dtype == BF16 else _nbytes((K, tn), BF16))


def _row_steps(w, M, tm):
    extra = 0 if w.dtype == BF16 else 1
    return M // tm + extra, lambda i: jnp.maximum(i - extra, 0)


def _matmul(x, w, layer, n_cols, out_dtype, colscale=None, row_ss=None, tm=1024, tn=1024):
    M, K = x.shape
    tm, tn = _pick_tile(M, tm), _pick_tile(n_cols, tn)
    if colscale is None:
        colscale = jnp.ones((n_cols,), F32)
    steps, row = _row_steps(w, M, tm)
    ss_specs = [] if row_ss is None else [pl.BlockSpec((row_ss.shape[0], tm, 1), lambda j, i: (0, row(i), 0))]
    ss_args = [] if row_ss is None else [row_ss]
    est = 2 * (_nbytes((tm, K), x.dtype) + _nbytes((tm, tn), out_dtype)) + _weight_bytes(w, K, tn) \
        + 2 * _nbytes((tm, tn), F32)
    return pl.pallas_call(
        functools.partial(_mm_body, norm_width=0 if row_ss is None else K),
        grid=(n_cols // tn, steps),
        in_specs=[pl.BlockSpec((tm, K), lambda j, i: (row(i), 0)),
                  _weight_spec(layer, K, tn),
                  pl.BlockSpec((1, tn), lambda j, i: (0, j))] + ss_specs,
        out_specs=pl.BlockSpec((tm, tn), lambda j, i: (row(i), j)),
        out_shape=jax.ShapeDtypeStruct((M, n_cols), out_dtype),
        scratch_shapes=_weight_scratch(w, K, tn),
        compiler_params=_params(("parallel", "arbitrary"), est),
        name="matmul",
    )(x, w, colscale.reshape(1, n_cols), *ss_args)


def _matmul_residual(x, w, layer, r, gain=None, tn=1024):
    M, K = x.shape
    N = w.shape[2]
    tm = 1024 if K <= 2048 else 512
    tm, tn = _pick_tile(M, tm), _pick_tile(N, tn)
    steps, row = _row_steps(w, M, tm)
    est = 2 * (_nbytes((tm, K), x.dtype) + 2 * _nbytes((tm, tn), F32)) + _weight_bytes(w, K, tn) \
        + 2 * _nbytes((tm, tn), F32)
    tile = pl.BlockSpec((tm, tn), lambda j, i: (row(i), j))
    in_specs = [pl.BlockSpec((tm, K), lambda j, i: (row(i), 0)), _weight_spec(layer, K, tn), tile]
    out_specs, out_shape, args = tile, jax.ShapeDtypeStruct((M, N), F32), [x, w, r]
    if gain is not None:
        in_specs.append(pl.BlockSpec((1, tn), lambda j, i: (0, j)))
        args.append(gain.reshape(1, N).astype(F32))
        out_specs = [tile, tile, pl.BlockSpec((None, tm, 1), lambda j, i: (j, row(i), 0))]
        out_shape = [out_shape, jax.ShapeDtypeStruct((M, N), BF16), jax.ShapeDtypeStruct((N // tn, M, 1), F32)]
    return pl.pallas_call(
        functools.partial(_mm_res_body, with_gain=gain is not None),
        grid=(N // tn, steps),
        in_specs=in_specs,
        out_specs=out_specs,
        out_shape=out_shape,
        scratch_shapes=_weight_scratch(w, K, tn),
        compiler_params=_params(("parallel", "arbitrary"), est),
        name="matmul_residual",
    )(*args)


def _matmul_swiglu(x, wg, wu, layer, tm=2048, tn=512):
    M, K = x.shape
    N = wg.shape[2]
    tm, tn = _pick_tile(M, tm), _pick_tile(N, tn)
    steps, row = _row_steps(wg, M, tm)
    est = 2 * (_nbytes((tm, K), x.dtype) + _nbytes((tm, tn), BF16)) + 2 * _weight_bytes(wg, K, tn) \
        + 4 * _nbytes((tm, tn), F32)
    return pl.pallas_call(
        _mm_swiglu_body,
        grid=(N // tn, steps),
        in_specs=[pl.BlockSpec((tm, K), lambda j, i: (row(i), 0)),
                  _weight_spec(layer, K, tn),
                  _weight_spec(layer, K, tn)],
        out_specs=pl.BlockSpec((tm, tn), lambda j, i: (row(i), j)),
        out_shape=jax.ShapeDtypeStruct((M, N), BF16),
        scratch_shapes=_weight_scratch(wg, K, tn) + _weight_scratch(wu, K, tn),
        compiler_params=_params(("parallel", "arbitrary"), est),
        name="matmul_swiglu",
    )(x, wg, wu)


def _mm_res_norm_body(*refs, n_parts):
    x_refs = refs[:n_parts]
    w_ref, r_ref, g_ref, o_ref, h_ref = refs[n_parts:]
    acc = r_ref[...]
    k0 = 0
    for x_ref in x_refs:
        kw = x_ref.shape[1]
        acc = acc + jnp.dot(x_ref[...], w_ref[k0:k0 + kw, :], preferred_element_type=F32)
        k0 += kw
    o_ref[...] = acc
    ms = jnp.mean(acc * acc, axis=-1, keepdims=True)
    h_ref[...] = (acc * lax.rsqrt(ms + EPS) * g_ref[...]).astype(h_ref.dtype)


def _matmul_residual_norm(x_parts, w, layer, r, gain, tm=512):
    M, N = r.shape
    K = w.shape[1]
    tm = _pick_tile(M, tm)
    est = 2 * (_nbytes((tm, K), BF16) + _nbytes((K, N), w.dtype) + 2 * _nbytes((tm, N), F32) + _nbytes((tm, N), BF16)) \
        + 3 * _nbytes((tm, N), F32)
    return pl.pallas_call(
        functools.partial(_mm_res_norm_body, n_parts=len(x_parts)),
        grid=(M // tm,),
        in_specs=[pl.BlockSpec((tm, xp.shape[1]), lambda i: (i, 0)) for xp in x_parts]
        + [pl.BlockSpec((None, K, N), lambda i: (layer, 0, 0)),
           pl.BlockSpec((tm, N), lambda i: (i, 0)),
           pl.BlockSpec((1, N), lambda i: (0, 0))],
        out_specs=[pl.BlockSpec((tm, N), lambda i: (i, 0)),
                   pl.BlockSpec((tm, N), lambda i: (i, 0))],
        out_shape=[jax.ShapeDtypeStruct((M, N), F32),
                   jax.ShapeDtypeStruct((M, N), BF16)],
        compiler_params=_params(("parallel",), est),
        name="matmul_residual_norm",
    )(*x_parts, w, r, gain.reshape(1, N).astype(F32))


def _pool_body(u_ref, halo_ref, w_ref, sc_ref, o_ref, ext_ref, *, tt, group):
    t = pl.program_id(1)
    u = u_ref[0].astype(F32)
    ext_ref[0:POOL_HALO, :] = jnp.where(t > 0, halo_ref[0].astype(F32), 0.0)
    ext_ref[POOL_HALO:POOL_HALO + tt, :] = u
    pos = t * tt + lax.broadcasted_iota(jnp.int32, (tt, 1), 0)
    for gi, win in enumerate(POOL_WINDOWS):
        c0, c1 = gi * group, (gi + 1) * group
        tok = u[:, c0:c1]
        acc = tok
        for d in range(1, win):
            acc = acc + ext_ref[POOL_HALO - d:POOL_HALO - d + tt, c0:c1]
        cnt = jnp.minimum(pos + 1, win).astype(F32)
        p = acc / cnt - tok
        y = jnp.dot(p.astype(BF16), w_ref[gi], preferred_element_type=F32)
        o_ref[0, :, c0:c1] = (y * sc_ref[:, c0:c1]).astype(o_ref.dtype)


def _pool_mixer(z, w_pool, scale, mix_a, tt=512):
    B, T, _ = z.shape
    tt = min(tt, T)
    group = mix_a // len(POOL_WINDOWS)
    halo_blocks = tt // POOL_HALO
    est = 2 * (_nbytes((tt, mix_a), F32) + _nbytes((tt, mix_a), BF16)) + 4 * _nbytes((tt, mix_a), F32)
    return pl.pallas_call(
        functools.partial(_pool_body, tt=tt, group=group),
        grid=(B, T // tt),
        in_specs=[pl.BlockSpec((1, tt, mix_a), lambda b, t: (b, t, 0)),
                  pl.BlockSpec((1, POOL_HALO, mix_a),
                               lambda b, t: (b, jnp.maximum(t * halo_blocks - 1, 0), 0)),
                  pl.BlockSpec((len(POOL_WINDOWS), group, group), lambda b, t: (0, 0, 0)),
                  pl.BlockSpec((1, mix_a), lambda b, t: (0, 0))],
        out_specs=pl.BlockSpec((1, tt, mix_a), lambda b, t: (b, t, 0)),
        out_shape=jax.ShapeDtypeStruct((B, T, mix_a), BF16),
        scratch_shapes=[pltpu.VMEM((tt + POOL_HALO, mix_a), F32)],
        compiler_params=_params(("parallel", "parallel"), est),
        name="pool_mixer",
    )(z, z, w_pool, scale.reshape(1, mix_a).astype(F32))


def _hgrn_subtile(qf, kk, v, logf, st):
    n = qf.shape[0]
    C, R = HG_CHUNK, HG_DIAG
    row = lax.broadcasted_iota(jnp.int32, (n, HG_HEAD), 0)

    pos = row & (C - 1)
    b = logf
    sh = 1
    while sh < C:
        b = b + jnp.where(pos >= sh, pltpu.roll(b, sh, axis=0), 0.0)
        sh *= 2

    nb = n // R
    b3, q3, k3, v3 = (a.reshape(nb, R, HG_HEAD) for a in (b, qf, kk, v))
    tpos = lax.broadcasted_iota(jnp.int32, (1, R, 1), 1)
    od = jnp.zeros((nb, R, HG_HEAD), F32)
    for s in range(R):
        w = q3 * jnp.exp(b3 - b3[:, s:s + 1, :]) * k3[:, s:s + 1, :]
        a = jnp.sum(w, axis=-1, keepdims=True)
        a = jnp.where(tpos >= s, a, 0.0)
        od = od + a * v3[:, s:s + 1, :]
    o = od.reshape(n, HG_HEAD)

    same = lax.broadcasted_iota(jnp.int32, (n, n), 0) ^ lax.broadcasted_iota(jnp.int32, (n, n), 1)
    a_off = jnp.zeros((n, n), F32)
    h = R
    while h < C:
        g = 2 * h
        ref = b.reshape(n // g, g, HG_HEAD)[:, h - 1:h, :]
        ref = jnp.broadcast_to(ref, (n // g, g, HG_HEAD)).reshape(n, HG_HEAD)
        e = jnp.exp(-jnp.abs(b - ref))
        right = (row & (g - 1)) >= h
        qt = jnp.where(right, qf * e, 0.0).astype(BF16)
        kt = jnp.where(right, 0.0, kk * e).astype(BF16)
        a = lax.dot_general(qt, kt, (((1,), (1,)), ((), ())), preferred_element_type=F32)
        a_off = a_off + jnp.where(same < g, a, 0.0)
        h = g
    v_bf = v.astype(BF16)
    o = o + jnp.dot(a_off.astype(BF16), v_bf, preferred_element_type=F32)

    nc = n // C
    bc = b.reshape(nc, C, HG_HEAD)
    b_last = bc[:, C - 1:C, :]
    q_in = (qf * jnp.exp(b)).astype(BF16)
    k_out = (kk.reshape(nc, C, HG_HEAD) * jnp.exp(b_last - bc)).astype(BF16)
    dec = jnp.exp(b_last)
    pieces = []
    for c in range(nc):
        rows = slice(c * C, (c + 1) * C)
        o_int = lax.dot_general(q_in[rows], st.astype(BF16), (((1,), (1,)), ((), ())),
                                preferred_element_type=F32)
        pieces.append(o[rows] + o_int)
        upd = lax.dot_general(v_bf[rows], k_out[c], (((0,), (0,)), ((), ())),
                              preferred_element_type=F32)
        st = st * dec[c] + upd
    return jnp.concatenate(pieces, axis=0), st


def _hgrn_body(lbt_ref, ng_ref, q_ref, f_ref, i_ref, g_ref, o_ref, st_ref, *, layer, tt, sub):
    t = pl.program_id(2)

    @pl.when(t == 0)
    def _():
        st_ref[...] = jnp.zeros_like(st_ref)

    lbt = lbt_ref[...]
    e = jnp.exp(lbt - jnp.max(lbt, axis=0, keepdims=True))
    sm = e / jnp.sum(e, axis=0, keepdims=True)
    lb = jnp.sum(sm[1:layer + 2], axis=0, keepdims=True)

    st = st_ref[...]
    for c in range(tt // sub):
        rows = slice(c * sub, (c + 1) * sub)
        f = lb + (1.0 - lb) * jax.nn.sigmoid(f_ref[0, rows, :].astype(F32))
        q = q_ref[0, rows, :].astype(F32)
        qf = q * jax.nn.sigmoid(q) * (HG_HEAD ** -0.5)
        o, st = _hgrn_subtile(qf, 1.0 - f, i_ref[0, rows, :].astype(F32), jnp.log(f), st)
        ms = jnp.mean(o * o, axis=-1, keepdims=True)
        o = o * lax.rsqrt(ms + EPS) * ng_ref[...]
        g = g_ref[0, rows, :].astype(F32)
        o_ref[0, rows, :] = (o * (g * jax.nn.sigmoid(g))).astype(o_ref.dtype)
    st_ref[...] = st


def _hgrn_mixer(z, lb_table, norm_g, layer, mix_a, mix_b, tt=512):
    B, T, _ = z.shape
    tt = min(tt, T)
    sub = min(HG_SUB, tt)
    heads = mix_b // HG_HEAD
    c0 = mix_a // HG_HEAD
    est = 2 * 5 * _nbytes((tt, HG_HEAD), F32) + 32 * _nbytes((sub, HG_HEAD), F32) + 6 * _nbytes((sub, sub), F32)

    def col(off):
        return pl.BlockSpec((1, tt, HG_HEAD), lambda b, h, t, off=off: (b, t, c0 + off * heads + h))

    return pl.pallas_call(
        functools.partial(_hgrn_body, layer=layer, tt=tt, sub=sub),
        grid=(B, heads, T // tt),
        in_specs=[pl.BlockSpec((lb_table.shape[0], HG_HEAD), lambda b, h, t: (0, h)),
                  pl.BlockSpec((1, HG_HEAD), lambda b, h, t: (0, 0)),
                  col(0), col(1), col(2), col(3)],
        out_specs=pl.BlockSpec((1, tt, HG_HEAD), lambda b, h, t: (b, t, h)),
        out_shape=jax.ShapeDtypeStruct((B, T, mix_b), BF16),
        scratch_shapes=[pltpu.VMEM((HG_HEAD, HG_HEAD), F32)],
        compiler_params=_params(("parallel", "parallel", "arbitrary"), est),
        name="hgrn2_mixer",
    )(lb_table.astype(F32), norm_g.reshape(1, HG_HEAD).astype(F32), z, z, z, z)


def _fgate_body(h_ref, w_ref, b_ref, *rest, tt, norm_width):
    if norm_width:
        ss_ref, o_ref, carry_ref = rest
    else:
        o_ref, carry_ref = rest
    t = pl.program_id(1)

    @pl.when(t == 0)
    def _():
        carry_ref[...] = jnp.zeros_like(carry_ref)

    fl = jnp.dot(h_ref[0], w_ref[...], preferred_element_type=F32)
    if norm_width:
        fl = fl * _row_rsqrt(ss_ref, norm_width)
    fl = fl + b_ref[...]
    c = (jnp.minimum(fl, 0.0) - jnp.log1p(jnp.exp(-jnp.abs(fl)))) * LOG2E
    row = lax.broadcasted_iota(jnp.int32, c.shape, 0)
    sh = 1
    while sh < tt:
        c = c + jnp.where(row >= sh, pltpu.roll(c, sh, axis=0), 0.0)
        sh *= 2
    c = c + carry_ref[...]
    o_ref[0] = c
    carry_ref[...] = c[tt - 1:tt, :]


def _fox_gates(h, wf, bf, row_ss=None, tt=512):
    B, T, D = h.shape
    tt = min(tt, T)
    ss_specs = [] if row_ss is None else [pl.BlockSpec((row_ss.shape[0], None, tt, 1), lambda b, t: (0, b, t, 0))]
    ss_args = [] if row_ss is None else [row_ss]
    est = 2 * (_nbytes((tt, D), BF16) + _nbytes((D, LANES), BF16) + _nbytes((tt, LANES), F32)) \
        + 8 * _nbytes((tt, LANES), F32)
    return pl.pallas_call(
        functools.partial(_fgate_body, tt=tt, norm_width=0 if row_ss is None else D),
        grid=(B, T // tt),
        in_specs=[pl.BlockSpec((1, tt, D), lambda b, t: (b, t, 0)),
                  pl.BlockSpec((D, LANES), lambda b, t: (0, 0)),
                  pl.BlockSpec((1, LANES), lambda b, t: (0, 0))] + ss_specs,
        out_specs=pl.BlockSpec((1, tt, LANES), lambda b, t: (b, t, 0)),
        out_shape=jax.ShapeDtypeStruct((B, T, LANES), F32),
        scratch_shapes=[pltpu.VMEM((1, LANES), F32)],
        compiler_params=_params(("parallel", "arbitrary"), est),
        name="fox_gates",
    )(h, wf, bf, *ss_args)


def _fox_body(q_ref, k_ref, v_ref, f_ref, fr_ref, o_ref, vt_ref, fk_ref, p_ref, acc_ref, *, tq, seq, group):
    hg = pl.program_id(1)
    i = pl.program_id(2)
    reps = tq // LANES
    heads = range(group)

    def head_cols(g):
        return slice(g * FOX_HEAD, (g + 1) * FOX_HEAD)

    @pl.when(i == 0)
    def _():
        for c in range(seq // tq):
            rows = slice(c * tq, (c + 1) * tq)
            fblk = f_ref[0, rows, :]
            lane = lax.broadcasted_iota(jnp.int32, fblk.shape, 1)
            for g in heads:
                vt_ref[g, :, rows] = v_ref[0, rows, head_cols(g)].T
                col = jnp.sum(jnp.where(lane == hg * group + g, fblk, 0.0), axis=-1, keepdims=True)
                fk_ref[g, rows, :] = jnp.broadcast_to(col, fblk.shape)

    qt = [q_ref[0, :, head_cols(g)].T for g in heads]
    fq = [fr_ref[0, g, i] for g in heads]

    def scores(g, j):
        start = pl.multiple_of(j * tq, tq)
        return jnp.dot(k_ref[0, pl.ds(start, tq), head_cols(g)], qt[g], preferred_element_type=F32)

    def softmax_block(g, j, st, m_prev, l_prev, diagonal):
        start = pl.multiple_of(j * tq, tq)
        t = st - jnp.concatenate([fk_ref[g, pl.ds(start, tq), :]] * reps, axis=1)
        if diagonal:
            key = lax.broadcasted_iota(jnp.int32, (tq, tq), 0)
            qry = lax.broadcasted_iota(jnp.int32, (tq, tq), 1)
            t = jnp.where(key <= qry, t, -jnp.inf)
        m_new = jnp.maximum(m_prev, jnp.max(t, axis=0, keepdims=True) + fq[g])
        alpha = jnp.exp2(m_prev - m_new)
        p = jnp.exp2(t - (m_new - fq[g]))
        l_new = alpha * l_prev + jnp.sum(p, axis=0, keepdims=True)
        return p.astype(BF16), m_new, l_new, alpha

    def weighted_values(g, j, p):
        start = pl.multiple_of(j * tq, tq)
        return jnp.dot(vt_ref[g, :, pl.ds(start, tq)], p, preferred_element_type=F32)

    def block(j, j_prev, stats, first):
        out = []
        for g in heads:
            m, l = stats[g]
            st = scores(g, j)
            if first:
                p, m, l, _ = softmax_block(g, j, st, m, l, True)
                acc_ref[g] = jnp.zeros((FOX_HEAD, tq), F32)
            else:
                pv_prev = weighted_values(g, j_prev, p_ref[g])
                p, m, l, alpha = softmax_block(g, j, st, m, l, False)
                acc_ref[g] = (acc_ref[g] + pv_prev) * alpha
            p_ref[g] = p
            out.append((m, l))
        return tuple(out)

    stats = tuple((jnp.full((1, tq), -jnp.inf, F32), jnp.zeros((1, tq), F32)) for _ in heads)
    stats = block(i, i, stats, True)
    stats = lax.fori_loop(0, i, lambda j, c: block(j, jnp.where(j == 0, i, j - 1), c, False), stats)
    j_last = jnp.where(i == 0, i, i - 1)
    for g in heads:
        acc = acc_ref[g] + weighted_values(g, j_last, p_ref[g])
        o_ref[0, :, head_cols(g)] = (acc / stats[g][1]).astype(o_ref.dtype).T


def _fox_attention(qkv, f, frow, heads, tq, group=FOX_GROUP):
    B, T, _ = qkv.shape
    nq = T // tq
    gw = group * FOX_HEAD
    ngrp = heads // group
    est = 2 * (2 * _nbytes((T, gw), BF16) + 2 * _nbytes((tq, gw), BF16)
               + _nbytes((T, LANES), F32) + group * _nbytes((nq, 8, tq), F32)) \
        + group * (_nbytes((T, FOX_HEAD), BF16) + _nbytes((T, LANES), F32) + 5 * _nbytes((tq, tq), F32))
    return pl.pallas_call(
        functools.partial(_fox_body, tq=tq, seq=T, group=group),
        grid=(B, ngrp, nq),
        in_specs=[pl.BlockSpec((1, tq, gw), lambda b, h, i: (b, i, h)),
                  pl.BlockSpec((1, T, gw), lambda b, h, i: (b, 0, ngrp + h)),
                  pl.BlockSpec((1, T, gw), lambda b, h, i: (b, 0, 2 * ngrp + h)),
                  pl.BlockSpec((1, T, LANES), lambda b, h, i: (b, 0, 0)),
                  pl.BlockSpec((1, group, nq, 1, tq), lambda b, h, i: (b, h, 0, 0, 0))],
        out_specs=pl.BlockSpec((1, tq, gw), lambda b, h, i: (b, i, h)),
        out_shape=jax.ShapeDtypeStruct((B, T, heads * FOX_HEAD), BF16),
        scratch_shapes=[pltpu.VMEM((group, FOX_HEAD, T), BF16),
                        pltpu.VMEM((group, T, LANES), F32),
                        pltpu.VMEM((group, tq, tq), BF16),
                        pltpu.VMEM((group, FOX_HEAD, tq), F32)],
        compiler_params=_params(("parallel", "parallel", "arbitrary"), est),
        name="fox_attention",
    )(qkv, qkv, qkv, f, frow)


def _xattn_out_body(q_ref, kv_ref, w_ref, r_ref, g_ref, o_ref, h_ref, a_ref, *, d_model):
    hd = d_model // XA_HEADS
    for hh in range(XA_HEADS):
        q = q_ref[:, hh * hd:(hh + 1) * hd]
        k = kv_ref[:, hh * hd:(hh + 1) * hd]
        v = kv_ref[:, d_model + hh * hd:d_model + (hh + 1) * hd]
        s = lax.dot_general(q, k, (((1,), (1,)), ((), ())), preferred_element_type=F32)
        p = jnp.exp(s - jnp.max(s, axis=-1, keepdims=True))
        p = p / jnp.sum(p, axis=-1, keepdims=True)
        a_ref[:, hh * hd:(hh + 1) * hd] = jnp.dot(p.astype(BF16), v, preferred_element_type=F32).astype(a_ref.dtype)
    acc = r_ref[...] + jnp.dot(a_ref[...], w_ref[...], preferred_element_type=F32)
    o_ref[...] = acc
    ms = jnp.mean(acc * acc, axis=-1, keepdims=True)
    h_ref[...] = (acc * lax.rsqrt(ms + EPS) * g_ref[...]).astype(h_ref.dtype)


def _xattn_out_norm(q, kv, w, layer, r, gain, seq, tm=512):
    M, D = q.shape
    n_mem = kv.shape[1]
    tm = _pick_tile(seq, tm)
    per_batch = seq // tm
    est = 2 * (_nbytes((tm, D), BF16) + _nbytes((n_mem, 2 * D), BF16) + _nbytes((D, D), BF16)
               + 2 * _nbytes((tm, D), F32) + _nbytes((tm, D), BF16)) \
        + _nbytes((tm, D), BF16) + 3 * _nbytes((tm, D), F32)
    row = pl.BlockSpec((tm, D), lambda i: (i, 0))
    return pl.pallas_call(
        functools.partial(_xattn_out_body, d_model=D),
        grid=(M // tm,),
        in_specs=[row,
                  pl.BlockSpec((None, n_mem, 2 * D), lambda i: (i // per_batch, 0, 0)),
                  pl.BlockSpec((None, D, D), lambda i: (layer, 0, 0)),
                  row,
                  pl.BlockSpec((1, D), lambda i: (0, 0))],
        out_specs=[row, row],
        out_shape=[jax.ShapeDtypeStruct((M, D), F32),
                   jax.ShapeDtypeStruct((M, D), BF16)],
        scratch_shapes=[pltpu.VMEM((tm, D), BF16)],
        compiler_params=_params(("parallel",), est),
        name="xattn_out_norm",
    )(q, kv, w, r, gain.reshape(1, D).astype(F32))


def kernel(x, mem, lb_table, ev_norm, ev_w_in, ev_w_pool, ev_pool_scale, ev_hg_norm, ev_w_out,
           od_norm, od_w_in, od_b_f, od_w_out, xa_norm, xa_mem_norm, xa_wq, xa_wkv, xa_wo,
           ffn_norm, ffn_w_gate, ffn_w_up, ffn_w_down, final_norm):
    B, T, D = x.shape
    M = B * T
    depth = xa_norm.shape[0]
    mix_a = ev_pool_scale.shape[1]
    mix_b = lb_table.shape[1]
    fox_heads = od_b_f.shape[1]
    n_mem = mem.shape[1]
    xa_scale = (D // XA_HEADS) ** -0.5
    fox_tq = _pick_tile(T, 512)

    xs = x.reshape(M, D)
    mem2 = mem.reshape(B * n_mem, D)
    w_down = ffn_w_down.astype(BF16)
    ev_wo, od_wo, xa_wo_b = ev_w_out.astype(BF16), od_w_out.astype(BF16), xa_wo.astype(BF16)
    pre = None
    for l in range(depth):
        if pre is None:
            h, ss = _rmsnorm(xs, ev_norm[l // 2] if l % 2 == 0 else od_norm[l // 2], BF16), None
        else:
            h, ss = pre
        if l % 2 == 0:
            e = l // 2
            z = _matmul(h, ev_w_in, e, ev_w_in.shape[2], BF16, row_ss=ss).reshape(B, T, -1)
            ya = _pool_mixer(z, ev_w_pool[e].astype(BF16), ev_pool_scale[e], mix_a)
            yb = _hgrn_mixer(z, lb_table, ev_hg_norm[e], l, mix_a, mix_b)
            xs, h = _matmul_residual_norm([ya.reshape(M, mix_a), yb.reshape(M, mix_b)], ev_wo, e, xs, xa_norm[l])
        else:
            o = l // 2
            qscale = jnp.concatenate([jnp.full((D,), LOG2E * FOX_HEAD ** -0.5, F32), jnp.ones((2 * D,), F32)])
            w_in = od_w_in.astype(BF16)
            qkv = _matmul(h, w_in, o, 3 * D, BF16, colscale=qscale, row_ss=ss).reshape(B, T, 3 * D)
            wf = jnp.zeros((D, LANES), BF16).at[:, :fox_heads].set(w_in[o, :, 3 * D:])
            bf = jnp.zeros((1, LANES), F32).at[0, :fox_heads].set(od_b_f[o].astype(F32))
            ss4 = None if ss is None else ss.reshape(ss.shape[0], B, T, 1)
            f = _fox_gates(h.reshape(B, T, D), wf, bf, ss4)
            frow = f[..., :fox_heads].transpose(0, 2, 1).reshape(B, fox_heads, T // fox_tq, 1, fox_tq)
            y = _fox_attention(qkv, f, frow, fox_heads, fox_tq).reshape(M, D)
            xs, h = _matmul_residual_norm([y], od_wo, o, xs, xa_norm[l])

        mn = _rmsnorm(mem2, xa_mem_norm[l], BF16)
        q = _matmul(h, xa_wq, l, D, BF16, colscale=jnp.full((D,), xa_scale, F32))
        kv = _matmul(mn, xa_wkv, l, 2 * D, BF16)
        xs, h = _xattn_out_norm(q, kv.reshape(B, n_mem, 2 * D), xa_wo_b, l, xs, ffn_norm[l], T)

        act = _matmul_swiglu(h, ffn_w_gate, ffn_w_up, l)
        if l + 1 < depth:
            nxt = l + 1
            gain = ev_norm[nxt // 2] if nxt % 2 == 0 else od_norm[nxt // 2]
            xs, hp, ss = _matmul_residual(act, w_down, l, xs, gain=gain)
            pre = (hp, ss)
        else:
            xs = _matmul_residual(act, w_down, l, xs)
    return _rmsnorm(xs, final_norm, x.dtype).reshape(B, T, D)
```

```python
import functools

import jax
import jax.numpy as jnp
from jax import lax
from jax.experimental import pallas as pl
from jax.experimental.pallas import tpu as pltpu

F32 = jnp.float32
BF16 = jnp.bfloat16

EPS = 1e-6
N_MEM = 256
POOL_WINDOWS = (2, 4, 8, 16)
POOL_HALO = 16
HG_HEAD = 128
HG_CHUNK = 64
HG_DIAG = 8
HG_SUB = 256
FOX_HEAD = 128
FOX_GROUP = 4
XA_HEADS = 4

LOG2E = 1.4426950408889634
LANES = 128
VMEM_CAP = 56 * 1024 * 1024
VMEM_FLOOR = 32 * 1024 * 1024


def _params(semantics, vmem_estimate):
    limit = int(min(max(vmem_estimate * 5 // 4, VMEM_FLOOR), VMEM_CAP))
    return pltpu.CompilerParams(dimension_semantics=semantics, vmem_limit_bytes=limit)


def _nbytes(shape, dtype):
    n = jnp.dtype(dtype).itemsize
    for s in shape:
        n *= s
    return n


def _rmsnorm_body(x_ref, g_ref, o_ref):
    x = x_ref[...]
    ms = jnp.mean(x * x, axis=-1, keepdims=True)
    o_ref[...] = (x * lax.rsqrt(ms + EPS) * g_ref[...]).astype(o_ref.dtype)


def _rmsnorm(x, g, out_dtype, tm=512):
    M, D = x.shape
    tm = min(tm, M)
    est = 2 * (_nbytes((tm, D), x.dtype) + _nbytes((tm, D), out_dtype)) + 3 * _nbytes((tm, D), F32)
    return pl.pallas_call(
        _rmsnorm_body,
        grid=(M // tm,),
        in_specs=[pl.BlockSpec((tm, D), lambda i: (i, 0)),
                  pl.BlockSpec((1, D), lambda i: (0, 0))],
        out_specs=pl.BlockSpec((tm, D), lambda i: (i, 0)),
        out_shape=jax.ShapeDtypeStruct((M, D), out_dtype),
        compiler_params=_params(("parallel",), est),
        name="rmsnorm",
    )(x, g.reshape(1, D).astype(F32))


def _with_bf16_weights(w_refs, wb_refs, compute):
    if not wb_refs:
        compute(*(w[...] for w in w_refs))
        return
    i = pl.program_id(1)

    @pl.when(i == 0)
    def _():
        for w, wb in zip(w_refs, wb_refs):
            wb[...] = w[...].astype(BF16)

    @pl.when(i > 0)
    def _():
        compute(*(wb[...] for wb in wb_refs))


def _row_rsqrt(ss_ref, width):
    return lax.rsqrt(jnp.sum(ss_ref[...], axis=0) * (1.0 / width) + EPS)


def _mm_body(x_ref, w_ref, cs_ref, *rest, norm_width):
    if norm_width:
        ss_ref, o_ref, *wb = rest
    else:
        o_ref, *wb = rest

    def compute(w):
        acc = jnp.dot(x_ref[...], w, preferred_element_type=F32)
        if norm_width:
            acc = acc * _row_rsqrt(ss_ref, norm_width)
        o_ref[...] = (acc * cs_ref[...]).astype(o_ref.dtype)

    _with_bf16_weights((w_ref,), wb, compute)


def _mm_res_body(x_ref, w_ref, r_ref, *rest, with_gain):
    if with_gain:
        g_ref, o_ref, hp_ref, ss_ref, *wb = rest
    else:
        o_ref, *wb = rest

    def compute(w):
        acc = r_ref[...] + jnp.dot(x_ref[...], w, preferred_element_type=F32)
        o_ref[...] = acc
        if with_gain:
            hp_ref[...] = (acc * g_ref[...]).astype(hp_ref.dtype)
            ss_ref[...] = jnp.sum(acc * acc, axis=-1, keepdims=True)

    _with_bf16_weights((w_ref,), wb, compute)


def _mm_swiglu_body(x_ref, wg_ref, wu_ref, o_ref, *wb):
    def compute(wg, wu):
        x = x_ref[...]
        g = jnp.dot(x, wg, preferred_element_type=F32)
        u = jnp.dot(x, wu, preferred_element_type=F32)
        o_ref[...] = (g * jax.nn.sigmoid(g) * u).astype(o_ref.dtype)

    _with_bf16_weights((wg_ref, wu_ref), wb, compute)


def _pick_tile(n, pref):
    t = min(pref, n)
    while n % t:
        t //= 2
    return t


def _weight_spec(layer, K, tn):
    return pl.BlockSpec((None, K, tn), lambda j, i: (layer, 0, j))


def _weight_scratch(w, K, tn):
    return [] if w.dtype == BF16 else [pltpu.VMEM((K, tn), BF16)]


def _weight_bytes(w, K, tn):
    return 2 * _nbytes((K, tn), w.dtype) + (0 if w.dtype == BF16 else _nbytes((K, tn), BF16))


def _row_steps(w, M, tm):
    extra = 0 if w.dtype == BF16 else 1
    return M // tm + extra, lambda i: jnp.maximum(i - extra, 0)


def _matmul(x, w, layer, n_cols, out_dtype, colscale=None, row_ss=None, tm=1024, tn=1024):
    M, K = x.shape
    tm, tn = _pick_tile(M, tm), _pick_tile(n_cols, tn)
    if colscale is None:
        colscale = jnp.ones((n_cols,), F32)
    steps, row = _row_steps(w, M, tm)
    ss_specs = [] if row_ss is None else [pl.BlockSpec((row_ss.shape[0], tm, 1), lambda j, i: (0, row(i), 0))]
    ss_args = [] if row_ss is None else [row_ss]
    est = 2 * (_nbytes((tm, K), x.dtype) + _nbytes((tm, tn), out_dtype)) + _weight_bytes(w, K, tn) \
        + 2 * _nbytes((tm, tn), F32)
    return pl.pallas_call(
        functools.partial(_mm_body, norm_width=0 if row_ss is None else K),
        grid=(n_cols // tn, steps),
        in_specs=[pl.BlockSpec((tm, K), lambda j, i: (row(i), 0)),
                  _weight_spec(layer, K, tn),
                  pl.BlockSpec((1, tn), lambda j, i: (0, j))] + ss_specs,
        out_specs=pl.BlockSpec((tm, tn), lambda j, i: (row(i), j)),
        out_shape=jax.ShapeDtypeStruct((M, n_cols), out_dtype),
        scratch_shapes=_weight_scratch(w, K, tn),
        compiler_params=_params(("parallel", "arbitrary"), est),
        name="matmul",
    )(x, w, colscale.reshape(1, n_cols), *ss_args)


def _matmul_residual(x, w, layer, r, gain=None, tn=1024):
    M, K = x.shape
    N = w.shape[2]
    tm = 1024 if K <= 2048 else 512
    tm, tn = _pick_tile(M, tm), _pick_tile(N, tn)
    steps, row = _row_steps(w, M, tm)
    est = 2 * (_nbytes((tm, K), x.dtype) + 2 * _nbytes((tm, tn), F32)) + _weight_bytes(w, K, tn) \
        + 2 * _nbytes((tm, tn), F32)
    tile = pl.BlockSpec((tm, tn), lambda j, i: (row(i), j))
    in_specs = [pl.BlockSpec((tm, K), lambda j, i: (row(i), 0)), _weight_spec(layer, K, tn), tile]
    out_specs, out_shape, args = tile, jax.ShapeDtypeStruct((M, N), F32), [x, w, r]
    if gain is not None:
        in_specs.append(pl.BlockSpec((1, tn), lambda j, i: (0, j)))
        args.append(gain.reshape(1, N).astype(F32))
        out_specs = [tile, tile, pl.BlockSpec((None, tm, 1), lambda j, i: (j, row(i), 0))]
        out_shape = [out_shape, jax.ShapeDtypeStruct((M, N), BF16), jax.ShapeDtypeStruct((N // tn, M, 1), F32)]
    return pl.pallas_call(
        functools.partial(_mm_res_body, with_gain=gain is not None),
        grid=(N // tn, steps),
        in_specs=in_specs,
        out_specs=out_specs,
        out_shape=out_shape,
        scratch_shapes=_weight_scratch(w, K, tn),
        compiler_params=_params(("parallel", "arbitrary"), est),
        name="matmul_residual",
    )(*args)


def _matmul_swiglu(x, wg, wu, layer, tm=1024, tn=512):
    M, K = x.shape
    N = wg.shape[2]
    tm, tn = _pick_tile(M, tm), _pick_tile(N, tn)
    steps, row = _row_steps(wg, M, tm)
    est = 2 * (_nbytes((tm, K), x.dtype) + _nbytes((tm, tn), BF16)) + 2 * _weight_bytes(wg, K, tn) \
        + 4 * _nbytes((tm, tn), F32)
    return pl.pallas_call(
        _mm_swiglu_body,
        grid=(N // tn, steps),
        in_specs=[pl.BlockSpec((tm, K), lambda j, i: (row(i), 0)),
                  _weight_spec(layer, K, tn),
                  _weight_spec(layer, K, tn)],
        out_specs=pl.BlockSpec((tm, tn), lambda j, i: (row(i), j)),
        out_shape=jax.ShapeDtypeStruct((M, N), BF16),
        scratch_shapes=_weight_scratch(wg, K, tn) + _weight_scratch(wu, K, tn),
        compiler_params=_params(("parallel", "arbitrary"), est),
        name="matmul_swiglu",
    )(x, wg, wu)


def _mm_res_norm_body(*refs, n_parts):
    x_refs = refs[:n_parts]
    w_ref, r_ref, g_ref, o_ref, h_ref = refs[n_parts:]
    acc = r_ref[...]
    k0 = 0
    for x_ref in x_refs:
        kw = x_ref.shape[1]
        acc = acc + jnp.dot(x_ref[...], w_ref[k0:k0 + kw, :], preferred_element_type=F32)
        k0 += kw
    o_ref[...] = acc
    ms = jnp.mean(acc * acc, axis=-1, keepdims=True)
    h_ref[...] = (acc * lax.rsqrt(ms + EPS) * g_ref[...]).astype(h_ref.dtype)


def _matmul_residual_norm(x_parts, w, layer, r, gain, tm=512):
    M, N = r.shape
    K = w.shape[1]
    tm = _pick_tile(M, tm)
    est = 2 * (_nbytes((tm, K), BF16) + _nbytes((K, N), w.dtype) + 2 * _nbytes((tm, N), F32) + _nbytes((tm, N), BF16)) \
        + 3 * _nbytes((tm, N), F32)
    return pl.pallas_call(
        functools.partial(_mm_res_norm_body, n_parts=len(x_parts)),
        grid=(M // tm,),
        in_specs=[pl.BlockSpec((tm, xp.shape[1]), lambda i: (i, 0)) for xp in x_parts]
        + [pl.BlockSpec((None, K, N), lambda i: (layer, 0, 0)),
           pl.BlockSpec((tm, N), lambda i: (i, 0)),
           pl.BlockSpec((1, N), lambda i: (0, 0))],
        out_specs=[pl.BlockSpec((tm, N), lambda i: (i, 0)),
                   pl.BlockSpec((tm, N), lambda i: (i, 0))],
        out_shape=[jax.ShapeDtypeStruct((M, N), F32),
                   jax.ShapeDtypeStruct((M, N), BF16)],
        compiler_params=_params(("parallel",), est),
        name="matmul_residual_norm",
    )(*x_parts, w, r, gain.reshape(1, N).astype(F32))


def _pool_body(u_ref, halo_ref, w_ref, sc_ref, o_ref, ext_ref, *, tt, group):
    t = pl.program_id(1)
    u = u_ref[0].astype(F32)
    ext_ref[0:POOL_HALO, :] = jnp.where(t > 0, halo_ref[0].astype(F32), 0.0)
    ext_ref[POOL_HALO:POOL_HALO + tt, :] = u
    pos = t * tt + lax.broadcasted_iota(jnp.int32, (tt, 1), 0)
    for gi, win in enumerate(POOL_WINDOWS):
        c0, c1 = gi * group, (gi + 1) * group
        tok = u[:, c0:c1]
        acc = tok
        for d in range(1, win):
            acc = acc + ext_ref[POOL_HALO - d:POOL_HALO - d + tt, c0:c1]
        cnt = jnp.minimum(pos + 1, win).astype(F32)
        p = acc / cnt - tok
        y = jnp.dot(p.astype(BF16), w_ref[gi], preferred_element_type=F32)
        o_ref[0, :, c0:c1] = (y * sc_ref[:, c0:c1]).astype(o_ref.dtype)


def _pool_mixer(z, w_pool, scale, mix_a, tt=512):
    B, T, _ = z.shape
    tt = min(tt, T)
    group = mix_a // len(POOL_WINDOWS)
    halo_blocks = tt // POOL_HALO
    est = 2 * (_nbytes((tt, mix_a), F32) + _nbytes((tt, mix_a), BF16)) + 4 * _nbytes((tt, mix_a), F32)
    return pl.pallas_call(
        functools.partial(_pool_body, tt=tt, group=group),
        grid=(B, T // tt),
        in_specs=[pl.BlockSpec((1, tt, mix_a), lambda b, t: (b, t, 0)),
                  pl.BlockSpec((1, POOL_HALO, mix_a),
                               lambda b, t: (b, jnp.maximum(t * halo_blocks - 1, 0), 0)),
                  pl.BlockSpec((len(POOL_WINDOWS), group, group), lambda b, t: (0, 0, 0)),
                  pl.BlockSpec((1, mix_a), lambda b, t: (0, 0))],
        out_specs=pl.BlockSpec((1, tt, mix_a), lambda b, t: (b, t, 0)),
        out_shape=jax.ShapeDtypeStruct((B, T, mix_a), BF16),
        scratch_shapes=[pltpu.VMEM((tt + POOL_HALO, mix_a), F32)],
        compiler_params=_params(("parallel", "parallel"), est),
        name="pool_mixer",
    )(z, z, w_pool, scale.reshape(1, mix_a).astype(F32))


def _hgrn_subtile(qf, kk, v, logf, st):
    n = qf.shape[0]
    C, R = HG_CHUNK, HG_DIAG
    row = lax.broadcasted_iota(jnp.int32, (n, HG_HEAD), 0)

    pos = row & (C - 1)
    b = logf
    sh = 1
    while sh < C:
        b = b + jnp.where(pos >= sh, pltpu.roll(b, sh, axis=0), 0.0)
        sh *= 2

    nb = n // R
    b3, q3, c3, v3 = (a.reshape(nb, R, HG_HEAD) for a in (b, qf, b - jnp.log2(kk), v))
    tpos = lax.broadcasted_iota(jnp.int32, (1, R, 1), 1)
    od = jnp.zeros((nb, R, HG_HEAD), F32)
    for s in range(R):
        w = q3 * jnp.exp2(b3 - c3[:, s:s + 1, :])
        a = jnp.sum(w, axis=-1, keepdims=True)
        a = jnp.where(tpos >= s, a, 0.0)
        od = od + a * v3[:, s:s + 1, :]
    o = od.reshape(n, HG_HEAD)

    same = lax.broadcasted_iota(jnp.int32, (n, n), 0) ^ lax.broadcasted_iota(jnp.int32, (n, n), 1)
    a_off = jnp.zeros((n, n), F32)
    h = R
    while h < C:
        g = 2 * h
        ref = b.reshape(n // g, g, HG_HEAD)[:, h - 1:h, :]
        ref = jnp.broadcast_to(ref, (n // g, g, HG_HEAD)).reshape(n, HG_HEAD)
        e = jnp.exp2(-jnp.abs(b - ref))
        right = (row & (g - 1)) >= h
        qt = jnp.where(right, qf * e, 0.0).astype(BF16)
        kt = jnp.where(right, 0.0, kk * e).astype(BF16)
        a = lax.dot_general(qt, kt, (((1,), (1,)), ((), ())), preferred_element_type=F32)
        a_off = a_off + jnp.where(same < g, a, 0.0)
        h = g
    v_bf = v.astype(BF16)
    o = o + jnp.dot(a_off.astype(BF16), v_bf, preferred_element_type=F32)

    nc = n // C
    bc = b.reshape(nc, C, HG_HEAD)
    b_last = bc[:, C - 1:C, :]
    q_in = (qf * jnp.exp2(b)).astype(BF16)
    k_out = (kk.reshape(nc, C, HG_HEAD) * jnp.exp2(b_last - bc)).astype(BF16)
    dec = jnp.exp2(b_last)
    pieces = []
    for c in range(nc):
        rows = slice(c * C, (c + 1) * C)
        o_int = lax.dot_general(q_in[rows], st.astype(BF16), (((1,), (1,)), ((), ())),
                                preferred_element_type=F32)
        pieces.append(o[rows] + o_int)
        upd = lax.dot_general(v_bf[rows], k_out[c], (((0,), (0,)), ((), ())),
                              preferred_element_type=F32)
        st = st * dec[c] + upd
    return jnp.concatenate(pieces, axis=0), st


def _hgrn_body(lbt_ref, ng_ref, q_ref, f_ref, i_ref, g_ref, o_ref, st_ref, *, layer, tt, sub):
    t = pl.program_id(2)

    @pl.when(t == 0)
    def _():
        st_ref[...] = jnp.zeros_like(st_ref)

    lbt = lbt_ref[...]
    e = jnp.exp(lbt - jnp.max(lbt, axis=0, keepdims=True))
    sm = e / jnp.sum(e, axis=0, keepdims=True)
    lb = jnp.sum(sm[1:layer + 2], axis=0, keepdims=True)

    st = st_ref[...]
    for c in range(tt // sub):
        rows = slice(c * sub, (c + 1) * sub)
        f = lb + (1.0 - lb) * jax.nn.sigmoid(f_ref[0, rows, :].astype(F32))
        q = q_ref[0, rows, :].astype(F32)
        qf = q * jax.nn.sigmoid(q) * (HG_HEAD ** -0.5)
        o, st = _hgrn_subtile(qf, 1.0 - f, i_ref[0, rows, :].astype(F32), jnp.log2(f), st)
        ms = jnp.mean(o * o, axis=-1, keepdims=True)
        o = o * lax.rsqrt(ms + EPS) * ng_ref[...]
        g = g_ref[0, rows, :].astype(F32)
        o_ref[0, rows, :] = (o * (g * jax.nn.sigmoid(g))).astype(o_ref.dtype)
    st_ref[...] = st


def _hgrn_mixer(z, lb_table, norm_g, layer, mix_a, mix_b, tt=512):
    B, T, _ = z.shape
    tt = min(tt, T)
    sub = min(HG_SUB, tt)
    heads = mix_b // HG_HEAD
    c0 = mix_a // HG_HEAD
    est = 2 * 5 * _nbytes((tt, HG_HEAD), F32) + 32 * _nbytes((sub, HG_HEAD), F32) + 6 * _nbytes((sub, sub), F32)

    def col(off):
        return pl.BlockSpec((1, tt, HG_HEAD), lambda b, h, t, off=off: (b, t, c0 + off * heads + h))

    return pl.pallas_call(
        functools.partial(_hgrn_body, layer=layer, tt=tt, sub=sub),
        grid=(B, heads, T // tt),
        in_specs=[pl.BlockSpec((lb_table.shape[0], HG_HEAD), lambda b, h, t: (0, h)),
                  pl.BlockSpec((1, HG_HEAD), lambda b, h, t: (0, 0)),
                  col(0), col(1), col(2), col(3)],
        out_specs=pl.BlockSpec((1, tt, HG_HEAD), lambda b, h, t: (b, t, h)),
        out_shape=jax.ShapeDtypeStruct((B, T, mix_b), BF16),
        scratch_shapes=[pltpu.VMEM((HG_HEAD, HG_HEAD), F32)],
        compiler_params=_params(("parallel", "parallel", "arbitrary"), est),
        name="hgrn2_mixer",
    )(lb_table.astype(F32), norm_g.reshape(1, HG_HEAD).astype(F32), z, z, z, z)


def _fgate_body(h_ref, w_ref, b_ref, *rest, tt, norm_width):
    if norm_width:
        ss_ref, o_ref, carry_ref = rest
    else:
        o_ref, carry_ref = rest
    t = pl.program_id(1)

    @pl.when(t == 0)
    def _():
        carry_ref[...] = jnp.zeros_like(carry_ref)

    fl = jnp.dot(h_ref[0], w_ref[...], preferred_element_type=F32)
    if norm_width:
        fl = fl * _row_rsqrt(ss_ref, norm_width)
    fl = fl + b_ref[...]
    c = (jnp.minimum(fl, 0.0) - jnp.log1p(jnp.exp(-jnp.abs(fl)))) * LOG2E
    row = lax.broadcasted_iota(jnp.int32, c.shape, 0)
    sh = 1
    while sh < tt:
        c = c + jnp.where(row >= sh, pltpu.roll(c, sh, axis=0), 0.0)
        sh *= 2
    c = c + carry_ref[...]
    o_ref[0] = c
    carry_ref[...] = c[tt - 1:tt, :]


def _fox_gates(h, wf, bf, row_ss=None, tt=512):
    B, T, D = h.shape
    tt = min(tt, T)
    ss_specs = [] if row_ss is None else [pl.BlockSpec((row_ss.shape[0], None, tt, 1), lambda b, t: (0, b, t, 0))]
    ss_args = [] if row_ss is None else [row_ss]
    est = 2 * (_nbytes((tt, D), BF16) + _nbytes((D, LANES), BF16) + _nbytes((tt, LANES), F32)) \
        + 8 * _nbytes((tt, LANES), F32)
    return pl.pallas_call(
        functools.partial(_fgate_body, tt=tt, norm_width=0 if row_ss is None else D),
        grid=(B, T // tt),
        in_specs=[pl.BlockSpec((1, tt, D), lambda b, t: (b, t, 0)),
                  pl.BlockSpec((D, LANES), lambda b, t: (0, 0)),
                  pl.BlockSpec((1, LANES), lambda b, t: (0, 0))] + ss_specs,
        out_specs=pl.BlockSpec((1, tt, LANES), lambda b, t: (b, t, 0)),
        out_shape=jax.ShapeDtypeStruct((B, T, LANES), F32),
        scratch_shapes=[pltpu.VMEM((1, LANES), F32)],
        compiler_params=_params(("parallel", "arbitrary"), est),
        name="fox_gates",
    )(h, wf, bf, *ss_args)


def _fox_body(q_ref, k_ref, v_ref, f_ref, fr_ref, o_ref, vt_ref, fk_ref, p_ref, acc_ref, *, tq, seq, group):
    hg = pl.program_id(1)
    i = pl.program_id(2)
    reps = tq // LANES
    heads = range(group)

    def head_cols(g):
        return slice(g * FOX_HEAD, (g + 1) * FOX_HEAD)

    @pl.when(i == 0)
    def _():
        for c in range(seq // tq):
            rows = slice(c * tq, (c + 1) * tq)
            fblk = f_ref[0, rows, :]
            lane = lax.broadcasted_iota(jnp.int32, fblk.shape, 1)
            for g in heads:
                vt_ref[g, :, rows] = v_ref[0, rows, head_cols(g)].T
                col = jnp.sum(jnp.where(lane == hg * group + g, fblk, 0.0), axis=-1, keepdims=True)
                fk_ref[g, rows, :] = jnp.broadcast_to(col, fblk.shape)

    qt = [q_ref[0, :, head_cols(g)].T for g in heads]
    fq = [fr_ref[0, g, i] for g in heads]

    def scores(g, j):
        start = pl.multiple_of(j * tq, tq)
        return jnp.dot(k_ref[0, pl.ds(start, tq), head_cols(g)], qt[g], preferred_element_type=F32)

    def softmax_block(g, j, st, m_prev, l_prev, diagonal):
        start = pl.multiple_of(j * tq, tq)
        t = st - jnp.concatenate([fk_ref[g, pl.ds(start, tq), :]] * reps, axis=1)
        if diagonal:
            key = lax.broadcasted_iota(jnp.int32, (tq, tq), 0)
            qry = lax.broadcasted_iota(jnp.int32, (tq, tq), 1)
            t = jnp.where(key <= qry, t, -jnp.inf)
        m_new = jnp.maximum(m_prev, jnp.max(t, axis=0, keepdims=True) + fq[g])
        alpha = jnp.exp2(m_prev - m_new)
        p = jnp.exp2(t - (m_new - fq[g]))
        l_new = alpha * l_prev + jnp.sum(p, axis=0, keepdims=True)
        return p.astype(BF16), m_new, l_new, alpha

    def weighted_values(g, j, p):
        start = pl.multiple_of(j * tq, tq)
        return jnp.dot(vt_ref[g, :, pl.ds(start, tq)], p, preferred_element_type=F32)

    def block(j, j_prev, stats, first):
        out = []
        for g in heads:
            m, l = stats[g]
            st = scores(g, j)
            if first:
                p, m, l, _ = softmax_block(g, j, st, m, l, True)
                acc_ref[g] = jnp.zeros((FOX_HEAD, tq), F32)
            else:
                pv_prev = weighted_values(g, j_prev, p_ref[g])
                p, m, l, alpha = softmax_block(g, j, st, m, l, False)
                acc_ref[g] = (acc_ref[g] + pv_prev) * alpha
            p_ref[g] = p
            out.append((m, l))
        return tuple(out)

    stats = tuple((jnp.full((1, tq), -jnp.inf, F32), jnp.zeros((1, tq), F32)) for _ in heads)
    stats = block(i, i, stats, True)
    stats = lax.fori_loop(0, i, lambda j, c: block(j, jnp.where(j == 0, i, j - 1), c, False), stats)
    j_last = jnp.where(i == 0, i, i - 1)
    for g in heads:
        acc = acc_ref[g] + weighted_values(g, j_last, p_ref[g])
        o_ref[0, :, head_cols(g)] = (acc / stats[g][1]).astype(o_ref.dtype).T


def _fox_attention(qkv, f, frow, heads, tq, group=FOX_GROUP):
    B, T, _ = qkv.shape
    nq = T // tq
    gw = group * FOX_HEAD
    ngrp = heads // group
    est = 2 * (2 * _nbytes((T, gw), BF16) + 2 * _nbytes((tq, gw), BF16)
               + _nbytes((T, LANES), F32) + group * _nbytes((nq, 8, tq), F32)) \
        + group * (_nbytes((T, FOX_HEAD), BF16) + _nbytes((T, LANES), F32) + 5 * _nbytes((tq, tq), F32))
    return pl.pallas_call(
        functools.partial(_fox_body, tq=tq, seq=T, group=group),
        grid=(B, ngrp, nq),
        in_specs=[pl.BlockSpec((1, tq, gw), lambda b, h, i: (b, i, h)),
                  pl.BlockSpec((1, T, gw), lambda b, h, i: (b, 0, ngrp + h)),
                  pl.BlockSpec((1, T, gw), lambda b, h, i: (b, 0, 2 * ngrp + h)),
                  pl.BlockSpec((1, T, LANES), lambda b, h, i: (b, 0, 0)),
                  pl.BlockSpec((1, group, nq, 1, tq), lambda b, h, i: (b, h, 0, 0, 0))],
        out_specs=pl.BlockSpec((1, tq, gw), lambda b, h, i: (b, i, h)),
        out_shape=jax.ShapeDtypeStruct((B, T, heads * FOX_HEAD), BF16),
        scratch_shapes=[pltpu.VMEM((group, FOX_HEAD, T), BF16),
                        pltpu.VMEM((group, T, LANES), F32),
                        pltpu.VMEM((group, tq, tq), BF16),
                        pltpu.VMEM((group, FOX_HEAD, tq), F32)],
        compiler_params=_params(("parallel", "parallel", "arbitrary"), est),
        name="fox_attention",
    )(qkv, qkv, qkv, f, frow)


def _xattn_out_body(q_ref, kv_ref, w_ref, r_ref, g_ref, o_ref, h_ref, a_ref, *, d_model):
    hd = d_model // XA_HEADS
    for hh in range(XA_HEADS):
        q = q_ref[:, hh * hd:(hh + 1) * hd]
        k = kv_ref[:, hh * hd:(hh + 1) * hd]
        v = kv_ref[:, d_model + hh * hd:d_model + (hh + 1) * hd]
        s = lax.dot_general(q, k, (((1,), (1,)), ((), ())), preferred_element_type=F32)
        p = jnp.exp(s - jnp.max(s, axis=-1, keepdims=True))
        p = p / jnp.sum(p, axis=-1, keepdims=True)
        a_ref[:, hh * hd:(hh + 1) * hd] = jnp.dot(p.astype(BF16), v, preferred_element_type=F32).astype(a_ref.dtype)
    acc = r_ref[...] + jnp.dot(a_ref[...], w_ref[...], preferred_element_type=F32)
    o_ref[...] = acc
    ms = jnp.mean(acc * acc, axis=-1, keepdims=True)
    h_ref[...] = (acc * lax.rsqrt(ms + EPS) * g_ref[...]).astype(h_ref.dtype)


def _xattn_out_norm(q, kv, w, layer, r, gain, seq, tm=512):
    M, D = q.shape
    n_mem = kv.shape[1]
    tm = _pick_tile(seq, tm)
    per_batch = seq // tm
    est = 2 * (_nbytes((tm, D), BF16) + _nbytes((n_mem, 2 * D), BF16) + _nbytes((D, D), BF16)
               + 2 * _nbytes((tm, D), F32) + _nbytes((tm, D), BF16)) \
        + _nbytes((tm, D), BF16) + 3 * _nbytes((tm, D), F32)
    row = pl.BlockSpec((tm, D), lambda i: (i, 0))
    return pl.pallas_call(
        functools.partial(_xattn_out_body, d_model=D),
        grid=(M // tm,),
        in_specs=[row,
                  pl.BlockSpec((None, n_mem, 2 * D), lambda i: (i // per_batch, 0, 0)),
                  pl.BlockSpec((None, D, D), lambda i: (layer, 0, 0)),
                  row,
                  pl.BlockSpec((1, D), lambda i: (0, 0))],
        out_specs=[row, row],
        out_shape=[jax.ShapeDtypeStruct((M, D), F32),
                   jax.ShapeDtypeStruct((M, D), BF16)],
        scratch_shapes=[pltpu.VMEM((tm, D), BF16)],
        compiler_params=_params(("parallel",), est),
        name="xattn_out_norm",
    )(q, kv, w, r, gain.reshape(1, D).astype(F32))


def kernel(x, mem, lb_table, ev_norm, ev_w_in, ev_w_pool, ev_pool_scale, ev_hg_norm, ev_w_out,
           od_norm, od_w_in, od_b_f, od_w_out, xa_norm, xa_mem_norm, xa_wq, xa_wkv, xa_wo,
           ffn_norm, ffn_w_gate, ffn_w_up, ffn_w_down, final_norm):
    B, T, D = x.shape
    M = B * T
    depth = xa_norm.shape[0]
    mix_a = ev_pool_scale.shape[1]
    mix_b = lb_table.shape[1]
    fox_heads = od_b_f.shape[1]
    n_mem = mem.shape[1]
    xa_scale = (D // XA_HEADS) ** -0.5
    fox_tq = _pick_tile(T, 512)

    xs = x.reshape(M, D)
    mem2 = mem.reshape(B * n_mem, D)
    w_down = ffn_w_down.astype(BF16)
    ev_wo, od_wo, xa_wo_b = ev_w_out.astype(BF16), od_w_out.astype(BF16), xa_wo.astype(BF16)
    pre = None
    for l in range(depth):
        if pre is None:
            h, ss = _rmsnorm(xs, ev_norm[l // 2] if l % 2 == 0 else od_norm[l // 2], BF16), None
        else:
            h, ss = pre
        if l % 2 == 0:
            e = l // 2
            z = _matmul(h, ev_w_in, e, ev_w_in.shape[2], BF16, row_ss=ss).reshape(B, T, -1)
            ya = _pool_mixer(z, ev_w_pool[e].astype(BF16), ev_pool_scale[e], mix_a)
            yb = _hgrn_mixer(z, lb_table, ev_hg_norm[e], l, mix_a, mix_b)
            xs, h = _matmul_residual_norm([ya.reshape(M, mix_a), yb.reshape(M, mix_b)], ev_wo, e, xs, xa_norm[l])
        else:
            o = l // 2
            qscale = jnp.concatenate([jnp.full((D,), LOG2E * FOX_HEAD ** -0.5, F32), jnp.ones((2 * D,), F32)])
            w_in = od_w_in.astype(BF16)
            qkv = _matmul(h, w_in, o, 3 * D, BF16, colscale=qscale, row_ss=ss).reshape(B, T, 3 * D)
            wf = jnp.zeros((D, LANES), BF16).at[:, :fox_heads].set(w_in[o, :, 3 * D:])
            bf = jnp.zeros((1, LANES), F32).at[0, :fox_heads].set(od_b_f[o].astype(F32))
            ss4 = None if ss is None else ss.reshape(ss.shape[0], B, T, 1)
            f = _fox_gates(h.reshape(B, T, D), wf, bf, ss4)
            frow = f[..., :fox_heads].transpose(0, 2, 1).reshape(B, fox_heads, T // fox_tq, 1, fox_tq)
            y = _fox_attention(qkv, f, frow, fox_heads, fox_tq).reshape(M, D)
            xs, h = _matmul_residual_norm([y], od_wo, o, xs, xa_norm[l])

        mn = _rmsnorm(mem2, xa_mem_norm[l], BF16)
        q = _matmul(h, xa_wq, l, D, BF16, colscale=jnp.full((D,), xa_scale, F32))
        kv = _matmul(mn, xa_wkv, l, 2 * D, BF16)
        xs, h = _xattn_out_norm(q, kv.reshape(B, n_mem, 2 * D), xa_wo_b, l, xs, ffn_norm[l], T)

        act = _matmul_swiglu(h, ffn_w_gate, ffn_w_up, l)
        if l + 1 < depth:
            nxt = l + 1
            gain = ev_norm[nxt // 2] if nxt % 2 == 0 else od_norm[nxt // 2]
            xs, hp, ss = _matmul_residual(act, w_down, l, xs, gain=gain)
            pre = (hp, ss)
        else:
            xs = _matmul_residual(act, w_down, l, xs)
    return _rmsnorm(xs, final_norm, x.dtype).reshape(B, T, D)
```

```python
import functools

import jax
import jax.numpy as jnp
from jax import lax
from jax.experimental import pallas as pl
from jax.experimental.pallas import tpu as pltpu

F32 = jnp.float32
BF16 = jnp.bfloat16

EPS = 1e-6
N_MEM = 256
POOL_WINDOWS = (2, 4, 8, 16)
POOL_HALO = 16
HG_HEAD = 128
HG_CHUNK = 64
HG_DIAG = 8
HG_SUB = 256
FOX_HEAD = 128
FOX_GROUP = 4
XA_HEADS = 4

LOG2E = 1.4426950408889634
LANES = 128
VMEM_CAP = 56 * 1024 * 1024
VMEM_FLOOR = 32 * 1024 * 1024


def _params(semantics, vmem_estimate):
    limit = int(min(max(vmem_estimate * 5 // 4, VMEM_FLOOR), VMEM_CAP))
    return pltpu.CompilerParams(dimension_semantics=semantics, vmem_limit_bytes=limit)


def _nbytes(shape, dtype):
    n = jnp.dtype(dtype).itemsize
    for s in shape:
        n *= s
    return n


def _rmsnorm_body(x_ref, g_ref, o_ref):
    x = x_ref[...]
    ms = jnp.mean(x * x, axis=-1, keepdims=True)
    o_ref[...] = (x * lax.rsqrt(ms + EPS) * g_ref[...]).astype(o_ref.dtype)


def _rmsnorm(x, g, out_dtype, tm=512):
    M, D = x.shape
    tm = min(tm, M)
    est = 2 * (_nbytes((tm, D), x.dtype) + _nbytes((tm, D), out_dtype)) + 3 * _nbytes((tm, D), F32)
    return pl.pallas_call(
        _rmsnorm_body,
        grid=(M // tm,),
        in_specs=[pl.BlockSpec((tm, D), lambda i: (i, 0)),
                  pl.BlockSpec((1, D), lambda i: (0, 0))],
        out_specs=pl.BlockSpec((tm, D), lambda i: (i, 0)),
        out_shape=jax.ShapeDtypeStruct((M, D), out_dtype),
        compiler_params=_params(("parallel",), est),
        name="rmsnorm",
    )(x, g.reshape(1, D).astype(F32))


def _with_bf16_weights(w_refs, wb_refs, compute):
    if not wb_refs:
        compute(*(w[...] for w in w_refs))
        return
    i = pl.program_id(1)

    @pl.when(i == 0)
    def _():
        for w, wb in zip(w_refs, wb_refs):
            wb[...] = w[...].astype(BF16)

    @pl.when(i > 0)
    def _():
        compute(*(wb[...] for wb in wb_refs))


def _row_rsqrt(ss_ref, width):
    return lax.rsqrt(jnp.sum(ss_ref[...], axis=0) * (1.0 / width) + EPS)


def _mm_body(x_ref, w_ref, cs_ref, *rest, norm_width):
    if norm_width:
        ss_ref, o_ref, *wb = rest
    else:
        o_ref, *wb = rest

    def compute(w):
        acc = jnp.dot(x_ref[...], w, preferred_element_type=F32)
        if norm_width:
            acc = acc * _row_rsqrt(ss_ref, norm_width)
        o_ref[...] = (acc * cs_ref[...]).astype(o_ref.dtype)

    _with_bf16_weights((w_ref,), wb, compute)


def _mm_res_body(x_ref, w_ref, r_ref, *rest, with_gain):
    if with_gain:
        g_ref, o_ref, hp_ref, ss_ref, *wb = rest
    else:
        o_ref, *wb = rest

    def compute(w):
        acc = r_ref[...] + jnp.dot(x_ref[...], w, preferred_element_type=F32)
        o_ref[...] = acc
        if with_gain:
            hp_ref[...] = (acc * g_ref[...]).astype(hp_ref.dtype)
            ss_ref[...] = jnp.sum(acc * acc, axis=-1, keepdims=True)

    _with_bf16_weights((w_ref,), wb, compute)


def _mm_swiglu_body(x_ref, wg_ref, wu_ref, o_ref, *wb):
    def compute(wg, wu):
        x = x_ref[...]
        g = jnp.dot(x, wg, preferred_element_type=F32)
        u = jnp.dot(x, wu, preferred_element_type=F32)
        o_ref[...] = (g * jax.nn.sigmoid(g) * u).astype(o_ref.dtype)

    _with_bf16_weights((wg_ref, wu_ref), wb, compute)


def _pick_tile(n, pref):
    t = min(pref, n)
    while n % t:
        t //= 2
    return t


def _weight_spec(layer, K, tn):
    return pl.BlockSpec((None, K, tn), lambda j, i: (layer, 0, j))


def _weight_scratch(w, K, tn):
    return [] if w.dtype == BF16 else [pltpu.VMEM((K, tn), BF16)]


def _weight_bytes(w, K, tn):
    return 2 * _nbytes((K, tn), w.dtype) + (0 if w.dtype == BF16 else _nbytes((K, tn), BF16))


def _row_steps(w, M, tm):
    extra = 0 if w.dtype == BF16 else 1
    return M // tm + extra, lambda i: jnp.maximum(i - extra, 0)


def _matmul(x, w, layer, n_cols, out_dtype, colscale=None, row_ss=None, tm=1024, tn=1024):
    M, K = x.shape
    tm, tn = _pick_tile(M, tm), _pick_tile(n_cols, tn)
    if colscale is None:
        colscale = jnp.ones((n_cols,), F32)
    steps, row = _row_steps(w, M, tm)
    ss_specs = [] if row_ss is None else [pl.BlockSpec((row_ss.shape[0], tm, 1), lambda j, i: (0, row(i), 0))]
    ss_args = [] if row_ss is None else [row_ss]
    est = 2 * (_nbytes((tm, K), x.dtype) + _nbytes((tm, tn), out_dtype)) + _weight_bytes(w, K, tn) \
        + 2 * _nbytes((tm, tn), F32)
    return pl.pallas_call(
        functools.partial(_mm_body, norm_width=0 if row_ss is None else K),
        grid=(n_cols // tn, steps),
        in_specs=[pl.BlockSpec((tm, K), lambda j, i: (row(i), 0)),
                  _weight_spec(layer, K, tn),
                  pl.BlockSpec((1, tn), lambda j, i: (0, j))] + ss_specs,
        out_specs=pl.BlockSpec((tm, tn), lambda j, i: (row(i), j)),
        out_shape=jax.ShapeDtypeStruct((M, n_cols), out_dtype),
        scratch_shapes=_weight_scratch(w, K, tn),
        compiler_params=_params(("parallel", "arbitrary"), est),
        name="matmul",
    )(x, w, colscale.reshape(1, n_cols), *ss_args)


def _matmul_residual(x, w, layer, r, gain=None, tn=1024):
    M, K = x.shape
    N = w.shape[2]
    tm = 1024 if K <= 2048 else 512
    tm, tn = _pick_tile(M, tm), _pick_tile(N, tn)
    steps, row = _row_steps(w, M, tm)
    est = 2 * (_nbytes((tm, K), x.dtype) + 2 * _nbytes((tm, tn), F32)) + _weight_bytes(w, K, tn) \
        + 2 * _nbytes((tm, tn), F32)
    tile = pl.BlockSpec((tm, tn), lambda j, i: (row(i), j))
    in_specs = [pl.BlockSpec((tm, K), lambda j, i: (row(i), 0)), _weight_spec(layer, K, tn), tile]
    out_specs, out_shape, args = tile, jax.ShapeDtypeStruct((M, N), F32), [x, w, r]
    if gain is not None:
        in_specs.append(pl.BlockSpec((1, tn), lambda j, i: (0, j)))
        args.append(gain.reshape(1, N).astype(F32))
        out_specs = [tile, tile, pl.BlockSpec((None, tm, 1), lambda j, i: (j, row(i), 0))]
        out_shape = [out_shape, jax.ShapeDtypeStruct((M, N), BF16), jax.ShapeDtypeStruct((N // tn, M, 1), F32)]
    return pl.pallas_call(
        functools.partial(_mm_res_body, with_gain=gain is not None),
        grid=(N // tn, steps),
        in_specs=in_specs,
        out_specs=out_specs,
        out_shape=out_shape,
        scratch_shapes=_weight_scratch(w, K, tn),
        compiler_params=_params(("parallel", "arbitrary"), est),
        name="matmul_residual",
    )(*args)


def _matmul_swiglu(x, wg, wu, layer, tm=1024, tn=512):
    M, K = x.shape
    N = wg.shape[2]
    tm, tn = _pick_tile(M, tm), _pick_tile(N, tn)
    steps, row = _row_steps(wg, M, tm)
    est = 2 * (_nbytes((tm, K), x.dtype) + _nbytes((tm, tn), BF16)) + 2 * _weight_bytes(wg, K, tn) \
        + 4 * _nbytes((tm, tn), F32)
    return pl.pallas_call(
        _mm_swiglu_body,
        grid=(N // tn, steps),
        in_specs=[pl.BlockSpec((tm, K), lambda j, i: (row(i), 0)),
                  _weight_spec(layer, K, tn),
                  _weight_spec(layer, K, tn)],
        out_specs=pl.BlockSpec((tm, tn), lambda j, i: (row(i), j)),
        out_shape=jax.ShapeDtypeStruct((M, N), BF16),
        scratch_shapes=_weight_scratch(wg, K, tn) + _weight_scratch(wu, K, tn),
        compiler_params=_params(("parallel", "arbitrary"), est),
        name="matmul_swiglu",
    )(x, wg, wu)


def _mm_res_norm_body(*refs, n_parts):
    x_refs = refs[:n_parts]
    w_ref, r_ref, g_ref, o_ref, h_ref = refs[n_parts:]
    acc = r_ref[...]
    k0 = 0
    for x_ref in x_refs:
        kw = x_ref.shape[1]
        acc = acc + jnp.dot(x_ref[...], w_ref[k0:k0 + kw, :], preferred_element_type=F32)
        k0 += kw
    o_ref[...] = acc
    ms = jnp.mean(acc * acc, axis=-1, keepdims=True)
    h_ref[...] = (acc * lax.rsqrt(ms + EPS) * g_ref[...]).astype(h_ref.dtype)


def _matmul_residual_norm(x_parts, w, layer, r, gain, tm=512):
    M, N = r.shape
    K = w.shape[1]
    tm = _pick_tile(M, tm)
    est = 2 * (_nbytes((tm, K), BF16) + _nbytes((K, N), w.dtype) + 2 * _nbytes((tm, N), F32) + _nbytes((tm, N), BF16)) \
        + 3 * _nbytes((tm, N), F32)
    return pl.pallas_call(
        functools.partial(_mm_res_norm_body, n_parts=len(x_parts)),
        grid=(M // tm,),
        in_specs=[pl.BlockSpec((tm, xp.shape[1]), lambda i: (i, 0)) for xp in x_parts]
        + [pl.BlockSpec((None, K, N), lambda i: (layer, 0, 0)),
           pl.BlockSpec((tm, N), lambda i: (i, 0)),
           pl.BlockSpec((1, N), lambda i: (0, 0))],
        out_specs=[pl.BlockSpec((tm, N), lambda i: (i, 0)),
                   pl.BlockSpec((tm, N), lambda i: (i, 0))],
        out_shape=[jax.ShapeDtypeStruct((M, N), F32),
                   jax.ShapeDtypeStruct((M, N), BF16)],
        compiler_params=_params(("parallel",), est),
        name="matmul_residual_norm",
    )(*x_parts, w, r, gain.reshape(1, N).astype(F32))


def _pool_body(u_ref, halo_ref, w_ref, sc_ref, o_ref, ext_ref, *, tt, group):
    t = pl.program_id(1)
    u = u_ref[0].astype(F32)
    ext_ref[0:POOL_HALO, :] = jnp.where(t > 0, halo_ref[0].astype(F32), 0.0)
    ext_ref[POOL_HALO:POOL_HALO + tt, :] = u
    pos = t * tt + lax.broadcasted_iota(jnp.int32, (tt, 1), 0)
    for gi, win in enumerate(POOL_WINDOWS):
        c0, c1 = gi * group, (gi + 1) * group
        tok = u[:, c0:c1]
        acc = tok
        for d in range(1, win):
            acc = acc + ext_ref[POOL_HALO - d:POOL_HALO - d + tt, c0:c1]
        cnt = jnp.minimum(pos + 1, win).astype(F32)
        p = acc / cnt - tok
        y = jnp.dot(p.astype(BF16), w_ref[gi], preferred_element_type=F32)
        o_ref[0, :, c0:c1] = (y * sc_ref[:, c0:c1]).astype(o_ref.dtype)


def _pool_mixer(z, w_pool, scale, mix_a, tt=512):
    B, T, _ = z.shape
    tt = min(tt, T)
    group = mix_a // len(POOL_WINDOWS)
    halo_blocks = tt // POOL_HALO
    est = 2 * (_nbytes((tt, mix_a), F32) + _nbytes((tt, mix_a), BF16)) + 4 * _nbytes((tt, mix_a), F32)
    return pl.pallas_call(
        functools.partial(_pool_body, tt=tt, group=group),
        grid=(B, T // tt),
        in_specs=[pl.BlockSpec((1, tt, mix_a), lambda b, t: (b, t, 0)),
                  pl.BlockSpec((1, POOL_HALO, mix_a),
                               lambda b, t: (b, jnp.maximum(t * halo_blocks - 1, 0), 0)),
                  pl.BlockSpec((len(POOL_WINDOWS), group, group), lambda b, t: (0, 0, 0)),
                  pl.BlockSpec((1, mix_a), lambda b, t: (0, 0))],
        out_specs=pl.BlockSpec((1, tt, mix_a), lambda b, t: (b, t, 0)),
        out_shape=jax.ShapeDtypeStruct((B, T, mix_a), BF16),
        scratch_shapes=[pltpu.VMEM((tt + POOL_HALO, mix_a), F32)],
        compiler_params=_params(("parallel", "parallel"), est),
        name="pool_mixer",
    )(z, z, w_pool, scale.reshape(1, mix_a).astype(F32))


def _hgrn_subtile(qf, kk, v, logf, st):
    n = qf.shape[0]
    C, R = HG_CHUNK, HG_DIAG
    row = lax.broadcasted_iota(jnp.int32, (n, HG_HEAD), 0)

    pos = row & (C - 1)
    b = logf
    sh = 1
    while sh < C:
        b = b + jnp.where(pos >= sh, pltpu.roll(b, sh, axis=0), 0.0)
        sh *= 2

    nb = n // R
    b3, q3, c3, v3 = (a.reshape(nb, R, HG_HEAD) for a in (b, qf, b - jnp.log2(kk), v))
    tpos = lax.broadcasted_iota(jnp.int32, (1, R, 1), 1)
    od = jnp.zeros((nb, R, HG_HEAD), F32)
    for s in range(R):
        w = q3 * jnp.exp2(b3 - c3[:, s:s + 1, :])
        a = jnp.sum(w, axis=-1, keepdims=True)
        a = jnp.where(tpos >= s, a, 0.0)
        od = od + a * v3[:, s:s + 1, :]
    o = od.reshape(n, HG_HEAD)

    same = lax.broadcasted_iota(jnp.int32, (n, n), 0) ^ lax.broadcasted_iota(jnp.int32, (n, n), 1)
    a_off = jnp.zeros((n, n), F32)
    h = R
    while h < C:
        g = 2 * h
        ref = b.reshape(n // g, g, HG_HEAD)[:, h - 1:h, :]
        ref = jnp.broadcast_to(ref, (n // g, g, HG_HEAD)).reshape(n, HG_HEAD)
        e = jnp.exp2(-jnp.abs(b - ref))
        right = (row & (g - 1)) >= h
        qt = jnp.where(right, qf * e, 0.0).astype(BF16)
        kt = jnp.where(right, 0.0, kk * e).astype(BF16)
        a = lax.dot_general(qt, kt, (((1,), (1,)), ((), ())), preferred_element_type=F32)
        a_off = a_off + jnp.where(same < g, a, 0.0)
        h = g
    v_bf = v.astype(BF16)
    o = o + jnp.dot(a_off.astype(BF16), v_bf, preferred_element_type=F32)

    nc = n // C
    bc = b.reshape(nc, C, HG_HEAD)
    b_last = bc[:, C - 1:C, :]
    q_in = (qf * jnp.exp2(b)).astype(BF16)
    k_out = (kk.reshape(nc, C, HG_HEAD) * jnp.exp2(b_last - bc)).astype(BF16)
    dec = jnp.exp2(b_last)
    pieces = []
    for c in range(nc):
        rows = slice(c * C, (c + 1) * C)
        o_int = lax.dot_general(q_in[rows], st.astype(BF16), (((1,), (1,)), ((), ())),
                                preferred_element_type=F32)
        pieces.append(o[rows] + o_int)
        upd = lax.dot_general(v_bf[rows], k_out[c], (((0,), (0,)), ((), ())),
                              preferred_element_type=F32)
        st = st * dec[c] + upd
    return jnp.concatenate(pieces, axis=0), st


def _hgrn_body(lbt_ref, ng_ref, q_ref, f_ref, i_ref, g_ref, o_ref, st_ref, *, layer, tt, sub):
    t = pl.program_id(2)

    @pl.when(t == 0)
    def _():
        st_ref[...] = jnp.zeros_like(st_ref)

    lbt = lbt_ref[...]
    e = jnp.exp(lbt - jnp.max(lbt, axis=0, keepdims=True))
    sm = e / jnp.sum(e, axis=0, keepdims=True)
    lb = jnp.sum(sm[1:layer + 2], axis=0, keepdims=True)

    st = st_ref[...]
    for c in range(tt // sub):
        rows = slice(c * sub, (c + 1) * sub)
        f = lb + (1.0 - lb) * jax.nn.sigmoid(f_ref[0, rows, :].astype(F32))
        q = q_ref[0, rows, :].astype(F32)
        qf = q * jax.nn.sigmoid(q) * (HG_HEAD ** -0.5)
        o, st = _hgrn_subtile(qf, 1.0 - f, i_ref[0, rows, :].astype(F32), jnp.log2(f), st)
        ms = jnp.mean(o * o, axis=-1, keepdims=True)
        o = o * lax.rsqrt(ms + EPS) * ng_ref[...]
        g = g_ref[0, rows, :].astype(F32)
        o_ref[0, rows, :] = (o * (g * jax.nn.sigmoid(g))).astype(o_ref.dtype)
    st_ref[...] = st


def _hgrn_mixer(z, lb_table, norm_g, layer, mix_a, mix_b, tt=512):
    B, T, _ = z.shape
    tt = min(tt, T)
    sub = min(HG_SUB, tt)
    heads = mix_b // HG_HEAD
    c0 = mix_a // HG_HEAD
    est = 2 * 5 * _nbytes((tt, HG_HEAD), F32) + 32 * _nbytes((sub, HG_HEAD), F32) + 6 * _nbytes((sub, sub), F32)

    def col(off):
        return pl.BlockSpec((1, tt, HG_HEAD), lambda b, h, t, off=off: (b, t, c0 + off * heads + h))

    return pl.pallas_call(
        functools.partial(_hgrn_body, layer=layer, tt=tt, sub=sub),
        grid=(B, heads, T // tt),
        in_specs=[pl.BlockSpec((lb_table.shape[0], HG_HEAD), lambda b, h, t: (0, h)),
                  pl.BlockSpec((1, HG_HEAD), lambda b, h, t: (0, 0)),
                  col(0), col(1), col(2), col(3)],
        out_specs=pl.BlockSpec((1, tt, HG_HEAD), lambda b, h, t: (b, t, h)),
        out_shape=jax.ShapeDtypeStruct((B, T, mix_b), BF16),
        scratch_shapes=[pltpu.VMEM((HG_HEAD, HG_HEAD), F32)],
        compiler_params=_params(("parallel", "parallel", "arbitrary"), est),
        name="hgrn2_mixer",
    )(lb_table.astype(F32), norm_g.reshape(1, HG_HEAD).astype(F32), z, z, z, z)


def _fgate_body(h_ref, w_ref, b_ref, *rest, tt, norm_width):
    if norm_width:
        ss_ref, o_ref, carry_ref = rest
    else:
        o_ref, carry_ref = rest
    t = pl.program_id(1)

    @pl.when(t == 0)
    def _():
        carry_ref[...] = jnp.zeros_like(carry_ref)

    fl = jnp.dot(h_ref[0], w_ref[...], preferred_element_type=F32)
    if norm_width:
        fl = fl * _row_rsqrt(ss_ref, norm_width)
    fl = fl + b_ref[...]
    c = (jnp.minimum(fl, 0.0) - jnp.log1p(jnp.exp(-jnp.abs(fl)))) * LOG2E
    row = lax.broadcasted_iota(jnp.int32, c.shape, 0)
    sh = 1
    while sh < tt:
        c = c + jnp.where(row >= sh, pltpu.roll(c, sh, axis=0), 0.0)
        sh *= 2
    c = c + carry_ref[...]
    o_ref[0] = c
    carry_ref[...] = c[tt - 1:tt, :]


def _fox_gates(h, wf, bf, row_ss=None, tt=512):
    B, T, D = h.shape
    tt = min(tt, T)
    ss_specs = [] if row_ss is None else [pl.BlockSpec((row_ss.shape[0], None, tt, 1), lambda b, t: (0, b, t, 0))]
    ss_args = [] if row_ss is None else [row_ss]
    est = 2 * (_nbytes((tt, D), BF16) + _nbytes((D, LANES), BF16) + _nbytes((tt, LANES), F32)) \
        + 8 * _nbytes((tt, LANES), F32)
    return pl.pallas_call(
        functools.partial(_fgate_body, tt=tt, norm_width=0 if row_ss is None else D),
        grid=(B, T // tt),
        in_specs=[pl.BlockSpec((1, tt, D), lambda b, t: (b, t, 0)),
                  pl.BlockSpec((D, LANES), lambda b, t: (0, 0)),
                  pl.BlockSpec((1, LANES), lambda b, t: (0, 0))] + ss_specs,
        out_specs=pl.BlockSpec((1, tt, LANES), lambda b, t: (b, t, 0)),
        out_shape=jax.ShapeDtypeStruct((B, T, LANES), F32),
        scratch_shapes=[pltpu.VMEM((1, LANES), F32)],
        compiler_params=_params(("parallel", "arbitrary"), est),
        name="fox_gates",
    )(h, wf, bf, *ss_args)


def _fox_body(q_ref, k_ref, v_ref, f_ref, fr_ref, o_ref, vt_ref, fk_ref, p_ref, acc_ref, *, tq, seq, group):
    hg = pl.program_id(1)
    i = pl.program_id(2)
    reps = tq // LANES
    heads = range(group)

    def head_cols(g):
        return slice(g * FOX_HEAD, (g + 1) * FOX_HEAD)

    @pl.when(i == 0)
    def _():
        for c in range(seq // tq):
            rows = slice(c * tq, (c + 1) * tq)
            fblk = f_ref[0, rows, :]
            lane = lax.broadcasted_iota(jnp.int32, fblk.shape, 1)
            for g in heads:
                vt_ref[g, :, rows] = v_ref[0, rows, head_cols(g)].T
                col = jnp.sum(jnp.where(lane == hg * group + g, fblk, 0.0), axis=-1, keepdims=True)
                fk_ref[g, rows, :] = jnp.broadcast_to(col, fblk.shape)

    qt = [q_ref[0, :, head_cols(g)].T for g in heads]
    fq = [fr_ref[0, g, i] for g in heads]

    def scores(g, j):
        start = pl.multiple_of(j * tq, tq)
        return jnp.dot(k_ref[0, pl.ds(start, tq), head_cols(g)], qt[g], preferred_element_type=F32)

    def softmax_block(g, j, st, m_prev, l_prev, diagonal):
        start = pl.multiple_of(j * tq, tq)
        t = st - jnp.concatenate([fk_ref[g, pl.ds(start, tq), :]] * reps, axis=1)
        if diagonal:
            key = lax.broadcasted_iota(jnp.int32, (tq, tq), 0)
            qry = lax.broadcasted_iota(jnp.int32, (tq, tq), 1)
            t = jnp.where(key <= qry, t, -jnp.inf)
        m_new = jnp.maximum(m_prev, jnp.max(t, axis=0, keepdims=True) + fq[g])
        alpha = jnp.exp2(m_prev - m_new)
        p = jnp.exp2(t - (m_new - fq[g]))
        l_new = alpha * l_prev + jnp.sum(p, axis=0, keepdims=True)
        return p.astype(BF16), m_new, l_new, alpha

    def weighted_values(g, j, p):
        start = pl.multiple_of(j * tq, tq)
        return jnp.dot(vt_ref[g, :, pl.ds(start, tq)], p, preferred_element_type=F32)

    def block(j, j_prev, stats, first):
        out = []
        for g in heads:
            m, l = stats[g]
            st = scores(g, j)
            if first:
                p, m, l, _ = softmax_block(g, j, st, m, l, True)
                acc_ref[g] = jnp.zeros((FOX_HEAD, tq), F32)
            else:
                pv_prev = weighted_values(g, j_prev, p_ref[g])
                p, m, l, alpha = softmax_block(g, j, st, m, l, False)
                acc_ref[g] = (acc_ref[g] + pv_prev) * alpha
            p_ref[g] = p
            out.append((m, l))
        return tuple(out)

    stats = tuple((jnp.full((1, tq), -jnp.inf, F32), jnp.zeros((1, tq), F32)) for _ in heads)
    stats = block(i, i, stats, True)
    odd = i % 2
    stats = lax.cond(odd == 1, lambda c: block(0, i, c, False), lambda c: c, stats)

    def pair(jj, c):
        j0 = odd + 2 * jj
        c = block(j0, jnp.where(j0 == 0, i, j0 - 1), c, False)
        return block(j0 + 1, j0, c, False)

    stats = lax.fori_loop(0, i // 2, pair, stats)
    j_last = jnp.where(i == 0, i, i - 1)
    for g in heads:
        acc = acc_ref[g] + weighted_values(g, j_last, p_ref[g])
        o_ref[0, :, head_cols(g)] = (acc / stats[g][1]).astype(o_ref.dtype).T


def _fox_attention(qkv, f, frow, heads, tq, group=FOX_GROUP):
    B, T, _ = qkv.shape
    nq = T // tq
    gw = group * FOX_HEAD
    ngrp = heads // group
    est = 2 * (2 * _nbytes((T, gw), BF16) + 2 * _nbytes((tq, gw), BF16)
               + _nbytes((T, LANES), F32) + group * _nbytes((nq, 8, tq), F32)) \
        + group * (_nbytes((T, FOX_HEAD), BF16) + _nbytes((T, LANES), F32) + 5 * _nbytes((tq, tq), F32))
    return pl.pallas_call(
        functools.partial(_fox_body, tq=tq, seq=T, group=group),
        grid=(B, ngrp, nq),
        in_specs=[pl.BlockSpec((1, tq, gw), lambda b, h, i: (b, i, h)),
                  pl.BlockSpec((1, T, gw), lambda b, h, i: (b, 0, ngrp + h)),
                  pl.BlockSpec((1, T, gw), lambda b, h, i: (b, 0, 2 * ngrp + h)),
                  pl.BlockSpec((1, T, LANES), lambda b, h, i: (b, 0, 0)),
                  pl.BlockSpec((1, group, nq, 1, tq), lambda b, h, i: (b, h, 0, 0, 0))],
        out_specs=pl.BlockSpec((1, tq, gw), lambda b, h, i: (b, i, h)),
        out_shape=jax.ShapeDtypeStruct((B, T, heads * FOX_HEAD), BF16),
        scratch_shapes=[pltpu.VMEM((group, FOX_HEAD, T), BF16),
                        pltpu.VMEM((group, T, LANES), F32),
                        pltpu.VMEM((group, tq, tq), BF16),
                        pltpu.VMEM((group, FOX_HEAD, tq), F32)],
        compiler_params=_params(("parallel", "parallel", "arbitrary"), est),
        name="fox_attention",
    )(qkv, qkv, qkv, f, frow)


def _xattn_out_body(q_ref, kv_ref, w_ref, r_ref, g_ref, o_ref, h_ref, a_ref, *, d_model):
    hd = d_model // XA_HEADS
    for hh in range(XA_HEADS):
        q = q_ref[:, hh * hd:(hh + 1) * hd]
        k = kv_ref[:, hh * hd:(hh + 1) * hd]
        v = kv_ref[:, d_model + hh * hd:d_model + (hh + 1) * hd]
        s = lax.dot_general(q, k, (((1,), (1,)), ((), ())), preferred_element_type=F32)
        p = jnp.exp(s - jnp.max(s, axis=-1, keepdims=True))
        p = p / jnp.sum(p, axis=-1, keepdims=True)
        a_ref[:, hh * hd:(hh + 1) * hd] = jnp.dot(p.astype(BF16), v, preferred_element_type=F32).astype(a_ref.dtype)
    acc = r_ref[...] + jnp.dot(a_ref[...], w_ref[...], preferred_element_type=F32)
    o_ref[...] = acc
    ms = jnp.mean(acc * acc, axis=-1, keepdims=True)
    h_ref[...] = (acc * lax.rsqrt(ms + EPS) * g_ref[...]).astype(h_ref.dtype)


def _xattn_out_norm(q, kv, w, layer, r, gain, seq, tm=512):
    M, D = q.shape
    n_mem = kv.shape[1]
    tm = _pick_tile(seq, tm)
    per_batch = seq // tm
    est = 2 * (_nbytes((tm, D), BF16) + _nbytes((n_mem, 2 * D), BF16) + _nbytes((D, D), BF16)
               + 2 * _nbytes((tm, D), F32) + _nbytes((tm, D), BF16)) \
        + _nbytes((tm, D), BF16) + 3 * _nbytes((tm, D), F32)
    row = pl.BlockSpec((tm, D), lambda i: (i, 0))
    return pl.pallas_call(
        functools.partial(_xattn_out_body, d_model=D),
        grid=(M // tm,),
        in_specs=[row,
                  pl.BlockSpec((None, n_mem, 2 * D), lambda i: (i // per_batch, 0, 0)),
                  pl.BlockSpec((None, D, D), lambda i: (layer, 0, 0)),
                  row,
                  pl.BlockSpec((1, D), lambda i: (0, 0))],
        out_specs=[row, row],
        out_shape=[jax.ShapeDtypeStruct((M, D), F32),
                   jax.ShapeDtypeStruct((M, D), BF16)],
        scratch_shapes=[pltpu.VMEM((tm, D), BF16)],
        compiler_params=_params(("parallel",), est),
        name="xattn_out_norm",
    )(q, kv, w, r, gain.reshape(1, D).astype(F32))


def kernel(x, mem, lb_table, ev_norm, ev_w_in, ev_w_pool, ev_pool_scale, ev_hg_norm, ev_w_out,
           od_norm, od_w_in, od_b_f, od_w_out, xa_norm, xa_mem_norm, xa_wq, xa_wkv, xa_wo,
           ffn_norm, ffn_w_gate, ffn_w_up, ffn_w_down, final_norm):
    B, T, D = x.shape
    M = B * T
    depth = xa_norm.shape[0]
    mix_a = ev_pool_scale.shape[1]
    mix_b = lb_table.shape[1]
    fox_heads = od_b_f.shape[1]
    n_mem = mem.shape[1]
    xa_scale = (D // XA_HEADS) ** -0.5
    fox_tq = _pick_tile(T, 512)

    xs = x.reshape(M, D)
    mem2 = mem.reshape(B * n_mem, D)
    w_down = ffn_w_down.astype(BF16)
    ev_wo, od_wo, xa_wo_b = ev_w_out.astype(BF16), od_w_out.astype(BF16), xa_wo.astype(BF16)
    pre = None
    for l in range(depth):
        if pre is None:
            h, ss = _rmsnorm(xs, ev_norm[l // 2] if l % 2 == 0 else od_norm[l // 2], BF16), None
        else:
            h, ss = pre
        if l % 2 == 0:
            e = l // 2
            z = _matmul(h, ev_w_in, e, ev_w_in.shape[2], BF16, row_ss=ss).reshape(B, T, -1)
            ya = _pool_mixer(z, ev_w_pool[e].astype(BF16), ev_pool_scale[e], mix_a)
            yb = _hgrn_mixer(z, lb_table, ev_hg_norm[e], l, mix_a, mix_b)
            xs, h = _matmul_residual_norm([ya.reshape(M, mix_a), yb.reshape(M, mix_b)], ev_wo, e, xs, xa_norm[l])
        else:
            o = l // 2
            qscale = jnp.concatenate([jnp.full((D,), LOG2E * FOX_HEAD ** -0.5, F32), jnp.ones((2 * D,), F32)])
            w_in = od_w_in.astype(BF16)
            qkv = _matmul(h, w_in, o, 3 * D, BF16, colscale=qscale, row_ss=ss).reshape(B, T, 3 * D)
            wf = jnp.zeros((D, LANES), BF16).at[:, :fox_heads].set(w_in[o, :, 3 * D:])
            bf = jnp.zeros((1, LANES), F32).at[0, :fox_heads].set(od_b_f[o].astype(F32))
            ss4 = None if ss is None else ss.reshape(ss.shape[0], B, T, 1)
            f = _fox_gates(h.reshape(B, T, D), wf, bf, ss4)
            frow = f[..., :fox_heads].transpose(0, 2, 1).reshape(B, fox_heads, T // fox_tq, 1, fox_tq)
            y = _fox_attention(qkv, f, frow, fox_heads, fox_tq).reshape(M, D)
            xs, h = _matmul_residual_norm([y], od_wo, o, xs, xa_norm[l])

        mn = _rmsnorm(mem2, xa_mem_norm[l], BF16)
        q = _matmul(h, xa_wq, l, D, BF16, colscale=jnp.full((D,), xa_scale, F32))
        kv = _matmul(mn, xa_wkv, l, 2 * D, BF16)
        xs, h = _xattn_out_norm(q, kv.reshape(B, n_mem, 2 * D), xa_wo_b, l, xs, ffn_norm[l], T)

        act = _matmul_swiglu(h, ffn_w_gate, ffn_w_up, l)
        if l + 1 < depth:
            nxt = l + 1
            gain = ev_norm[nxt // 2] if nxt % 2 == 0 else od_norm[nxt // 2]
            xs, hp, ss = _matmul_residual(act, w_down, l, xs, gain=gain)
            pre = (hp, ss)
        else:
            xs = _matmul_residual(act, w_down, l, xs)
    return _rmsnorm(xs, final_norm, x.dtype).reshape(B, T, D)
```

```python
import functools

import jax
import jax.numpy as jnp
from jax import lax
from jax.experimental import pallas as pl
from jax.experimental.pallas import tpu as pltpu

F32 = jnp.float32
BF16 = jnp.bfloat16

EPS = 1e-6
POOL_WINDOWS = (2, 4, 8, 16)
POOL_HALO = 16
HG_HEAD = 128
HG_CHUNK = 64
HG_DIAG = 8
HG_SUB = 128
FOX_HEAD = 128
FOX_GROUP = 4
XA_HEADS = 4

LOG2E = 1.4426950408889634
LANES = 128
VMEM_CAP = 56 * 1024 * 1024
VMEM_FLOOR = 32 * 1024 * 1024


def _params(semantics, vmem_estimate):
    limit = int(min(max(vmem_estimate * 5 // 4, VMEM_FLOOR), VMEM_CAP))
    return pltpu.CompilerParams(dimension_semantics=semantics, vmem_limit_bytes=limit)


def _nbytes(shape, dtype):
    n = jnp.dtype(dtype).itemsize
    for s in shape:
        n *= s
    return n


def _rmsnorm_body(x_ref, g_ref, o_ref):
    x = x_ref[...]
    ms = jnp.mean(x * x, axis=-1, keepdims=True)
    o_ref[...] = (x * lax.rsqrt(ms + EPS) * g_ref[...]).astype(o_ref.dtype)


def _rmsnorm(x, g, out_dtype, tm=512):
    M, D = x.shape
    tm = min(tm, M)
    est = 2 * (_nbytes((tm, D), x.dtype) + _nbytes((tm, D), out_dtype)) + 3 * _nbytes((tm, D), F32)
    return pl.pallas_call(
        _rmsnorm_body,
        grid=(M // tm,),
        in_specs=[pl.BlockSpec((tm, D), lambda i: (i, 0)),
                  pl.BlockSpec((1, D), lambda i: (0, 0))],
        out_specs=pl.BlockSpec((tm, D), lambda i: (i, 0)),
        out_shape=jax.ShapeDtypeStruct((M, D), out_dtype),
        compiler_params=_params(("parallel",), est),
        name="rmsnorm",
    )(x, g.reshape(1, D).astype(F32))


def _with_bf16_weights(w_refs, wb_refs, compute):
    if not wb_refs:
        compute(*(w[...] for w in w_refs))
        return
    i = pl.program_id(1)

    @pl.when(i == 0)
    def _():
        for w, wb in zip(w_refs, wb_refs):
            wb[...] = w[...].astype(BF16)

    @pl.when(i > 0)
    def _():
        compute(*(wb[...] for wb in wb_refs))


def _row_rsqrt(ss_ref, width):
    return lax.rsqrt(jnp.sum(ss_ref[...], axis=0) * (1.0 / width) + EPS)


def _mm_body(x_ref, w_ref, cs_ref, *rest, norm_width):
    if norm_width:
        ss_ref, o_ref, *wb = rest
    else:
        o_ref, *wb = rest

    def compute(w):
        acc = jnp.dot(x_ref[...], w, preferred_element_type=F32)
        if norm_width:
            acc = acc * _row_rsqrt(ss_ref, norm_width)
        o_ref[...] = (acc * cs_ref[...]).astype(o_ref.dtype)

    _with_bf16_weights((w_ref,), wb, compute)


def _mm_res_body(x_ref, w_ref, r_ref, *rest, with_gain):
    if with_gain:
        g_ref, o_ref, hp_ref, ss_ref, *wb = rest
    else:
        o_ref, *wb = rest

    def compute(w):
        acc = r_ref[...] + jnp.dot(x_ref[...], w, preferred_element_type=F32)
        o_ref[...] = acc
        if with_gain:
            hp_ref[...] = (acc * g_ref[...]).astype(hp_ref.dtype)
            ss_ref[...] = jnp.sum(acc * acc, axis=-1, keepdims=True)

    _with_bf16_weights((w_ref,), wb, compute)


def _mm_swiglu_body(x_ref, wg_ref, wu_ref, o_ref, *wb):
    def compute(wg, wu):
        x = x_ref[...]
        g = jnp.dot(x, wg, preferred_element_type=F32)
        u = jnp.dot(x, wu, preferred_element_type=F32)
        o_ref[...] = (g * jax.nn.sigmoid(g) * u).astype(o_ref.dtype)

    _with_bf16_weights((wg_ref, wu_ref), wb, compute)


def _pick_tile(n, pref):
    t = min(pref, n)
    while n % t:
        t //= 2
    return t


def _weight_spec(layer, K, tn):
    return pl.BlockSpec((None, K, tn), lambda j, i: (layer, 0, j))


def _weight_scratch(w, K, tn):
    return [] if w.dtype == BF16 else [pltpu.VMEM((K, tn), BF16)]


def _weight_bytes(w, K, tn):
    return 2 * _nbytes((K, tn), w.dtype) + (0 if w.dtype == BF16 else _nbytes((K, tn), BF16))


def _row_steps(w, M, tm):
    extra = 0 if w.dtype == BF16 else 1
    return M // tm + extra, lambda i: jnp.maximum(i - extra, 0)


def _matmul(x, w, layer, n_cols, out_dtype, colscale=None, row_ss=None, tm=1024, tn=1024):
    M, K = x.shape
    tm, tn = _pick_tile(M, tm), _pick_tile(n_cols, tn)
    if colscale is None:
        colscale = jnp.ones((n_cols,), F32)
    steps, row = _row_steps(w, M, tm)
    ss_specs = [] if row_ss is None else [pl.BlockSpec((row_ss.shape[0], tm, 1), lambda j, i: (0, row(i), 0))]
    ss_args = [] if row_ss is None else [row_ss]
    est = 2 * (_nbytes((tm, K), x.dtype) + _nbytes((tm, tn), out_dtype)) + _weight_bytes(w, K, tn) \
        + 2 * _nbytes((tm, tn), F32)
    return pl.pallas_call(
        functools.partial(_mm_body, norm_width=0 if row_ss is None else K),
        grid=(n_cols // tn, steps),
        in_specs=[pl.BlockSpec((tm, K), lambda j, i: (row(i), 0)),
                  _weight_spec(layer, K, tn),
                  pl.BlockSpec((1, tn), lambda j, i: (0, j))] + ss_specs,
        out_specs=pl.BlockSpec((tm, tn), lambda j, i: (row(i), j)),
        out_shape=jax.ShapeDtypeStruct((M, n_cols), out_dtype),
        scratch_shapes=_weight_scratch(w, K, tn),
        compiler_params=_params(("parallel", "arbitrary"), est),
        name="matmul",
    )(x, w, colscale.reshape(1, n_cols), *ss_args)


def _matmul_residual(x, w, layer, r, gain=None, tn=1024):
    M, K = x.shape
    N = w.shape[2]
    tm = 1024 if K <= 2048 else 512
    tm, tn = _pick_tile(M, tm), _pick_tile(N, tn)
    steps, row = _row_steps(w, M, tm)
    est = 2 * (_nbytes((tm, K), x.dtype) + 2 * _nbytes((tm, tn), F32)) + _weight_bytes(w, K, tn) \
        + 2 * _nbytes((tm, tn), F32)
    tile = pl.BlockSpec((tm, tn), lambda j, i: (row(i), j))
    in_specs = [pl.BlockSpec((tm, K), lambda j, i: (row(i), 0)), _weight_spec(layer, K, tn), tile]
    out_specs, out_shape, args = tile, jax.ShapeDtypeStruct((M, N), F32), [x, w, r]
    if gain is not None:
        in_specs.append(pl.BlockSpec((1, tn), lambda j, i: (0, j)))
        args.append(gain.reshape(1, N).astype(F32))
        out_specs = [tile, tile, pl.BlockSpec((None, tm, 1), lambda j, i: (j, row(i), 0))]
        out_shape = [out_shape, jax.ShapeDtypeStruct((M, N), BF16), jax.ShapeDtypeStruct((N // tn, M, 1), F32)]
    return pl.pallas_call(
        functools.partial(_mm_res_body, with_gain=gain is not None),
        grid=(N // tn, steps),
        in_specs=in_specs,
        out_specs=out_specs,
        out_shape=out_shape,
        scratch_shapes=_weight_scratch(w, K, tn),
        compiler_params=_params(("parallel", "arbitrary"), est),
        name="matmul_residual",
    )(*args)


def _matmul_swiglu(x, wg, wu, layer, tm=1024, tn=512):
    M, K = x.shape
    N = wg.shape[2]
    tm, tn = _pick_tile(M, tm), _pick_tile(N, tn)
    steps, row = _row_steps(wg, M, tm)
    est = 2 * (_nbytes((tm, K), x.dtype) + _nbytes((tm, tn), BF16)) + 2 * _weight_bytes(wg, K, tn) \
        + 4 * _nbytes((tm, tn), F32)
    return pl.pallas_call(
        _mm_swiglu_body,
        grid=(N // tn, steps),
        in_specs=[pl.BlockSpec((tm, K), lambda j, i: (row(i), 0)),
                  _weight_spec(layer, K, tn),
                  _weight_spec(layer, K, tn)],
        out_specs=pl.BlockSpec((tm, tn), lambda j, i: (row(i), j)),
        out_shape=jax.ShapeDtypeStruct((M, N), BF16),
        scratch_shapes=_weight_scratch(wg, K, tn) + _weight_scratch(wu, K, tn),
        compiler_params=_params(("parallel", "arbitrary"), est),
        name="matmul_swiglu",
    )(x, wg, wu)


def _mm_res_norm_body(*refs, n_parts):
    x_refs = refs[:n_parts]
    w_ref, r_ref, g_ref, o_ref, h_ref = refs[n_parts:]
    acc = r_ref[...]
    k0 = 0
    for x_ref in x_refs:
        kw = x_ref.shape[1]
        acc = acc + jnp.dot(x_ref[...], w_ref[k0:k0 + kw, :], preferred_element_type=F32)
        k0 += kw
    o_ref[...] = acc
    ms = jnp.mean(acc * acc, axis=-1, keepdims=True)
    h_ref[...] = (acc * lax.rsqrt(ms + EPS) * g_ref[...]).astype(h_ref.dtype)


def _matmul_residual_norm(x_parts, w, layer, r, gain, tm=512):
    M, N = r.shape
    K = w.shape[1]
    tm = _pick_tile(M, tm)
    est = 2 * (_nbytes((tm, K), BF16) + _nbytes((K, N), w.dtype) + 2 * _nbytes((tm, N), F32) + _nbytes((tm, N), BF16)) \
        + 3 * _nbytes((tm, N), F32)
    return pl.pallas_call(
        functools.partial(_mm_res_norm_body, n_parts=len(x_parts)),
        grid=(M // tm,),
        in_specs=[pl.BlockSpec((tm, xp.shape[1]), lambda i: (i, 0)) for xp in x_parts]
        + [pl.BlockSpec((None, K, N), lambda i: (layer, 0, 0)),
           pl.BlockSpec((tm, N), lambda i: (i, 0)),
           pl.BlockSpec((1, N), lambda i: (0, 0))],
        out_specs=[pl.BlockSpec((tm, N), lambda i: (i, 0)),
                   pl.BlockSpec((tm, N), lambda i: (i, 0))],
        out_shape=[jax.ShapeDtypeStruct((M, N), F32),
                   jax.ShapeDtypeStruct((M, N), BF16)],
        compiler_params=_params(("parallel",), est),
        name="matmul_residual_norm",
    )(*x_parts, w, r, gain.reshape(1, N).astype(F32))


def _pool_body(u_ref, halo_ref, w_ref, sc_ref, o_ref, ext_ref, *, tt, group):
    t = pl.program_id(1)
    u = u_ref[0].astype(F32)
    ext_ref[0:POOL_HALO, :] = jnp.where(t > 0, halo_ref[0].astype(F32), 0.0)
    ext_ref[POOL_HALO:POOL_HALO + tt, :] = u
    pos = t * tt + lax.broadcasted_iota(jnp.int32, (tt, 1), 0)
    for gi, win in enumerate(POOL_WINDOWS):
        c0, c1 = gi * group, (gi + 1) * group
        tok = u[:, c0:c1]
        acc = tok
        for d in range(1, win):
            acc = acc + ext_ref[POOL_HALO - d:POOL_HALO - d + tt, c0:c1]
        cnt = jnp.minimum(pos + 1, win).astype(F32)
        p = acc / cnt - tok
        y = jnp.dot(p.astype(BF16), w_ref[gi], preferred_element_type=F32)
        o_ref[0, :, c0:c1] = (y * sc_ref[:, c0:c1]).astype(o_ref.dtype)


def _pool_mixer(z, w_pool, scale, mix_a, tt=1024):
    B, T, _ = z.shape
    tt = min(tt, T)
    group = mix_a // len(POOL_WINDOWS)
    halo_blocks = tt // POOL_HALO
    est = 2 * (_nbytes((tt, mix_a), F32) + _nbytes((tt, mix_a), BF16)) + 4 * _nbytes((tt, mix_a), F32)
    return pl.pallas_call(
        functools.partial(_pool_body, tt=tt, group=group),
        grid=(B, T // tt),
        in_specs=[pl.BlockSpec((1, tt, mix_a), lambda b, t: (b, t, 0)),
                  pl.BlockSpec((1, POOL_HALO, mix_a),
                               lambda b, t: (b, jnp.maximum(t * halo_blocks - 1, 0), 0)),
                  pl.BlockSpec((len(POOL_WINDOWS), group, group), lambda b, t: (0, 0, 0)),
                  pl.BlockSpec((1, mix_a), lambda b, t: (0, 0))],
        out_specs=pl.BlockSpec((1, tt, mix_a), lambda b, t: (b, t, 0)),
        out_shape=jax.ShapeDtypeStruct((B, T, mix_a), BF16),
        scratch_shapes=[pltpu.VMEM((tt + POOL_HALO, mix_a), F32)],
        compiler_params=_params(("parallel", "parallel"), est),
        name="pool_mixer",
    )(z, z, w_pool, scale.reshape(1, mix_a).astype(F32))


def _hgrn_subtile(qf, kk, v, logf, st):
    n = qf.shape[0]
    C, R = HG_CHUNK, HG_DIAG
    row = lax.broadcasted_iota(jnp.int32, (n, HG_HEAD), 0)

    pos = row & (C - 1)
    b = logf
    sh = 1
    while sh < C:
        b = b + jnp.where(pos >= sh, pltpu.roll(b, sh, axis=0), 0.0)
        sh *= 2

    nb = n // R
    b3, q3, c3, v3 = (a.reshape(nb, R, HG_HEAD) for a in (b, qf, b - jnp.log2(kk), v))
    tpos = lax.broadcasted_iota(jnp.int32, (1, R, 1), 1)
    od = jnp.zeros((nb, R, HG_HEAD), F32)
    for s in range(R):
        w = q3 * jnp.exp2(b3 - c3[:, s:s + 1, :])
        a = jnp.sum(w, axis=-1, keepdims=True)
        a = jnp.where(tpos >= s, a, 0.0)
        od = od + a * v3[:, s:s + 1, :]
    o = od.reshape(n, HG_HEAD)

    same = lax.broadcasted_iota(jnp.int32, (n, n), 0) ^ lax.broadcasted_iota(jnp.int32, (n, n), 1)
    a_off = jnp.zeros((n, n), F32)
    h = R
    while h < C:
        g = 2 * h
        ref = b.reshape(n // g, g, HG_HEAD)[:, h - 1:h, :]
        ref = jnp.broadcast_to(ref, (n // g, g, HG_HEAD)).reshape(n, HG_HEAD)
        e = jnp.exp2(-jnp.abs(b - ref))
        right = (row & (g - 1)) >= h
        qt = jnp.where(right, qf * e, 0.0).astype(BF16)
        kt = jnp.where(right, 0.0, kk * e).astype(BF16)
        a = lax.dot_general(qt, kt, (((1,), (1,)), ((), ())), preferred_element_type=F32)
        a_off = a_off + jnp.where(same < g, a, 0.0)
        h = g
    v_bf = v.astype(BF16)
    o = o + jnp.dot(a_off.astype(BF16), v_bf, preferred_element_type=F32)

    nc = n // C
    bc = b.reshape(nc, C, HG_HEAD)
    b_last = bc[:, C - 1:C, :]
    q_in = (qf * jnp.exp2(b)).astype(BF16)
    k_out = (kk.reshape(nc, C, HG_HEAD) * jnp.exp2(b_last - bc)).astype(BF16)
    dec = jnp.exp2(b_last)
    pieces = []
    for c in range(nc):
        rows = slice(c * C, (c + 1) * C)
        o_int = lax.dot_general(q_in[rows], st.astype(BF16), (((1,), (1,)), ((), ())),
                                preferred_element_type=F32)
        pieces.append(o[rows] + o_int)
        upd = lax.dot_general(v_bf[rows], k_out[c], (((0,), (0,)), ((), ())),
                              preferred_element_type=F32)
        st = st * dec[c] + upd
    return jnp.concatenate(pieces, axis=0), st


def _hgrn_body(lbt_ref, ng_ref, q_ref, f_ref, i_ref, g_ref, o_ref, st_ref, *, layer, tt, sub):
    t = pl.program_id(2)

    @pl.when(t == 0)
    def _():
        st_ref[...] = jnp.zeros_like(st_ref)

    lbt = lbt_ref[...]
    e = jnp.exp(lbt - jnp.max(lbt, axis=0, keepdims=True))
    sm = e / jnp.sum(e, axis=0, keepdims=True)
    lb = jnp.sum(sm[1:layer + 2], axis=0, keepdims=True)

    st = st_ref[...]
    for c in range(tt // sub):
        rows = slice(c * sub, (c + 1) * sub)
        f = lb + (1.0 - lb) * jax.nn.sigmoid(f_ref[0, rows, :].astype(F32))
        q = q_ref[0, rows, :].astype(F32)
        qf = q * jax.nn.sigmoid(q) * (HG_HEAD ** -0.5)
        o, st = _hgrn_subtile(qf, 1.0 - f, i_ref[0, rows, :].astype(F32), jnp.log2(f), st)
        ms = jnp.mean(o * o, axis=-1, keepdims=True)
        o = o * lax.rsqrt(ms + EPS) * ng_ref[...]
        g = g_ref[0, rows, :].astype(F32)
        o_ref[0, rows, :] = (o * (g * jax.nn.sigmoid(g))).astype(o_ref.dtype)
    st_ref[...] = st


def _hgrn_mixer(z, lb_table, norm_g, layer, mix_a, mix_b, tt=1024):
    B, T, _ = z.shape
    tt = min(tt, T)
    sub = min(HG_SUB, tt)
    heads = mix_b // HG_HEAD
    c0 = mix_a // HG_HEAD
    est = 2 * 5 * _nbytes((tt, HG_HEAD), F32) + 32 * _nbytes((sub, HG_HEAD), F32) + 6 * _nbytes((sub, sub), F32)

    def col(off):
        return pl.BlockSpec((1, tt, HG_HEAD), lambda b, h, t, off=off: (b, t, c0 + off * heads + h))

    return pl.pallas_call(
        functools.partial(_hgrn_body, layer=layer, tt=tt, sub=sub),
        grid=(B, heads, T // tt),
        in_specs=[pl.BlockSpec((lb_table.shape[0], HG_HEAD), lambda b, h, t: (0, h)),
                  pl.BlockSpec((1, HG_HEAD), lambda b, h, t: (0, 0)),
                  col(0), col(1), col(2), col(3)],
        out_specs=pl.BlockSpec((1, tt, HG_HEAD), lambda b, h, t: (b, t, h)),
        out_shape=jax.ShapeDtypeStruct((B, T, mix_b), BF16),
        scratch_shapes=[pltpu.VMEM((HG_HEAD, HG_HEAD), F32)],
        compiler_params=_params(("parallel", "parallel", "arbitrary"), est),
        name="hgrn2_mixer",
    )(lb_table.astype(F32), norm_g.reshape(1, HG_HEAD).astype(F32), z, z, z, z)


def _fgate_body(h_ref, w_ref, b_ref, *rest, tt, norm_width):
    if norm_width:
        ss_ref, o_ref, carry_ref = rest
    else:
        o_ref, carry_ref = rest
    t = pl.program_id(1)

    @pl.when(t == 0)
    def _():
        carry_ref[...] = jnp.zeros_like(carry_ref)

    fl = jnp.dot(h_ref[0], w_ref[...], preferred_element_type=F32)
    if norm_width:
        fl = fl * _row_rsqrt(ss_ref, norm_width)
    fl = fl + b_ref[...]
    c = (jnp.minimum(fl, 0.0) - jnp.log1p(jnp.exp(-jnp.abs(fl)))) * LOG2E
    row = lax.broadcasted_iota(jnp.int32, c.shape, 0)
    sh = 1
    while sh < tt:
        c = c + jnp.where(row >= sh, pltpu.roll(c, sh, axis=0), 0.0)
        sh *= 2
    c = c + carry_ref[...]
    o_ref[0] = c
    carry_ref[...] = c[tt - 1:tt, :]


def _fox_gates(h, wf, bf, row_ss=None, tt=512):
    B, T, D = h.shape
    tt = min(tt, T)
    ss_specs = [] if row_ss is None else [pl.BlockSpec((row_ss.shape[0], None, tt, 1), lambda b, t: (0, b, t, 0))]
    ss_args = [] if row_ss is None else [row_ss]
    est = 2 * (_nbytes((tt, D), BF16) + _nbytes((D, LANES), BF16) + _nbytes((tt, LANES), F32)) \
        + 8 * _nbytes((tt, LANES), F32)
    return pl.pallas_call(
        functools.partial(_fgate_body, tt=tt, norm_width=0 if row_ss is None else D),
        grid=(B, T // tt),
        in_specs=[pl.BlockSpec((1, tt, D), lambda b, t: (b, t, 0)),
                  pl.BlockSpec((D, LANES), lambda b, t: (0, 0)),
                  pl.BlockSpec((1, LANES), lambda b, t: (0, 0))] + ss_specs,
        out_specs=pl.BlockSpec((1, tt, LANES), lambda b, t: (b, t, 0)),
        out_shape=jax.ShapeDtypeStruct((B, T, LANES), F32),
        scratch_shapes=[pltpu.VMEM((1, LANES), F32)],
        compiler_params=_params(("parallel", "arbitrary"), est),
        name="fox_gates",
    )(h, wf, bf, *ss_args)


def _fox_body(q_ref, k_ref, v_ref, f_ref, fr_ref, o_ref, vt_ref, fk_ref, p_ref, acc_ref, *, tq, seq, group):
    hg = pl.program_id(1)
    i = pl.program_id(2)
    reps = tq // LANES
    heads = range(group)

    def head_cols(g):
        return slice(g * FOX_HEAD, (g + 1) * FOX_HEAD)

    @pl.when(i == 0)
    def _():
        for c in range(seq // tq):
            rows = slice(c * tq, (c + 1) * tq)
            fblk = f_ref[0, rows, :]
            lane = lax.broadcasted_iota(jnp.int32, fblk.shape, 1)
            for g in heads:
                vt_ref[g, :, rows] = v_ref[0, rows, head_cols(g)].T
                col = jnp.sum(jnp.where(lane == hg * group + g, fblk, 0.0), axis=-1, keepdims=True)
                fk_ref[g, rows, :] = jnp.broadcast_to(col, fblk.shape)

    qt = [q_ref[0, :, head_cols(g)].T for g in heads]
    fq = [fr_ref[0, g, i] for g in heads]

    def scores(g, j):
        start = pl.multiple_of(j * tq, tq)
        return jnp.dot(k_ref[0, pl.ds(start, tq), head_cols(g)], qt[g], preferred_element_type=F32)

    def softmax_block(g, j, st, m_prev, l_prev, diagonal):
        start = pl.multiple_of(j * tq, tq)
        t = st - jnp.concatenate([fk_ref[g, pl.ds(start, tq), :]] * reps, axis=1)
        if diagonal:
            key = lax.broadcasted_iota(jnp.int32, (tq, tq), 0)
            qry = lax.broadcasted_iota(jnp.int32, (tq, tq), 1)
            t = jnp.where(key <= qry, t, -jnp.inf)
        m_new = jnp.maximum(m_prev, jnp.max(t, axis=0, keepdims=True) + fq[g])
        alpha = jnp.exp2(m_prev - m_new)
        p = jnp.exp2(t - (m_new - fq[g]))
        l_new = alpha * l_prev + jnp.sum(p, axis=0, keepdims=True)
        return p.astype(BF16), m_new, l_new, alpha

    def weighted_values(g, j, p):
        start = pl.multiple_of(j * tq, tq)
        return jnp.dot(vt_ref[g, :, pl.ds(start, tq)], p, preferred_element_type=F32)

    def block(j, j_prev, stats, first):
        out = []
        for g in heads:
            m, l = stats[g]
            st = scores(g, j)
            if first:
                p, m, l, _ = softmax_block(g, j, st, m, l, True)
                acc_ref[g] = jnp.zeros((FOX_HEAD, tq), F32)
            else:
                pv_prev = weighted_values(g, j_prev, p_ref[g])
                p, m, l, alpha = softmax_block(g, j, st, m, l, False)
                acc_ref[g] = (acc_ref[g] + pv_prev) * alpha
            p_ref[g] = p
            out.append((m, l))
        return tuple(out)

    stats = tuple((jnp.full((1, tq), -jnp.inf, F32), jnp.zeros((1, tq), F32)) for _ in heads)
    stats = block(i, i, stats, True)
    odd = i % 2
    stats = lax.cond(odd == 1, lambda c: block(0, i, c, False), lambda c: c, stats)

    def pair(jj, c):
        j0 = odd + 2 * jj
        c = block(j0, jnp.where(j0 == 0, i, j0 - 1), c, False)
        return block(j0 + 1, j0, c, False)

    stats = lax.fori_loop(0, i // 2, pair, stats)
    j_last = jnp.where(i == 0, i, i - 1)
    for g in heads:
        acc = acc_ref[g] + weighted_values(g, j_last, p_ref[g])
        o_ref[0, :, head_cols(g)] = (acc / stats[g][1]).astype(o_ref.dtype).T


def _fox_attention(qkv, f, frow, heads, tq, group=FOX_GROUP):
    B, T, _ = qkv.shape
    nq = T // tq
    gw = group * FOX_HEAD
    ngrp = heads // group
    est = 2 * (2 * _nbytes((T, gw), BF16) + 2 * _nbytes((tq, gw), BF16)
               + _nbytes((T, LANES), F32) + group * _nbytes((nq, 8, tq), F32)) \
        + group * (_nbytes((T, FOX_HEAD), BF16) + _nbytes((T, LANES), F32) + 5 * _nbytes((tq, tq), F32))
    return pl.pallas_call(
        functools.partial(_fox_body, tq=tq, seq=T, group=group),
        grid=(B, ngrp, nq),
        in_specs=[pl.BlockSpec((1, tq, gw), lambda b, h, i: (b, i, h)),
                  pl.BlockSpec((1, T, gw), lambda b, h, i: (b, 0, ngrp + h)),
                  pl.BlockSpec((1, T, gw), lambda b, h, i: (b, 0, 2 * ngrp + h)),
                  pl.BlockSpec((1, T, LANES), lambda b, h, i: (b, 0, 0)),
                  pl.BlockSpec((1, group, nq, 1, tq), lambda b, h, i: (b, h, 0, 0, 0))],
        out_specs=pl.BlockSpec((1, tq, gw), lambda b, h, i: (b, i, h)),
        out_shape=jax.ShapeDtypeStruct((B, T, heads * FOX_HEAD), BF16),
        scratch_shapes=[pltpu.VMEM((group, FOX_HEAD, T), BF16),
                        pltpu.VMEM((group, T, LANES), F32),
                        pltpu.VMEM((group, tq, tq), BF16),
                        pltpu.VMEM((group, FOX_HEAD, tq), F32)],
        compiler_params=_params(("parallel", "parallel", "arbitrary"), est),
        name="fox_attention",
    )(qkv, qkv, qkv, f, frow)


def _xattn_out_body(q_ref, kv_ref, w_ref, r_ref, g_ref, o_ref, h_ref, a_ref, *, d_model):
    hd = d_model // XA_HEADS
    for hh in range(XA_HEADS):
        q = q_ref[:, hh * hd:(hh + 1) * hd]
        k = kv_ref[:, hh * hd:(hh + 1) * hd]
        v = kv_ref[:, d_model + hh * hd:d_model + (hh + 1) * hd]
        s = lax.dot_general(q, k, (((1,), (1,)), ((), ())), preferred_element_type=F32)
        p = jnp.exp(s - jnp.max(s, axis=-1, keepdims=True))
        p = p / jnp.sum(p, axis=-1, keepdims=True)
        a_ref[:, hh * hd:(hh + 1) * hd] = jnp.dot(p.astype(BF16), v, preferred_element_type=F32).astype(a_ref.dtype)
    acc = r_ref[...] + jnp.dot(a_ref[...], w_ref[...], preferred_element_type=F32)
    o_ref[...] = acc
    ms = jnp.mean(acc * acc, axis=-1, keepdims=True)
    h_ref[...] = (acc * lax.rsqrt(ms + EPS) * g_ref[...]).astype(h_ref.dtype)


def _xattn_out_norm(q, kv, w, layer, r, gain, seq, tm=512):
    M, D = q.shape
    n_mem = kv.shape[1]
    tm = _pick_tile(seq, tm)
    per_batch = seq // tm
    est = 2 * (_nbytes((tm, D), BF16) + _nbytes((n_mem, 2 * D), BF16) + _nbytes((D, D), BF16)
               + 2 * _nbytes((tm, D), F32) + _nbytes((tm, D), BF16)) \
        + _nbytes((tm, D), BF16) + 3 * _nbytes((tm, D), F32)
    row = pl.BlockSpec((tm, D), lambda i: (i, 0))
    return pl.pallas_call(
        functools.partial(_xattn_out_body, d_model=D),
        grid=(M // tm,),
        in_specs=[row,
                  pl.BlockSpec((None, n_mem, 2 * D), lambda i: (i // per_batch, 0, 0)),
                  pl.BlockSpec((None, D, D), lambda i: (layer, 0, 0)),
                  row,
                  pl.BlockSpec((1, D), lambda i: (0, 0))],
        out_specs=[row, row],
        out_shape=[jax.ShapeDtypeStruct((M, D), F32),
                   jax.ShapeDtypeStruct((M, D), BF16)],
        scratch_shapes=[pltpu.VMEM((tm, D), BF16)],
        compiler_params=_params(("parallel",), est),
        name="xattn_out_norm",
    )(q, kv, w, r, gain.reshape(1, D).astype(F32))


def kernel(x, mem, lb_table, ev_norm, ev_w_in, ev_w_pool, ev_pool_scale, ev_hg_norm, ev_w_out,
           od_norm, od_w_in, od_b_f, od_w_out, xa_norm, xa_mem_norm, xa_wq, xa_wkv, xa_wo,
           ffn_norm, ffn_w_gate, ffn_w_up, ffn_w_down, final_norm):
    B, T, D = x.shape
    M = B * T
    depth = xa_norm.shape[0]
    mix_a = ev_pool_scale.shape[1]
    mix_b = lb_table.shape[1]
    fox_heads = od_b_f.shape[1]
    n_mem = mem.shape[1]
    xa_scale = (D // XA_HEADS) ** -0.5
    fox_tq = _pick_tile(T, 512)

    xs = x.reshape(M, D)
    mem2 = mem.reshape(B * n_mem, D)
    w_down = ffn_w_down.astype(BF16)
    ev_wo, od_wo, xa_wo_b = ev_w_out.astype(BF16), od_w_out.astype(BF16), xa_wo.astype(BF16)
    pre = None
    for l in range(depth):
        if pre is None:
            h, ss = _rmsnorm(xs, ev_norm[l // 2] if l % 2 == 0 else od_norm[l // 2], BF16), None
        else:
            h, ss = pre
        if l % 2 == 0:
            e = l // 2
            z = _matmul(h, ev_w_in, e, ev_w_in.shape[2], BF16, row_ss=ss).reshape(B, T, -1)
            ya = _pool_mixer(z, ev_w_pool[e].astype(BF16), ev_pool_scale[e], mix_a)
            yb = _hgrn_mixer(z, lb_table, ev_hg_norm[e], l, mix_a, mix_b)
            xs, h = _matmul_residual_norm([ya.reshape(M, mix_a), yb.reshape(M, mix_b)], ev_wo, e, xs, xa_norm[l])
        else:
            o = l // 2
            qscale = jnp.concatenate([jnp.full((D,), LOG2E * FOX_HEAD ** -0.5, F32), jnp.ones((2 * D,), F32)])
            w_in = od_w_in.astype(BF16)
            qkv = _matmul(h, w_in, o, 3 * D, BF16, colscale=qscale, row_ss=ss).reshape(B, T, 3 * D)
            wf = jnp.zeros((D, LANES), BF16).at[:, :fox_heads].set(w_in[o, :, 3 * D:])
            bf = jnp.zeros((1, LANES), F32).at[0, :fox_heads].set(od_b_f[o].astype(F32))
            ss4 = None if ss is None else ss.reshape(ss.shape[0], B, T, 1)
            f = _fox_gates(h.reshape(B, T, D), wf, bf, ss4)
            frow = f[..., :fox_heads].transpose(0, 2, 1).reshape(B, fox_heads, T // fox_tq, 1, fox_tq)
            y = _fox_attention(qkv, f, frow, fox_heads, fox_tq).reshape(M, D)
            xs, h = _matmul_residual_norm([y], od_wo, o, xs, xa_norm[l])

        mn = _rmsnorm(mem2, xa_mem_norm[l], BF16)
        q = _matmul(h, xa_wq, l, D, BF16, colscale=jnp.full((D,), xa_scale, F32))
        kv = _matmul(mn, xa_wkv, l, 2 * D, BF16)
        xs, h = _xattn_out_norm(q, kv.reshape(B, n_mem, 2 * D), xa_wo_b, l, xs, ffn_norm[l], T)

        act = _matmul_swiglu(h, ffn_w_gate, ffn_w_up, l)
        if l + 1 < depth:
            nxt = l + 1
            gain = ev_norm[nxt // 2] if nxt % 2 == 0 else od_norm[nxt // 2]
            xs, hp, ss = _matmul_residual(act, w_down, l, xs, gain=gain)
            pre = (hp, ss)
        else:
            xs = _matmul_residual(act, w_down, l, xs)
    return _rmsnorm(xs, final_norm, x.dtype).reshape(B, T, D)
```

```python
import functools

import jax
import jax.numpy as jnp
from jax import lax
from jax.experimental import pallas as pl
from jax.experimental.pallas import tpu as pltpu

F32 = jnp.float32
BF16 = jnp.bfloat16

EPS = 1e-6
POOL_WINDOWS = (2, 4, 8, 16)
POOL_HALO = 16
assert all(w & (w - 1) == 0 and w <= POOL_HALO for w in POOL_WINDOWS)
HG_HEAD = 128
HG_CHUNK = 64
HG_DIAG = 8
HG_SUB = 128
FOX_HEAD = 128
FOX_GROUP = 4
XA_HEADS = 4

LOG2E = 1.4426950408889634
LANES = 128
VMEM_CAP = 56 * 1024 * 1024
VMEM_FLOOR = 32 * 1024 * 1024


def _params(semantics, vmem_estimate):
    limit = int(min(max(vmem_estimate * 5 // 4, VMEM_FLOOR), VMEM_CAP))
    return pltpu.CompilerParams(dimension_semantics=semantics, vmem_limit_bytes=limit)


def _nbytes(shape, dtype):
    n = jnp.dtype(dtype).itemsize
    for s in shape:
        n *= s
    return n


def _rmsnorm_body(x_ref, g_ref, o_ref):
    x = x_ref[...]
    ms = jnp.mean(x * x, axis=-1, keepdims=True)
    o_ref[...] = (x * lax.rsqrt(ms + EPS) * g_ref[...]).astype(o_ref.dtype)


def _rmsnorm(x, g, out_dtype, tm=512):
    M, D = x.shape
    tm = min(tm, M)
    est = 2 * (_nbytes((tm, D), x.dtype) + _nbytes((tm, D), out_dtype)) + 3 * _nbytes((tm, D), F32)
    return pl.pallas_call(
        _rmsnorm_body,
        grid=(M // tm,),
        in_specs=[pl.BlockSpec((tm, D), lambda i: (i, 0)),
                  pl.BlockSpec((1, D), lambda i: (0, 0))],
        out_specs=pl.BlockSpec((tm, D), lambda i: (i, 0)),
        out_shape=jax.ShapeDtypeStruct((M, D), out_dtype),
        compiler_params=_params(("parallel",), est),
        name="rmsnorm",
    )(x, g.reshape(1, D).astype(F32))


def _with_bf16_weights(w_refs, wb_refs, compute):
    if not wb_refs:
        compute(*(w[...] for w in w_refs))
        return
    i = pl.program_id(1)

    @pl.when(i == 0)
    def _():
        for w, wb in zip(w_refs, wb_refs):
            wb[...] = w[...].astype(BF16)

    @pl.when(i > 0)
    def _():
        compute(*(wb[...] for wb in wb_refs))


def _row_rsqrt(ss_ref, width):
    return lax.rsqrt(jnp.sum(ss_ref[...], axis=0) * (1.0 / width) + EPS)


def _mm_body(x_ref, w_ref, cs_ref, *rest, norm_width):
    if norm_width:
        ss_ref, o_ref, *wb = rest
    else:
        o_ref, *wb = rest

    def compute(w):
        acc = jnp.dot(x_ref[...], w, preferred_element_type=F32)
        if norm_width:
            acc = acc * _row_rsqrt(ss_ref, norm_width)
        o_ref[...] = (acc * cs_ref[...]).astype(o_ref.dtype)

    _with_bf16_weights((w_ref,), wb, compute)


def _mm_res_body(x_ref, w_ref, r_ref, *rest, with_gain):
    if with_gain:
        g_ref, o_ref, hp_ref, ss_ref, *wb = rest
    else:
        o_ref, *wb = rest

    def compute(w):
        acc = r_ref[...] + jnp.dot(x_ref[...], w, preferred_element_type=F32)
        o_ref[...] = acc
        if with_gain:
            hp_ref[...] = (acc * g_ref[...]).astype(hp_ref.dtype)
            ss_ref[...] = jnp.sum(acc * acc, axis=-1, keepdims=True)

    _with_bf16_weights((w_ref,), wb, compute)


def _mm_swiglu_body(x_ref, wg_ref, wu_ref, o_ref, *wb):
    def compute(wg, wu):
        x = x_ref[...]
        g = jnp.dot(x, wg, preferred_element_type=F32)
        u = jnp.dot(x, wu, preferred_element_type=F32)
        o_ref[...] = (g * jax.nn.sigmoid(g) * u).astype(o_ref.dtype)

    _with_bf16_weights((wg_ref, wu_ref), wb, compute)


def _pick_tile(n, pref):
    t = min(pref, n)
    while n % t:
        t //= 2
    return t


def _weight_spec(layer, K, tn):
    return pl.BlockSpec((None, K, tn), lambda j, i: (layer, 0, j))


def _weight_scratch(w, K, tn):
    return [] if w.dtype == BF16 else [pltpu.VMEM((K, tn), BF16)]


def _weight_bytes(w, K, tn):
    return 2 * _nbytes((K, tn), w.dtype) + (0 if w.dtype == BF16 else _nbytes((K, tn), BF16))


def _row_steps(w, M, tm):
    extra = 0 if w.dtype == BF16 else 1
    return M // tm + extra, lambda i: jnp.maximum(i - extra, 0)


def _matmul(x, w, layer, n_cols, out_dtype, colscale=None, row_ss=None, tm=1024, tn=1024):
    M, K = x.shape
    tm, tn = _pick_tile(M, tm), _pick_tile(n_cols, tn)
    if colscale is None:
        colscale = jnp.ones((n_cols,), F32)
    steps, row = _row_steps(w, M, tm)
    ss_specs = [] if row_ss is None else [pl.BlockSpec((row_ss.shape[0], tm, 1), lambda j, i: (0, row(i), 0))]
    ss_args = [] if row_ss is None else [row_ss]
    est = 2 * (_nbytes((tm, K), x.dtype) + _nbytes((tm, tn), out_dtype)) + _weight_bytes(w, K, tn) \
        + 2 * _nbytes((tm, tn), F32)
    return pl.pallas_call(
        functools.partial(_mm_body, norm_width=0 if row_ss is None else K),
        grid=(n_cols // tn, steps),
        in_specs=[pl.BlockSpec((tm, K), lambda j, i: (row(i), 0)),
                  _weight_spec(layer, K, tn),
                  pl.BlockSpec((1, tn), lambda j, i: (0, j))] + ss_specs,
        out_specs=pl.BlockSpec((tm, tn), lambda j, i: (row(i), j)),
        out_shape=jax.ShapeDtypeStruct((M, n_cols), out_dtype),
        scratch_shapes=_weight_scratch(w, K, tn),
        compiler_params=_params(("parallel", "arbitrary"), est),
        name="matmul",
    )(x, w, colscale.reshape(1, n_cols), *ss_args)


def _matmul_residual(x, w, layer, r, gain=None, tn=1024):
    M, K = x.shape
    N = w.shape[2]
    tm = 1024 if K <= 2048 else 512
    tm, tn = _pick_tile(M, tm), _pick_tile(N, tn)
    steps, row = _row_steps(w, M, tm)
    est = 2 * (_nbytes((tm, K), x.dtype) + 2 * _nbytes((tm, tn), F32)) + _weight_bytes(w, K, tn) \
        + 2 * _nbytes((tm, tn), F32)
    tile = pl.BlockSpec((tm, tn), lambda j, i: (row(i), j))
    in_specs = [pl.BlockSpec((tm, K), lambda j, i: (row(i), 0)), _weight_spec(layer, K, tn), tile]
    out_specs, out_shape, args = tile, jax.ShapeDtypeStruct((M, N), F32), [x, w, r]
    if gain is not None:
        in_specs.append(pl.BlockSpec((1, tn), lambda j, i: (0, j)))
        args.append(gain.reshape(1, N).astype(F32))
        out_specs = [tile, tile, pl.BlockSpec((None, tm, 1), lambda j, i: (j, row(i), 0))]
        out_shape = [out_shape, jax.ShapeDtypeStruct((M, N), BF16), jax.ShapeDtypeStruct((N // tn, M, 1), F32)]
    return pl.pallas_call(
        functools.partial(_mm_res_body, with_gain=gain is not None),
        grid=(N // tn, steps),
        in_specs=in_specs,
        out_specs=out_specs,
        out_shape=out_shape,
        scratch_shapes=_weight_scratch(w, K, tn),
        compiler_params=_params(("parallel", "arbitrary"), est),
        name="matmul_residual",
    )(*args)


def _matmul_swiglu(x, wg, wu, layer, tm=1024, tn=512):
    M, K = x.shape
    N = wg.shape[2]
    tm, tn = _pick_tile(M, tm), _pick_tile(N, tn)
    steps, row = _row_steps(wg, M, tm)
    est = 2 * (_nbytes((tm, K), x.dtype) + _nbytes((tm, tn), BF16)) + 2 * _weight_bytes(wg, K, tn) \
        + 4 * _nbytes((tm, tn), F32)
    return pl.pallas_call(
        _mm_swiglu_body,
        grid=(N // tn, steps),
        in_specs=[pl.BlockSpec((tm, K), lambda j, i: (row(i), 0)),
                  _weight_spec(layer, K, tn),
                  _weight_spec(layer, K, tn)],
        out_specs=pl.BlockSpec((tm, tn), lambda j, i: (row(i), j)),
        out_shape=jax.ShapeDtypeStruct((M, N), BF16),
        scratch_shapes=_weight_scratch(wg, K, tn) + _weight_scratch(wu, K, tn),
        compiler_params=_params(("parallel", "arbitrary"), est),
        name="matmul_swiglu",
    )(x, wg, wu)


def _mm_res_norm_body(*refs, n_parts):
    x_refs = refs[:n_parts]
    w_ref, r_ref, g_ref, o_ref, h_ref = refs[n_parts:]
    acc = r_ref[...]
    k0 = 0
    for x_ref in x_refs:
        kw = x_ref.shape[1]
        acc = acc + jnp.dot(x_ref[...], w_ref[k0:k0 + kw, :], preferred_element_type=F32)
        k0 += kw
    o_ref[...] = acc
    ms = jnp.mean(acc * acc, axis=-1, keepdims=True)
    h_ref[...] = (acc * lax.rsqrt(ms + EPS) * g_ref[...]).astype(h_ref.dtype)


def _matmul_residual_norm(x_parts, w, layer, r, gain, tm=512):
    M, N = r.shape
    K = w.shape[1]
    tm = _pick_tile(M, tm)
    est = 2 * (_nbytes((tm, K), BF16) + _nbytes((K, N), w.dtype) + 2 * _nbytes((tm, N), F32) + _nbytes((tm, N), BF16)) \
        + 3 * _nbytes((tm, N), F32)
    return pl.pallas_call(
        functools.partial(_mm_res_norm_body, n_parts=len(x_parts)),
        grid=(M // tm,),
        in_specs=[pl.BlockSpec((tm, xp.shape[1]), lambda i: (i, 0)) for xp in x_parts]
        + [pl.BlockSpec((None, K, N), lambda i: (layer, 0, 0)),
           pl.BlockSpec((tm, N), lambda i: (i, 0)),
           pl.BlockSpec((1, N), lambda i: (0, 0))],
        out_specs=[pl.BlockSpec((tm, N), lambda i: (i, 0)),
                   pl.BlockSpec((tm, N), lambda i: (i, 0))],
        out_shape=[jax.ShapeDtypeStruct((M, N), F32),
                   jax.ShapeDtypeStruct((M, N), BF16)],
        compiler_params=_params(("parallel",), est),
        name="matmul_residual_norm",
    )(*x_parts, w, r, gain.reshape(1, N).astype(F32))


def _pool_body(u_ref, halo_ref, w_ref, sc_ref, o_ref, ext_ref, *, tt, group):
    t = pl.program_id(1)
    u = u_ref[0].astype(F32)
    ext_ref[0:POOL_HALO, :] = jnp.where(t > 0, halo_ref[0].astype(F32), 0.0)
    ext_ref[POOL_HALO:POOL_HALO + tt, :] = u
    pos = t * tt + lax.broadcasted_iota(jnp.int32, (tt, 1), 0)
    for gi, win in enumerate(POOL_WINDOWS):
        c0, c1 = gi * group, (gi + 1) * group
        tok = u[:, c0:c1]
        lvl = ext_ref[:, c0:c1]
        k = 1
        while k < win:
            lvl = lvl + pltpu.roll(lvl, k, axis=0)
            k *= 2
        acc = lvl[POOL_HALO:POOL_HALO + tt]
        cnt = jnp.minimum(pos + 1, win).astype(F32)
        p = acc / cnt - tok
        y = jnp.dot(p.astype(BF16), w_ref[gi], preferred_element_type=F32)
        o_ref[0, :, c0:c1] = (y * sc_ref[:, c0:c1]).astype(o_ref.dtype)


def _pool_mixer(z, w_pool, scale, mix_a, tt=1024):
    B, T, _ = z.shape
    tt = min(tt, T)
    group = mix_a // len(POOL_WINDOWS)
    halo_blocks = tt // POOL_HALO
    est = 2 * (_nbytes((tt, mix_a), F32) + _nbytes((tt, mix_a), BF16)) + 4 * _nbytes((tt, mix_a), F32)
    return pl.pallas_call(
        functools.partial(_pool_body, tt=tt, group=group),
        grid=(B, T // tt),
        in_specs=[pl.BlockSpec((1, tt, mix_a), lambda b, t: (b, t, 0)),
                  pl.BlockSpec((1, POOL_HALO, mix_a),
                               lambda b, t: (b, jnp.maximum(t * halo_blocks - 1, 0), 0)),
                  pl.BlockSpec((len(POOL_WINDOWS), group, group), lambda b, t: (0, 0, 0)),
                  pl.BlockSpec((1, mix_a), lambda b, t: (0, 0))],
        out_specs=pl.BlockSpec((1, tt, mix_a), lambda b, t: (b, t, 0)),
        out_shape=jax.ShapeDtypeStruct((B, T, mix_a), BF16),
        scratch_shapes=[pltpu.VMEM((tt + POOL_HALO, mix_a), F32)],
        compiler_params=_params(("parallel", "parallel"), est),
        name="pool_mixer",
    )(z, z, w_pool, scale.reshape(1, mix_a).astype(F32))


def _hgrn_subtile(qf, kk, v, logf, st):
    n = qf.shape[0]
    C, R = HG_CHUNK, HG_DIAG
    row = lax.broadcasted_iota(jnp.int32, (n, HG_HEAD), 0)

    pos = row & (C - 1)
    b = logf
    sh = 1
    while sh < C:
        b = b + jnp.where(pos >= sh, pltpu.roll(b, sh, axis=0), 0.0)
        sh *= 2

    nb = n // R
    b3, q3, c3, v3 = (a.reshape(nb, R, HG_HEAD) for a in (b, qf, b - jnp.log2(kk), v))
    tpos = lax.broadcasted_iota(jnp.int32, (1, R, 1), 1)
    od = jnp.zeros((nb, R, HG_HEAD), F32)
    for s in range(R):
        w = q3 * jnp.exp2(b3 - c3[:, s:s + 1, :])
        a = jnp.sum(w, axis=-1, keepdims=True)
        a = jnp.where(tpos >= s, a, 0.0)
        od = od + a * v3[:, s:s + 1, :]
    o = od.reshape(n, HG_HEAD)

    same = lax.broadcasted_iota(jnp.int32, (n, n), 0) ^ lax.broadcasted_iota(jnp.int32, (n, n), 1)
    a_off = jnp.zeros((n, n), F32)
    h = R
    while h < C:
        g = 2 * h
        ref = b.reshape(n // g, g, HG_HEAD)[:, h - 1:h, :]
        ref = jnp.broadcast_to(ref, (n // g, g, HG_HEAD)).reshape(n, HG_HEAD)
        e = jnp.exp2(-jnp.abs(b - ref))
        right = (row & (g - 1)) >= h
        qt = jnp.where(right, qf * e, 0.0).astype(BF16)
        kt = jnp.where(right, 0.0, kk * e).astype(BF16)
        a = lax.dot_general(qt, kt, (((1,), (1,)), ((), ())), preferred_element_type=F32)
        a_off = a_off + jnp.where(same < g, a, 0.0)
        h = g
    v_bf = v.astype(BF16)
    o = o + jnp.dot(a_off.astype(BF16), v_bf, preferred_element_type=F32)

    nc = n // C
    bc = b.reshape(nc, C, HG_HEAD)
    b_last = bc[:, C - 1:C, :]
    q_in = (qf * jnp.exp2(b)).astype(BF16)
    k_out = (kk.reshape(nc, C, HG_HEAD) * jnp.exp2(b_last - bc)).astype(BF16)
    dec = jnp.exp2(b_last)
    pieces = []
    for c in range(nc):
        rows = slice(c * C, (c + 1) * C)
        o_int = lax.dot_general(q_in[rows], st.astype(BF16), (((1,), (1,)), ((), ())),
                                preferred_element_type=F32)
        pieces.append(o[rows] + o_int)
        upd = lax.dot_general(v_bf[rows], k_out[c], (((0,), (0,)), ((), ())),
                              preferred_element_type=F32)
        st = st * dec[c] + upd
    return jnp.concatenate(pieces, axis=0), st


def _hgrn_body(lbt_ref, ng_ref, q_ref, f_ref, i_ref, g_ref, o_ref, st_ref, *, layer, tt, sub):
    t = pl.program_id(2)

    @pl.when(t == 0)
    def _():
        st_ref[...] = jnp.zeros_like(st_ref)

    lbt = lbt_ref[...]
    e = jnp.exp(lbt - jnp.max(lbt, axis=0, keepdims=True))
    sm = e / jnp.sum(e, axis=0, keepdims=True)
    lb = jnp.sum(sm[1:layer + 2], axis=0, keepdims=True)

    st = st_ref[...]
    for c in range(tt // sub):
        rows = slice(c * sub, (c + 1) * sub)
        f = lb + (1.0 - lb) * jax.nn.sigmoid(f_ref[0, rows, :].astype(F32))
        q = q_ref[0, rows, :].astype(F32)
        qf = q * jax.nn.sigmoid(q) * (HG_HEAD ** -0.5)
        o, st = _hgrn_subtile(qf, 1.0 - f, i_ref[0, rows, :].astype(F32), jnp.log2(f), st)
        ms = jnp.mean(o * o, axis=-1, keepdims=True)
        o = o * lax.rsqrt(ms + EPS) * ng_ref[...]
        g = g_ref[0, rows, :].astype(F32)
        o_ref[0, rows, :] = (o * (g * jax.nn.sigmoid(g))).astype(o_ref.dtype)
    st_ref[...] = st


def _hgrn_mixer(z, lb_table, norm_g, layer, mix_a, mix_b, tt=2048):
    B, T, _ = z.shape
    tt = min(tt, T)
    sub = min(HG_SUB, tt)
    heads = mix_b // HG_HEAD
    c0 = mix_a // HG_HEAD
    est = 2 * 5 * _nbytes((tt, HG_HEAD), F32) + 32 * _nbytes((sub, HG_HEAD), F32) + 6 * _nbytes((sub, sub), F32)

    def col(off):
        return pl.BlockSpec((1, tt, HG_HEAD), lambda b, h, t, off=off: (b, t, c0 + off * heads + h))

    return pl.pallas_call(
        functools.partial(_hgrn_body, layer=layer, tt=tt, sub=sub),
        grid=(B, heads, T // tt),
        in_specs=[pl.BlockSpec((lb_table.shape[0], HG_HEAD), lambda b, h, t: (0, h)),
                  pl.BlockSpec((1, HG_HEAD), lambda b, h, t: (0, 0)),
                  col(0), col(1), col(2), col(3)],
        out_specs=pl.BlockSpec((1, tt, HG_HEAD), lambda b, h, t: (b, t, h)),
        out_shape=jax.ShapeDtypeStruct((B, T, mix_b), BF16),
        scratch_shapes=[pltpu.VMEM((HG_HEAD, HG_HEAD), F32)],
        compiler_params=_params(("parallel", "parallel", "arbitrary"), est),
        name="hgrn2_mixer",
    )(lb_table.astype(F32), norm_g.reshape(1, HG_HEAD).astype(F32), z, z, z, z)


def _fgate_body(h_ref, w_ref, b_ref, *rest, tt, norm_width):
    if norm_width:
        ss_ref, o_ref, carry_ref = rest
    else:
        o_ref, carry_ref = rest
    t = pl.program_id(1)

    @pl.when(t == 0)
    def _():
        carry_ref[...] = jnp.zeros_like(carry_ref)

    fl = jnp.dot(h_ref[0], w_ref[...], preferred_element_type=F32)
    if norm_width:
        fl = fl * _row_rsqrt(ss_ref, norm_width)
    fl = fl + b_ref[...]
    c = (jnp.minimum(fl, 0.0) - jnp.log1p(jnp.exp(-jnp.abs(fl)))) * LOG2E
    row = lax.broadcasted_iota(jnp.int32, c.shape, 0)
    sh = 1
    while sh < tt:
        c = c + jnp.where(row >= sh, pltpu.roll(c, sh, axis=0), 0.0)
        sh *= 2
    c = c + carry_ref[...]
    o_ref[0] = c
    carry_ref[...] = c[tt - 1:tt, :]


def _fox_gates(h, wf, bf, row_ss=None, tt=512):
    B, T, D = h.shape
    tt = min(tt, T)
    ss_specs = [] if row_ss is None else [pl.BlockSpec((row_ss.shape[0], None, tt, 1), lambda b, t: (0, b, t, 0))]
    ss_args = [] if row_ss is None else [row_ss]
    est = 2 * (_nbytes((tt, D), BF16) + _nbytes((D, LANES), BF16) + _nbytes((tt, LANES), F32)) \
        + 8 * _nbytes((tt, LANES), F32)
    return pl.pallas_call(
        functools.partial(_fgate_body, tt=tt, norm_width=0 if row_ss is None else D),
        grid=(B, T // tt),
        in_specs=[pl.BlockSpec((1, tt, D), lambda b, t: (b, t, 0)),
                  pl.BlockSpec((D, LANES), lambda b, t: (0, 0)),
                  pl.BlockSpec((1, LANES), lambda b, t: (0, 0))] + ss_specs,
        out_specs=pl.BlockSpec((1, tt, LANES), lambda b, t: (b, t, 0)),
        out_shape=jax.ShapeDtypeStruct((B, T, LANES), F32),
        scratch_shapes=[pltpu.VMEM((1, LANES), F32)],
        compiler_params=_params(("parallel", "arbitrary"), est),
        name="fox_gates",
    )(h, wf, bf, *ss_args)


def _fox_body(q_ref, k_ref, v_ref, f_ref, fr_ref, o_ref, vt_ref, fk_ref, p_ref, acc_ref, *, tq, seq, group):
    hg = pl.program_id(1)
    i = pl.program_id(2)
    reps = tq // LANES
    heads = range(group)

    def head_cols(g):
        return slice(g * FOX_HEAD, (g + 1) * FOX_HEAD)

    @pl.when(i == 0)
    def _():
        for c in range(seq // tq):
            rows = slice(c * tq, (c + 1) * tq)
            fblk = f_ref[0, rows, :]
            lane = lax.broadcasted_iota(jnp.int32, fblk.shape, 1)
            for g in heads:
                vt_ref[g, :, rows] = v_ref[0, rows, head_cols(g)].T
                col = jnp.sum(jnp.where(lane == hg * group + g, fblk, 0.0), axis=-1, keepdims=True)
                fk_ref[g, rows, :] = jnp.broadcast_to(col, fblk.shape)

    qt = [q_ref[0, :, head_cols(g)].T for g in heads]
    fq = [fr_ref[0, g, i] for g in heads]

    def scores(g, j):
        start = pl.multiple_of(j * tq, tq)
        return jnp.dot(k_ref[0, pl.ds(start, tq), head_cols(g)], qt[g], preferred_element_type=F32)

    def softmax_block(g, j, st, m_prev, l_prev, diagonal):
        start = pl.multiple_of(j * tq, tq)
        t = st - jnp.concatenate([fk_ref[g, pl.ds(start, tq), :]] * reps, axis=1)
        if diagonal:
            key = lax.broadcasted_iota(jnp.int32, (tq, tq), 0)
            qry = lax.broadcasted_iota(jnp.int32, (tq, tq), 1)
            t = jnp.where(key <= qry, t, -jnp.inf)
        m_new = jnp.maximum(m_prev, jnp.max(t, axis=0, keepdims=True) + fq[g])
        alpha = jnp.exp2(m_prev - m_new)
        p = jnp.exp2(t - (m_new - fq[g]))
        l_new = alpha * l_prev + jnp.sum(p, axis=0, keepdims=True)
        return p.astype(BF16), m_new, l_new, alpha

    def weighted_values(g, j, p):
        start = pl.multiple_of(j * tq, tq)
        return jnp.dot(vt_ref[g, :, pl.ds(start, tq)], p, preferred_element_type=F32)

    def block(j, j_prev, stats, first):
        out = []
        for g in heads:
            m, l = stats[g]
            st = scores(g, j)
            if first:
                p, m, l, _ = softmax_block(g, j, st, m, l, True)
                acc_ref[g] = jnp.zeros((FOX_HEAD, tq), F32)
            else:
                pv_prev = weighted_values(g, j_prev, p_ref[g])
                p, m, l, alpha = softmax_block(g, j, st, m, l, False)
                acc_ref[g] = (acc_ref[g] + pv_prev) * alpha
            p_ref[g] = p
            out.append((m, l))
        return tuple(out)

    stats = tuple((jnp.full((1, tq), -jnp.inf, F32), jnp.zeros((1, tq), F32)) for _ in heads)
    stats = block(i, i, stats, True)
    odd = i % 2
    stats = lax.cond(odd == 1, lambda c: block(0, i, c, False), lambda c: c, stats)

    def pair(jj, c):
        j0 = odd + 2 * jj
        c = block(j0, jnp.where(j0 == 0, i, j0 - 1), c, False)
        return block(j0 + 1, j0, c, False)

    stats = lax.fori_loop(0, i // 2, pair, stats)
    j_last = jnp.where(i == 0, i, i - 1)
    for g in heads:
        acc = acc_ref[g] + weighted_values(g, j_last, p_ref[g])
        o_ref[0, :, head_cols(g)] = (acc / stats[g][1]).astype(o_ref.dtype).T


def _fox_attention(qkv, f, frow, heads, tq, group=FOX_GROUP):
    B, T, _ = qkv.shape
    nq = T // tq
    gw = group * FOX_HEAD
    ngrp = heads // group
    est = 2 * (2 * _nbytes((T, gw), BF16) + 2 * _nbytes((tq, gw), BF16)
               + _nbytes((T, LANES), F32) + group * _nbytes((nq, 8, tq), F32)) \
        + group * (_nbytes((T, FOX_HEAD), BF16) + _nbytes((T, LANES), F32) + 5 * _nbytes((tq, tq), F32))
    return pl.pallas_call(
        functools.partial(_fox_body, tq=tq, seq=T, group=group),
        grid=(B, ngrp, nq),
        in_specs=[pl.BlockSpec((1, tq, gw), lambda b, h, i: (b, i, h)),
                  pl.BlockSpec((1, T, gw), lambda b, h, i: (b, 0, ngrp + h)),
                  pl.BlockSpec((1, T, gw), lambda b, h, i: (b, 0, 2 * ngrp + h)),
                  pl.BlockSpec((1, T, LANES), lambda b, h, i: (b, 0, 0)),
                  pl.BlockSpec((1, group, nq, 1, tq), lambda b, h, i: (b, h, 0, 0, 0))],
        out_specs=pl.BlockSpec((1, tq, gw), lambda b, h, i: (b, i, h)),
        out_shape=jax.ShapeDtypeStruct((B, T, heads * FOX_HEAD), BF16),
        scratch_shapes=[pltpu.VMEM((group, FOX_HEAD, T), BF16),
                        pltpu.VMEM((group, T, LANES), F32),
                        pltpu.VMEM((group, tq, tq), BF16),
                        pltpu.VMEM((group, FOX_HEAD, tq), F32)],
        compiler_params=_params(("parallel", "parallel", "arbitrary"), est),
        name="fox_attention",
    )(qkv, qkv, qkv, f, frow)


def _xattn_out_body(q_ref, kv_ref, w_ref, r_ref, g_ref, o_ref, h_ref, a_ref, *, d_model):
    hd = d_model // XA_HEADS
    for hh in range(XA_HEADS):
        q = q_ref[:, hh * hd:(hh + 1) * hd]
        k = kv_ref[:, hh * hd:(hh + 1) * hd]
        v = kv_ref[:, d_model + hh * hd:d_model + (hh + 1) * hd]
        s = lax.dot_general(q, k, (((1,), (1,)), ((), ())), preferred_element_type=F32)
        p = jnp.exp(s - jnp.max(s, axis=-1, keepdims=True))
        p = p / jnp.sum(p, axis=-1, keepdims=True)
        a_ref[:, hh * hd:(hh + 1) * hd] = jnp.dot(p.astype(BF16), v, preferred_element_type=F32).astype(a_ref.dtype)
    acc = r_ref[...] + jnp.dot(a_ref[...], w_ref[...], preferred_element_type=F32)
    o_ref[...] = acc
    ms = jnp.mean(acc * acc, axis=-1, keepdims=True)
    h_ref[...] = (acc * lax.rsqrt(ms + EPS) * g_ref[...]).astype(h_ref.dtype)


def _xattn_out_norm(q, kv, w, layer, r, gain, seq, tm=512):
    M, D = q.shape
    n_mem = kv.shape[1]
    tm = _pick_tile(seq, tm)
    per_batch = seq // tm
    est = 2 * (_nbytes((tm, D), BF16) + _nbytes((n_mem, 2 * D), BF16) + _nbytes((D, D), BF16)
               + 2 * _nbytes((tm, D), F32) + _nbytes((tm, D), BF16)) \
        + _nbytes((tm, D), BF16) + 3 * _nbytes((tm, D), F32)
    row = pl.BlockSpec((tm, D), lambda i: (i, 0))
    return pl.pallas_call(
        functools.partial(_xattn_out_body, d_model=D),
        grid=(M // tm,),
        in_specs=[row,
                  pl.BlockSpec((None, n_mem, 2 * D), lambda i: (i // per_batch, 0, 0)),
                  pl.BlockSpec((None, D, D), lambda i: (layer, 0, 0)),
                  row,
                  pl.BlockSpec((1, D), lambda i: (0, 0))],
        out_specs=[row, row],
        out_shape=[jax.ShapeDtypeStruct((M, D), F32),
                   jax.ShapeDtypeStruct((M, D), BF16)],
        scratch_shapes=[pltpu.VMEM((tm, D), BF16)],
        compiler_params=_params(("parallel",), est),
        name="xattn_out_norm",
    )(q, kv, w, r, gain.reshape(1, D).astype(F32))


def kernel(x, mem, lb_table, ev_norm, ev_w_in, ev_w_pool, ev_pool_scale, ev_hg_norm, ev_w_out,
           od_norm, od_w_in, od_b_f, od_w_out, xa_norm, xa_mem_norm, xa_wq, xa_wkv, xa_wo,
           ffn_norm, ffn_w_gate, ffn_w_up, ffn_w_down, final_norm):
    B, T, D = x.shape
    M = B * T
    depth = xa_norm.shape[0]
    mix_a = ev_pool_scale.shape[1]
    mix_b = lb_table.shape[1]
    fox_heads = od_b_f.shape[1]
    n_mem = mem.shape[1]
    xa_scale = (D // XA_HEADS) ** -0.5
    fox_tq = _pick_tile(T, 512)

    xs = x.reshape(M, D)
    mem2 = mem.reshape(B * n_mem, D)
    w_down = ffn_w_down.astype(BF16)
    ev_wo, od_wo, xa_wo_b = ev_w_out.astype(BF16), od_w_out.astype(BF16), xa_wo.astype(BF16)
    pre = None
    for l in range(depth):
        if pre is None:
            h, ss = _rmsnorm(xs, ev_norm[l // 2] if l % 2 == 0 else od_norm[l // 2], BF16), None
        else:
            h, ss = pre
        if l % 2 == 0:
            e = l // 2
            z = _matmul(h, ev_w_in, e, ev_w_in.shape[2], BF16, row_ss=ss).reshape(B, T, -1)
            ya = _pool_mixer(z, ev_w_pool[e].astype(BF16), ev_pool_scale[e], mix_a)
            yb = _hgrn_mixer(z, lb_table, ev_hg_norm[e], l, mix_a, mix_b)
            xs, h = _matmul_residual_norm([ya.reshape(M, mix_a), yb.reshape(M, mix_b)], ev_wo, e, xs, xa_norm[l])
        else:
            o = l // 2
            qscale = jnp.concatenate([jnp.full((D,), LOG2E * FOX_HEAD ** -0.5, F32), jnp.ones((2 * D,), F32)])
            w_in = od_w_in.astype(BF16)
            qkv = _matmul(h, w_in, o, 3 * D, BF16, colscale=qscale, row_ss=ss).reshape(B, T, 3 * D)
            wf = jnp.zeros((D, LANES), BF16).at[:, :fox_heads].set(w_in[o, :, 3 * D:])
            bf = jnp.zeros((1, LANES), F32).at[0, :fox_heads].set(od_b_f[o].astype(F32))
            ss4 = None if ss is None else ss.reshape(ss.shape[0], B, T, 1)
            f = _fox_gates(h.reshape(B, T, D), wf, bf, ss4)
            frow = f[..., :fox_heads].transpose(0, 2, 1).reshape(B, fox_heads, T // fox_tq, 1, fox_tq)
            y = _fox_attention(qkv, f, frow, fox_heads, fox_tq).reshape(M, D)
            xs, h = _matmul_residual_norm([y], od_wo, o, xs, xa_norm[l])

        mn = _rmsnorm(mem2, xa_mem_norm[l], BF16)
        q = _matmul(h, xa_wq, l, D, BF16, colscale=jnp.full((D,), xa_scale, F32))
        kv = _matmul(mn, xa_wkv, l, 2 * D, BF16)
        xs, h = _xattn_out_norm(q, kv.reshape(B, n_mem, 2 * D), xa_wo_b, l, xs, ffn_norm[l], T)

        act = _matmul_swiglu(h, ffn_w_gate, ffn_w_up, l)
        if l + 1 < depth:
            nxt = l + 1
            gain = ev_norm[nxt // 2] if nxt % 2 == 0 else od_norm[nxt // 2]
            xs, hp, ss = _matmul_residual(act, w_down, l, xs, gain=gain)
            pre = (hp, ss)
        else:
            xs = _matmul_residual(act, w_down, l, xs)
    return _rmsnorm(xs, final_norm, x.dtype).reshape(B, T, D)
```

```python
import functools

import jax
import jax.numpy as jnp
from jax import lax
from jax.experimental import pallas as pl
from jax.experimental.pallas import tpu as pltpu

F32 = jnp.float32
BF16 = jnp.bfloat16

EPS = 1e-6
POOL_WINDOWS = (2, 4, 8, 16)
POOL_HALO = 16
assert all(w & (w - 1) == 0 and w <= POOL_HALO for w in POOL_WINDOWS)
HG_HEAD = 128
HG_CHUNK = 64
HG_DIAG = 8
HG_SUB = 128
FOX_HEAD = 128
FOX_GROUP = 4
XA_HEADS = 4

LOG2E = 1.4426950408889634
LANES = 128
VMEM_CAP = 56 * 1024 * 1024
VMEM_FLOOR = 32 * 1024 * 1024


def _params(semantics, vmem_estimate):
    limit = int(min(max(vmem_estimate * 5 // 4, VMEM_FLOOR), VMEM_CAP))
    return pltpu.CompilerParams(dimension_semantics=semantics, vmem_limit_bytes=limit)


def _nbytes(shape, dtype):
    n = jnp.dtype(dtype).itemsize
    for s in shape:
        n *= s
    return n


def _rmsnorm_body(x_ref, g_ref, o_ref):
    x = x_ref[...]
    ms = jnp.mean(x * x, axis=-1, keepdims=True)
    o_ref[...] = (x * lax.rsqrt(ms + EPS) * g_ref[...]).astype(o_ref.dtype)


def _rmsnorm(x, g, out_dtype, tm=512):
    M, D = x.shape
    tm = min(tm, M)
    est = 2 * (_nbytes((tm, D), x.dtype) + _nbytes((tm, D), out_dtype)) + 3 * _nbytes((tm, D), F32)
    return pl.pallas_call(
        _rmsnorm_body,
        grid=(M // tm,),
        in_specs=[pl.BlockSpec((tm, D), lambda i: (i, 0)),
                  pl.BlockSpec((1, D), lambda i: (0, 0))],
        out_specs=pl.BlockSpec((tm, D), lambda i: (i, 0)),
        out_shape=jax.ShapeDtypeStruct((M, D), out_dtype),
        compiler_params=_params(("parallel",), est),
        name="rmsnorm",
    )(x, g.reshape(1, D).astype(F32))


def _with_bf16_weights(w_refs, wb_refs, compute):
    if not wb_refs:
        compute(*(w[...] for w in w_refs))
        return
    i = pl.program_id(1)

    @pl.when(i == 0)
    def _():
        for w, wb in zip(w_refs, wb_refs):
            wb[...] = w[...].astype(BF16)

    @pl.when(i > 0)
    def _():
        compute(*(wb[...] for wb in wb_refs))


def _row_rsqrt(ss_ref, width):
    return lax.rsqrt(jnp.sum(ss_ref[...], axis=0) * (1.0 / width) + EPS)


def _mm_body(x_ref, w_ref, cs_ref, *rest, norm_width):
    if norm_width:
        ss_ref, o_ref, *wb = rest
    else:
        o_ref, *wb = rest

    def compute(w):
        acc = jnp.dot(x_ref[...], w, preferred_element_type=F32)
        if norm_width:
            acc = acc * _row_rsqrt(ss_ref, norm_width)
        o_ref[...] = (acc * cs_ref[...]).astype(o_ref.dtype)

    _with_bf16_weights((w_ref,), wb, compute)


def _mm_res_body(x_ref, w_ref, r_ref, *rest, with_gain):
    if with_gain:
        g_ref, o_ref, hp_ref, ss_ref, *wb = rest
    else:
        o_ref, *wb = rest

    def compute(w):
        acc = r_ref[...] + jnp.dot(x_ref[...], w, preferred_element_type=F32)
        o_ref[...] = acc
        if with_gain:
            hp_ref[...] = (acc * g_ref[...]).astype(hp_ref.dtype)
            ss_ref[...] = jnp.sum(acc * acc, axis=-1, keepdims=True)

    _with_bf16_weights((w_ref,), wb, compute)


def _mm_swiglu_body(x_ref, wg_ref, wu_ref, o_ref, *wb):
    def compute(wg, wu):
        x = x_ref[...]
        g = jnp.dot(x, wg, preferred_element_type=F32)
        u = jnp.dot(x, wu, preferred_element_type=F32)
        o_ref[...] = (g * jax.nn.sigmoid(g) * u).astype(o_ref.dtype)

    _with_bf16_weights((wg_ref, wu_ref), wb, compute)


def _pick_tile(n, pref):
    t = min(pref, n)
    while n % t:
        t //= 2
    return t


def _weight_spec(layer, K, tn):
    return pl.BlockSpec((None, K, tn), lambda j, i: (layer, 0, j))


def _weight_scratch(w, K, tn):
    return [] if w.dtype == BF16 else [pltpu.VMEM((K, tn), BF16)]


def _weight_bytes(w, K, tn):
    return 2 * _nbytes((K, tn), w.dtype) + (0 if w.dtype == BF16 else _nbytes((K, tn), BF16))


def _row_steps(w, M, tm):
    extra = 0 if w.dtype == BF16 else 1
    return M // tm + extra, lambda i: jnp.maximum(i - extra, 0)


def _matmul(x, w, layer, n_cols, out_dtype, colscale=None, row_ss=None, tm=1024, tn=1024):
    M, K = x.shape
    tm, tn = _pick_tile(M, tm), _pick_tile(n_cols, tn)
    if colscale is None:
        colscale = jnp.ones((n_cols,), F32)
    steps, row = _row_steps(w, M, tm)
    ss_specs = [] if row_ss is None else [pl.BlockSpec((row_ss.shape[0], tm, 1), lambda j, i: (0, row(i), 0))]
    ss_args = [] if row_ss is None else [row_ss]
    est = 2 * (_nbytes((tm, K), x.dtype) + _nbytes((tm, tn), out_dtype)) + _weight_bytes(w, K, tn) \
        + 2 * _nbytes((tm, tn), F32)
    return pl.pallas_call(
        functools.partial(_mm_body, norm_width=0 if row_ss is None else K),
        grid=(n_cols // tn, steps),
        in_specs=[pl.BlockSpec((tm, K), lambda j, i: (row(i), 0)),
                  _weight_spec(layer, K, tn),
                  pl.BlockSpec((1, tn), lambda j, i: (0, j))] + ss_specs,
        out_specs=pl.BlockSpec((tm, tn), lambda j, i: (row(i), j)),
        out_shape=jax.ShapeDtypeStruct((M, n_cols), out_dtype),
        scratch_shapes=_weight_scratch(w, K, tn),
        compiler_params=_params(("parallel", "arbitrary"), est),
        name="matmul",
    )(x, w, colscale.reshape(1, n_cols), *ss_args)


def _matmul_residual(x, w, layer, r, gain=None, tn=1024):
    M, K = x.shape
    N = w.shape[2]
    tm = 1024 if K <= 2048 else 512
    tm, tn = _pick_tile(M, tm), _pick_tile(N, tn)
    steps, row = _row_steps(w, M, tm)
    est = 2 * (_nbytes((tm, K), x.dtype) + 2 * _nbytes((tm, tn), F32)) + _weight_bytes(w, K, tn) \
        + 2 * _nbytes((tm, tn), F32)
    tile = pl.BlockSpec((tm, tn), lambda j, i: (row(i), j))
    in_specs = [pl.BlockSpec((tm, K), lambda j, i: (row(i), 0)), _weight_spec(layer, K, tn), tile]
    out_specs, out_shape, args = tile, jax.ShapeDtypeStruct((M, N), F32), [x, w, r]
    if gain is not None:
        in_specs.append(pl.BlockSpec((1, tn), lambda j, i: (0, j)))
        args.append(gain.reshape(1, N).astype(F32))
        out_specs = [tile, tile, pl.BlockSpec((None, tm, 1), lambda j, i: (j, row(i), 0))]
        out_shape = [out_shape, jax.ShapeDtypeStruct((M, N), BF16), jax.ShapeDtypeStruct((N // tn, M, 1), F32)]
    return pl.pallas_call(
        functools.partial(_mm_res_body, with_gain=gain is not None),
        grid=(N // tn, steps),
        in_specs=in_specs,
        out_specs=out_specs,
        out_shape=out_shape,
        scratch_shapes=_weight_scratch(w, K, tn),
        compiler_params=_params(("parallel", "arbitrary"), est),
        name="matmul_residual",
    )(*args)


def _matmul_swiglu(x, wg, wu, layer, tm=1024, tn=512):
    M, K = x.shape
    N = wg.shape[2]
    tm, tn = _pick_tile(M, tm), _pick_tile(N, tn)
    steps, row = _row_steps(wg, M, tm)
    est = 2 * (_nbytes((tm, K), x.dtype) + _nbytes((tm, tn), BF16)) + 2 * _weight_bytes(wg, K, tn) \
        + 4 * _nbytes((tm, tn), F32)
    return pl.pallas_call(
        _mm_swiglu_body,
        grid=(N // tn, steps),
        in_specs=[pl.BlockSpec((tm, K), lambda j, i: (row(i), 0)),
                  _weight_spec(layer, K, tn),
                  _weight_spec(layer, K, tn)],
        out_specs=pl.BlockSpec((tm, tn), lambda j, i: (row(i), j)),
        out_shape=jax.ShapeDtypeStruct((M, N), BF16),
        scratch_shapes=_weight_scratch(wg, K, tn) + _weight_scratch(wu, K, tn),
        compiler_params=_params(("parallel", "arbitrary"), est),
        name="matmul_swiglu",
    )(x, wg, wu)


def _mm_res_norm_proj_body(*refs, n_parts):
    x_refs = refs[:n_parts]
    w_ref, r_ref, g_ref, w2_ref, cs_ref, o_ref, q_ref = refs[n_parts:]
    acc = r_ref[...]
    k0 = 0
    for x_ref in x_refs:
        kw = x_ref.shape[1]
        acc = acc + jnp.dot(x_ref[...], w_ref[k0:k0 + kw, :], preferred_element_type=F32)
        k0 += kw
    o_ref[...] = acc
    ms = jnp.mean(acc * acc, axis=-1, keepdims=True)
    h = (acc * lax.rsqrt(ms + EPS) * g_ref[...]).astype(BF16)
    q_ref[...] = (jnp.dot(h, w2_ref[...], preferred_element_type=F32) * cs_ref[...]).astype(q_ref.dtype)


def _matmul_residual_norm_proj(x_parts, w, layer, r, gain, w2, layer2, colscale, tm=512):
    M, N = r.shape
    K = w.shape[1]
    N2 = w2.shape[2]
    tm = _pick_tile(M, tm)
    est = 2 * (_nbytes((tm, K), BF16) + 2 * _nbytes((tm, N), F32) + _nbytes((tm, N2), BF16)) \
        + _nbytes((K, N), BF16) + _nbytes((N, N2), BF16) + 4 * _nbytes((tm, N), F32)
    resident = pl.Buffered(1)
    return pl.pallas_call(
        functools.partial(_mm_res_norm_proj_body, n_parts=len(x_parts)),
        grid=(M // tm,),
        in_specs=[pl.BlockSpec((tm, xp.shape[1]), lambda i: (i, 0)) for xp in x_parts]
        + [pl.BlockSpec((None, K, N), lambda i: (layer, 0, 0), pipeline_mode=resident),
           pl.BlockSpec((tm, N), lambda i: (i, 0)),
           pl.BlockSpec((1, N), lambda i: (0, 0)),
           pl.BlockSpec((None, N, N2), lambda i: (layer2, 0, 0), pipeline_mode=resident),
           pl.BlockSpec((1, N2), lambda i: (0, 0))],
        out_specs=[pl.BlockSpec((tm, N), lambda i: (i, 0)),
                   pl.BlockSpec((tm, N2), lambda i: (i, 0))],
        out_shape=[jax.ShapeDtypeStruct((M, N), F32),
                   jax.ShapeDtypeStruct((M, N2), BF16)],
        compiler_params=_params(("parallel",), est),
        name="matmul_residual_norm_proj",
    )(*x_parts, w, r, gain.reshape(1, N).astype(F32), w2, colscale.reshape(1, N2).astype(F32))


def _pool_body(u_ref, halo_ref, w_ref, sc_ref, o_ref, ext_ref, *, tt, group):
    t = pl.program_id(1)
    u = u_ref[0].astype(F32)
    ext_ref[0:POOL_HALO, :] = jnp.where(t > 0, halo_ref[0].astype(F32), 0.0)
    ext_ref[POOL_HALO:POOL_HALO + tt, :] = u
    pos = t * tt + lax.broadcasted_iota(jnp.int32, (tt, 1), 0)
    for gi, win in enumerate(POOL_WINDOWS):
        c0, c1 = gi * group, (gi + 1) * group
        tok = u[:, c0:c1]
        lvl = ext_ref[:, c0:c1]
        k = 1
        while k < win:
            lvl = lvl + pltpu.roll(lvl, k, axis=0)
            k *= 2
        acc = lvl[POOL_HALO:POOL_HALO + tt]
        cnt = jnp.minimum(pos + 1, win).astype(F32)
        p = acc / cnt - tok
        y = jnp.dot(p.astype(BF16), w_ref[gi], preferred_element_type=F32)
        o_ref[0, :, c0:c1] = (y * sc_ref[:, c0:c1]).astype(o_ref.dtype)


def _pool_mixer(z, w_pool, scale, mix_a, tt=1024):
    B, T, _ = z.shape
    tt = min(tt, T)
    group = mix_a // len(POOL_WINDOWS)
    halo_blocks = tt // POOL_HALO
    est = 2 * (_nbytes((tt, mix_a), F32) + _nbytes((tt, mix_a), BF16)) + 4 * _nbytes((tt, mix_a), F32)
    return pl.pallas_call(
        functools.partial(_pool_body, tt=tt, group=group),
        grid=(B, T // tt),
        in_specs=[pl.BlockSpec((1, tt, mix_a), lambda b, t: (b, t, 0)),
                  pl.BlockSpec((1, POOL_HALO, mix_a),
                               lambda b, t: (b, jnp.maximum(t * halo_blocks - 1, 0), 0)),
                  pl.BlockSpec((len(POOL_WINDOWS), group, group), lambda b, t: (0, 0, 0)),
                  pl.BlockSpec((1, mix_a), lambda b, t: (0, 0))],
        out_specs=pl.BlockSpec((1, tt, mix_a), lambda b, t: (b, t, 0)),
        out_shape=jax.ShapeDtypeStruct((B, T, mix_a), BF16),
        scratch_shapes=[pltpu.VMEM((tt + POOL_HALO, mix_a), F32)],
        compiler_params=_params(("parallel", "parallel"), est),
        name="pool_mixer",
    )(z, z, w_pool, scale.reshape(1, mix_a).astype(F32))


def _hgrn_subtile(qf, kk, v, logf, st):
    n = qf.shape[0]
    C, R = HG_CHUNK, HG_DIAG
    row = lax.broadcasted_iota(jnp.int32, (n, HG_HEAD), 0)

    pos = row & (C - 1)
    b = logf
    sh = 1
    while sh < C:
        b = b + jnp.where(pos >= sh, pltpu.roll(b, sh, axis=0), 0.0)
        sh *= 2

    nb = n // R
    b3, q3, c3, v3 = (a.reshape(nb, R, HG_HEAD) for a in (b, qf, b - jnp.log2(kk), v))
    tpos = lax.broadcasted_iota(jnp.int32, (1, R, 1), 1)
    od = jnp.zeros((nb, R, HG_HEAD), F32)
    for s in range(R):
        w = q3 * jnp.exp2(b3 - c3[:, s:s + 1, :])
        a = jnp.sum(w, axis=-1, keepdims=True)
        a = jnp.where(tpos >= s, a, 0.0)
        od = od + a * v3[:, s:s + 1, :]
    o = od.reshape(n, HG_HEAD)

    same = lax.broadcasted_iota(jnp.int32, (n, n), 0) ^ lax.broadcasted_iota(jnp.int32, (n, n), 1)
    a_off = jnp.zeros((n, n), F32)
    h = R
    while h < C:
        g = 2 * h
        ref = b.reshape(n // g, g, HG_HEAD)[:, h - 1:h, :]
        ref = jnp.broadcast_to(ref, (n // g, g, HG_HEAD)).reshape(n, HG_HEAD)
        e = jnp.exp2(-jnp.abs(b - ref))
        right = (row & (g - 1)) >= h
        qt = jnp.where(right, qf * e, 0.0).astype(BF16)
        kt = jnp.where(right, 0.0, kk * e).astype(BF16)
        a = lax.dot_general(qt, kt, (((1,), (1,)), ((), ())), preferred_element_type=F32)
        a_off = a_off + jnp.where(same < g, a, 0.0)
        h = g
    v_bf = v.astype(BF16)
    o = o + jnp.dot(a_off.astype(BF16), v_bf, preferred_element_type=F32)

    nc = n // C
    bc = b.reshape(nc, C, HG_HEAD)
    b_last = bc[:, C - 1:C, :]
    q_in = (qf * jnp.exp2(b)).astype(BF16)
    k_out = (kk.reshape(nc, C, HG_HEAD) * jnp.exp2(b_last - bc)).astype(BF16)
    dec = jnp.exp2(b_last)
    pieces = []
    for c in range(nc):
        rows = slice(c * C, (c + 1) * C)
        o_int = lax.dot_general(q_in[rows], st.astype(BF16), (((1,), (1,)), ((), ())),
                                preferred_element_type=F32)
        pieces.append(o[rows] + o_int)
        upd = lax.dot_general(v_bf[rows], k_out[c], (((0,), (0,)), ((), ())),
                              preferred_element_type=F32)
        st = st * dec[c] + upd
    return jnp.concatenate(pieces, axis=0), st


def _hgrn_body(lbt_ref, ng_ref, q_ref, f_ref, i_ref, g_ref, o_ref, st_ref, *, layer, tt, sub):
    t = pl.program_id(2)

    @pl.when(t == 0)
    def _():
        st_ref[...] = jnp.zeros_like(st_ref)

    lbt = lbt_ref[...]
    e = jnp.exp(lbt - jnp.max(lbt, axis=0, keepdims=True))
    sm = e / jnp.sum(e, axis=0, keepdims=True)
    lb = jnp.sum(sm[1:layer + 2], axis=0, keepdims=True)

    st = st_ref[...]
    for c in range(tt // sub):
        rows = slice(c * sub, (c + 1) * sub)
        f = lb + (1.0 - lb) * jax.nn.sigmoid(f_ref[0, rows, :].astype(F32))
        q = q_ref[0, rows, :].astype(F32)
        qf = q * jax.nn.sigmoid(q) * (HG_HEAD ** -0.5)
        o, st = _hgrn_subtile(qf, 1.0 - f, i_ref[0, rows, :].astype(F32), jnp.log2(f), st)
        ms = jnp.mean(o * o, axis=-1, keepdims=True)
        o = o * lax.rsqrt(ms + EPS) * ng_ref[...]
        g = g_ref[0, rows, :].astype(F32)
        o_ref[0, rows, :] = (o * (g * jax.nn.sigmoid(g))).astype(o_ref.dtype)
    st_ref[...] = st


def _hgrn_mixer(z, lb_table, norm_g, layer, mix_a, mix_b, tt=2048):
    B, T, _ = z.shape
    tt = min(tt, T)
    sub = min(HG_SUB, tt)
    heads = mix_b // HG_HEAD
    c0 = mix_a // HG_HEAD
    est = 2 * 5 * _nbytes((tt, HG_HEAD), F32) + 32 * _nbytes((sub, HG_HEAD), F32) + 6 * _nbytes((sub, sub), F32)

    def col(off):
        return pl.BlockSpec((1, tt, HG_HEAD), lambda b, h, t, off=off: (b, t, c0 + off * heads + h))

    return pl.pallas_call(
        functools.partial(_hgrn_body, layer=layer, tt=tt, sub=sub),
        grid=(B, heads, T // tt),
        in_specs=[pl.BlockSpec((lb_table.shape[0], HG_HEAD), lambda b, h, t: (0, h)),
                  pl.BlockSpec((1, HG_HEAD), lambda b, h, t: (0, 0)),
                  col(0), col(1), col(2), col(3)],
        out_specs=pl.BlockSpec((1, tt, HG_HEAD), lambda b, h, t: (b, t, h)),
        out_shape=jax.ShapeDtypeStruct((B, T, mix_b), BF16),
        scratch_shapes=[pltpu.VMEM((HG_HEAD, HG_HEAD), F32)],
        compiler_params=_params(("parallel", "parallel", "arbitrary"), est),
        name="hgrn2_mixer",
    )(lb_table.astype(F32), norm_g.reshape(1, HG_HEAD).astype(F32), z, z, z, z)


def _fgate_body(h_ref, w_ref, b_ref, *rest, tt, norm_width):
    if norm_width:
        ss_ref, o_ref, carry_ref = rest
    else:
        o_ref, carry_ref = rest
    t = pl.program_id(1)

    @pl.when(t == 0)
    def _():
        carry_ref[...] = jnp.zeros_like(carry_ref)

    fl = jnp.dot(h_ref[0], w_ref[...], preferred_element_type=F32)
    if norm_width:
        fl = fl * _row_rsqrt(ss_ref, norm_width)
    fl = fl + b_ref[...]
    c = (jnp.minimum(fl, 0.0) - jnp.log1p(jnp.exp(-jnp.abs(fl)))) * LOG2E
    row = lax.broadcasted_iota(jnp.int32, c.shape, 0)
    sh = 1
    while sh < tt:
        c = c + jnp.where(row >= sh, pltpu.roll(c, sh, axis=0), 0.0)
        sh *= 2
    c = c + carry_ref[...]
    o_ref[0] = c
    carry_ref[...] = c[tt - 1:tt, :]


def _fox_gates(h, wf, bf, row_ss=None, tt=512):
    B, T, D = h.shape
    tt = min(tt, T)
    ss_specs = [] if row_ss is None else [pl.BlockSpec((row_ss.shape[0], None, tt, 1), lambda b, t: (0, b, t, 0))]
    ss_args = [] if row_ss is None else [row_ss]
    est = 2 * (_nbytes((tt, D), BF16) + _nbytes((D, LANES), BF16) + _nbytes((tt, LANES), F32)) \
        + 8 * _nbytes((tt, LANES), F32)
    return pl.pallas_call(
        functools.partial(_fgate_body, tt=tt, norm_width=0 if row_ss is None else D),
        grid=(B, T // tt),
        in_specs=[pl.BlockSpec((1, tt, D), lambda b, t: (b, t, 0)),
                  pl.BlockSpec((D, LANES), lambda b, t: (0, 0)),
                  pl.BlockSpec((1, LANES), lambda b, t: (0, 0))] + ss_specs,
        out_specs=pl.BlockSpec((1, tt, LANES), lambda b, t: (b, t, 0)),
        out_shape=jax.ShapeDtypeStruct((B, T, LANES), F32),
        scratch_shapes=[pltpu.VMEM((1, LANES), F32)],
        compiler_params=_params(("parallel", "arbitrary"), est),
        name="fox_gates",
    )(h, wf, bf, *ss_args)


def _fox_body(q_ref, k_ref, v_ref, f_ref, fr_ref, o_ref, vt_ref, fk_ref, p_ref, acc_ref, *, tq, seq, group):
    hg = pl.program_id(1)
    i = pl.program_id(2)
    reps = tq // LANES
    heads = range(group)

    def head_cols(g):
        return slice(g * FOX_HEAD, (g + 1) * FOX_HEAD)

    @pl.when(i == 0)
    def _():
        for c in range(seq // tq):
            rows = slice(c * tq, (c + 1) * tq)
            fblk = f_ref[0, rows, :]
            lane = lax.broadcasted_iota(jnp.int32, fblk.shape, 1)
            for g in heads:
                vt_ref[g, :, rows] = v_ref[0, rows, head_cols(g)].T
                col = jnp.sum(jnp.where(lane == hg * group + g, fblk, 0.0), axis=-1, keepdims=True)
                fk_ref[g, rows, :] = jnp.broadcast_to(col, fblk.shape)

    qt = [q_ref[0, :, head_cols(g)].T for g in heads]
    fq = [fr_ref[0, g, i] for g in heads]

    def scores(g, j):
        start = pl.multiple_of(j * tq, tq)
        return jnp.dot(k_ref[0, pl.ds(start, tq), head_cols(g)], qt[g], preferred_element_type=F32)

    def softmax_block(g, j, st, m_prev, l_prev, diagonal):
        start = pl.multiple_of(j * tq, tq)
        t = st - jnp.concatenate([fk_ref[g, pl.ds(start, tq), :]] * reps, axis=1)
        if diagonal:
            key = lax.broadcasted_iota(jnp.int32, (tq, tq), 0)
            qry = lax.broadcasted_iota(jnp.int32, (tq, tq), 1)
            t = jnp.where(key <= qry, t, -jnp.inf)
        m_new = jnp.maximum(m_prev, jnp.max(t, axis=0, keepdims=True) + fq[g])
        alpha = jnp.exp2(m_prev - m_new)
        p = jnp.exp2(t - (m_new - fq[g]))
        l_new = alpha * l_prev + jnp.sum(p, axis=0, keepdims=True)
        return p.astype(BF16), m_new, l_new, alpha

    def weighted_values(g, j, p):
        start = pl.multiple_of(j * tq, tq)
        return jnp.dot(vt_ref[g, :, pl.ds(start, tq)], p, preferred_element_type=F32)

    def block(j, j_prev, stats, first):
        out = []
        for g in heads:
            m, l = stats[g]
            st = scores(g, j)
            if first:
                p, m, l, _ = softmax_block(g, j, st, m, l, True)
                acc_ref[g] = jnp.zeros((FOX_HEAD, tq), F32)
            else:
                pv_prev = weighted_values(g, j_prev, p_ref[g])
                p, m, l, alpha = softmax_block(g, j, st, m, l, False)
                acc_ref[g] = (acc_ref[g] + pv_prev) * alpha
            p_ref[g] = p
            out.append((m, l))
        return tuple(out)

    stats = tuple((jnp.full((1, tq), -jnp.inf, F32), jnp.zeros((1, tq), F32)) for _ in heads)
    stats = block(i, i, stats, True)
    odd = i % 2
    stats = lax.cond(odd == 1, lambda c: block(0, i, c, False), lambda c: c, stats)

    def pair(jj, c):
        j0 = odd + 2 * jj
        c = block(j0, jnp.where(j0 == 0, i, j0 - 1), c, False)
        return block(j0 + 1, j0, c, False)

    stats = lax.fori_loop(0, i // 2, pair, stats)
    j_last = jnp.where(i == 0, i, i - 1)
    for g in heads:
        acc = acc_ref[g] + weighted_values(g, j_last, p_ref[g])
        o_ref[0, :, head_cols(g)] = (acc / stats[g][1]).astype(o_ref.dtype).T


def _fox_attention(qkv, f, frow, heads, tq, group=FOX_GROUP):
    B, T, _ = qkv.shape
    nq = T // tq
    gw = group * FOX_HEAD
    ngrp = heads // group
    est = 2 * (2 * _nbytes((T, gw), BF16) + 2 * _nbytes((tq, gw), BF16)
               + _nbytes((T, LANES), F32) + group * _nbytes((nq, 8, tq), F32)) \
        + group * (_nbytes((T, FOX_HEAD), BF16) + _nbytes((T, LANES), F32) + 5 * _nbytes((tq, tq), F32))
    return pl.pallas_call(
        functools.partial(_fox_body, tq=tq, seq=T, group=group),
        grid=(B, ngrp, nq),
        in_specs=[pl.BlockSpec((1, tq, gw), lambda b, h, i: (b, i, h)),
                  pl.BlockSpec((1, T, gw), lambda b, h, i: (b, 0, ngrp + h)),
                  pl.BlockSpec((1, T, gw), lambda b, h, i: (b, 0, 2 * ngrp + h)),
                  pl.BlockSpec((1, T, LANES), lambda b, h, i: (b, 0, 0)),
                  pl.BlockSpec((1, group, nq, 1, tq), lambda b, h, i: (b, h, 0, 0, 0))],
        out_specs=pl.BlockSpec((1, tq, gw), lambda b, h, i: (b, i, h)),
        out_shape=jax.ShapeDtypeStruct((B, T, heads * FOX_HEAD), BF16),
        scratch_shapes=[pltpu.VMEM((group, FOX_HEAD, T), BF16),
                        pltpu.VMEM((group, T, LANES), F32),
                        pltpu.VMEM((group, tq, tq), BF16),
                        pltpu.VMEM((group, FOX_HEAD, tq), F32)],
        compiler_params=_params(("parallel", "parallel", "arbitrary"), est),
        name="fox_attention",
    )(qkv, qkv, qkv, f, frow)


def _xattn_out_body(q_ref, kv_ref, w_ref, r_ref, g_ref, o_ref, h_ref, a_ref, *, d_model):
    hd = d_model // XA_HEADS
    for hh in range(XA_HEADS):
        q = q_ref[:, hh * hd:(hh + 1) * hd]
        k = kv_ref[:, hh * hd:(hh + 1) * hd]
        v = kv_ref[:, d_model + hh * hd:d_model + (hh + 1) * hd]
        s = lax.dot_general(q, k, (((1,), (1,)), ((), ())), preferred_element_type=F32)
        p = jnp.exp(s - jnp.max(s, axis=-1, keepdims=True))
        p = p / jnp.sum(p, axis=-1, keepdims=True)
        a_ref[:, hh * hd:(hh + 1) * hd] = jnp.dot(p.astype(BF16), v, preferred_element_type=F32).astype(a_ref.dtype)
    acc = r_ref[...] + jnp.dot(a_ref[...], w_ref[...], preferred_element_type=F32)
    o_ref[...] = acc
    ms = jnp.mean(acc * acc, axis=-1, keepdims=True)
    h_ref[...] = (acc * lax.rsqrt(ms + EPS) * g_ref[...]).astype(h_ref.dtype)


def _xattn_out_norm(q, kv, w, layer, r, gain, seq, tm=512):
    M, D = q.shape
    n_mem = kv.shape[1]
    tm = _pick_tile(seq, tm)
    per_batch = seq // tm
    est = 2 * (_nbytes((tm, D), BF16) + _nbytes((n_mem, 2 * D), BF16) + _nbytes((D, D), BF16)
               + 2 * _nbytes((tm, D), F32) + _nbytes((tm, D), BF16)) \
        + _nbytes((tm, D), BF16) + 3 * _nbytes((tm, D), F32)
    row = pl.BlockSpec((tm, D), lambda i: (i, 0))
    return pl.pallas_call(
        functools.partial(_xattn_out_body, d_model=D),
        grid=(M // tm,),
        in_specs=[row,
                  pl.BlockSpec((None, n_mem, 2 * D), lambda i: (i // per_batch, 0, 0)),
                  pl.BlockSpec((None, D, D), lambda i: (layer, 0, 0)),
                  row,
                  pl.BlockSpec((1, D), lambda i: (0, 0))],
        out_specs=[row, row],
        out_shape=[jax.ShapeDtypeStruct((M, D), F32),
                   jax.ShapeDtypeStruct((M, D), BF16)],
        scratch_shapes=[pltpu.VMEM((tm, D), BF16)],
        compiler_params=_params(("parallel",), est),
        name="xattn_out_norm",
    )(q, kv, w, r, gain.reshape(1, D).astype(F32))


def kernel(x, mem, lb_table, ev_norm, ev_w_in, ev_w_pool, ev_pool_scale, ev_hg_norm, ev_w_out,
           od_norm, od_w_in, od_b_f, od_w_out, xa_norm, xa_mem_norm, xa_wq, xa_wkv, xa_wo,
           ffn_norm, ffn_w_gate, ffn_w_up, ffn_w_down, final_norm):
    B, T, D = x.shape
    M = B * T
    depth = xa_norm.shape[0]
    mix_a = ev_pool_scale.shape[1]
    mix_b = lb_table.shape[1]
    fox_heads = od_b_f.shape[1]
    n_mem = mem.shape[1]
    xa_scale = (D // XA_HEADS) ** -0.5
    fox_tq = _pick_tile(T, 512)

    xs = x.reshape(M, D)
    mem2 = mem.reshape(B * n_mem, D)
    w_down = ffn_w_down.astype(BF16)
    ev_wo, od_wo, xa_wo_b = ev_w_out.astype(BF16), od_w_out.astype(BF16), xa_wo.astype(BF16)
    xa_wq_b = xa_wq.astype(BF16)
    xa_qscale = jnp.full((D,), xa_scale, F32)
    pre = None
    for l in range(depth):
        if pre is None:
            h, ss = _rmsnorm(xs, ev_norm[l // 2] if l % 2 == 0 else od_norm[l // 2], BF16), None
        else:
            h, ss = pre
        if l % 2 == 0:
            e = l // 2
            z = _matmul(h, ev_w_in, e, ev_w_in.shape[2], BF16, row_ss=ss).reshape(B, T, -1)
            ya = _pool_mixer(z, ev_w_pool[e].astype(BF16), ev_pool_scale[e], mix_a)
            yb = _hgrn_mixer(z, lb_table, ev_hg_norm[e], l, mix_a, mix_b)
            xs, q = _matmul_residual_norm_proj([ya.reshape(M, mix_a), yb.reshape(M, mix_b)], ev_wo, e, xs, xa_norm[l],
                                               xa_wq_b, l, xa_qscale)
        else:
            o = l // 2
            qscale = jnp.concatenate([jnp.full((D,), LOG2E * FOX_HEAD ** -0.5, F32), jnp.ones((2 * D,), F32)])
            w_in = od_w_in.astype(BF16)
            qkv = _matmul(h, w_in, o, 3 * D, BF16, colscale=qscale, row_ss=ss).reshape(B, T, 3 * D)
            wf = jnp.zeros((D, LANES), BF16).at[:, :fox_heads].set(w_in[o, :, 3 * D:])
            bf = jnp.zeros((1, LANES), F32).at[0, :fox_heads].set(od_b_f[o].astype(F32))
            ss4 = None if ss is None else ss.reshape(ss.shape[0], B, T, 1)
            f = _fox_gates(h.reshape(B, T, D), wf, bf, ss4)
            frow = f[..., :fox_heads].transpose(0, 2, 1).reshape(B, fox_heads, T // fox_tq, 1, fox_tq)
            y = _fox_attention(qkv, f, frow, fox_heads, fox_tq).reshape(M, D)
            xs, q = _matmul_residual_norm_proj([y], od_wo, o, xs, xa_norm[l], xa_wq_b, l, xa_qscale)

        mn = _rmsnorm(mem2, xa_mem_norm[l], BF16)
        kv = _matmul(mn, xa_wkv, l, 2 * D, BF16)
        xs, h = _xattn_out_norm(q, kv.reshape(B, n_mem, 2 * D), xa_wo_b, l, xs, ffn_norm[l], T)

        act = _matmul_swiglu(h, ffn_w_gate, ffn_w_up, l)
        if l + 1 < depth:
            nxt = l + 1
            gain = ev_norm[nxt // 2] if nxt % 2 == 0 else od_norm[nxt // 2]
            xs, hp, ss = _matmul_residual(act, w_down, l, xs, gain=gain)
            pre = (hp, ss)
        else:
            xs = _matmul_residual(act, w_down, l, xs)
    return _rmsnorm(xs, final_norm, x.dtype).reshape(B, T, D)
```

```python
import functools

import jax
import jax.numpy as jnp
from jax import lax
from jax.experimental import pallas as pl
from jax.experimental.pallas import tpu as pltpu

F32 = jnp.float32
BF16 = jnp.bfloat16

EPS = 1e-6
POOL_WINDOWS = (2, 4, 8, 16)
POOL_HALO = 16
assert all(w & (w - 1) == 0 and w <= POOL_HALO for w in POOL_WINDOWS)
HG_HEAD = 128
HG_CHUNK = 64
HG_DIAG = 8
HG_SUB = 128
FOX_HEAD = 128
FOX_GROUP = 4
XA_HEADS = 4

LOG2E = 1.4426950408889634
LANES = 128
VMEM_CAP = 56 * 1024 * 1024
VMEM_FLOOR = 32 * 1024 * 1024


def _params(semantics, vmem_estimate):
    limit = int(min(max(vmem_estimate * 5 // 4, VMEM_FLOOR), VMEM_CAP))
    return pltpu.CompilerParams(dimension_semantics=semantics, vmem_limit_bytes=limit)


def _nbytes(shape, dtype):
    n = jnp.dtype(dtype).itemsize
    for s in shape:
        n *= s
    return n


def _rmsnorm_body(x_ref, g_ref, o_ref):
    x = x_ref[...]
    ms = jnp.mean(x * x, axis=-1, keepdims=True)
    o_ref[...] = (x * lax.rsqrt(ms + EPS) * g_ref[...]).astype(o_ref.dtype)


def _rmsnorm(x, g, out_dtype, tm=512):
    M, D = x.shape
    tm = min(tm, M)
    est = 2 * (_nbytes((tm, D), x.dtype) + _nbytes((tm, D), out_dtype)) + 3 * _nbytes((tm, D), F32)
    return pl.pallas_call(
        _rmsnorm_body,
        grid=(M // tm,),
        in_specs=[pl.BlockSpec((tm, D), lambda i: (i, 0)),
                  pl.BlockSpec((1, D), lambda i: (0, 0))],
        out_specs=pl.BlockSpec((tm, D), lambda i: (i, 0)),
        out_shape=jax.ShapeDtypeStruct((M, D), out_dtype),
        compiler_params=_params(("parallel",), est),
        name="rmsnorm",
    )(x, g.reshape(1, D).astype(F32))


def _with_bf16_weights(w_refs, wb_refs, compute):
    if not wb_refs:
        compute(*(w[...] for w in w_refs))
        return
    i = pl.program_id(1)

    @pl.when(i == 0)
    def _():
        for w, wb in zip(w_refs, wb_refs):
            wb[...] = w[...].astype(BF16)

    @pl.when(i > 0)
    def _():
        compute(*(wb[...] for wb in wb_refs))


def _row_rsqrt(ss_ref, width):
    return lax.rsqrt(jnp.sum(ss_ref[...], axis=0) * (1.0 / width) + EPS)


def _mm_body(x_ref, w_ref, cs_ref, *rest, norm_width):
    if norm_width:
        ss_ref, o_ref, *wb = rest
    else:
        o_ref, *wb = rest

    def compute(w):
        acc = jnp.dot(x_ref[...], w, preferred_element_type=F32)
        if norm_width:
            acc = acc * _row_rsqrt(ss_ref, norm_width)
        o_ref[...] = (acc * cs_ref[...]).astype(o_ref.dtype)

    _with_bf16_weights((w_ref,), wb, compute)


def _mm_res_body(x_ref, w_ref, r_ref, *rest, with_gain):
    if with_gain:
        g_ref, o_ref, hp_ref, ss_ref, *wb = rest
    else:
        o_ref, *wb = rest

    def compute(w):
        acc = r_ref[...] + jnp.dot(x_ref[...], w, preferred_element_type=F32)
        o_ref[...] = acc
        if with_gain:
            hp_ref[...] = (acc * g_ref[...]).astype(hp_ref.dtype)
            ss_ref[...] = jnp.sum(acc * acc, axis=-1, keepdims=True)

    _with_bf16_weights((w_ref,), wb, compute)


def _mm_swiglu_body(x_ref, wg_ref, wu_ref, o_ref, *wb):
    def compute(wg, wu):
        x = x_ref[...]
        g = jnp.dot(x, wg, preferred_element_type=F32)
        u = jnp.dot(x, wu, preferred_element_type=F32)
        o_ref[...] = (g * jax.nn.sigmoid(g) * u).astype(o_ref.dtype)

    _with_bf16_weights((wg_ref, wu_ref), wb, compute)


def _pick_tile(n, pref):
    t = min(pref, n)
    while n % t:
        t //= 2
    return t


def _weight_spec(layer, K, tn):
    return pl.BlockSpec((None, K, tn), lambda j, i: (layer, 0, j))


def _weight_scratch(w, K, tn):
    return [] if w.dtype == BF16 else [pltpu.VMEM((K, tn), BF16)]


def _weight_bytes(w, K, tn):
    return 2 * _nbytes((K, tn), w.dtype) + (0 if w.dtype == BF16 else _nbytes((K, tn), BF16))


def _row_steps(w, M, tm):
    extra = 0 if w.dtype == BF16 else 1
    return M // tm + extra, lambda i: jnp.maximum(i - extra, 0)


def _matmul(x, w, layer, n_cols, out_dtype, colscale=None, row_ss=None, tm=1024, tn=1024):
    M, K = x.shape
    tm, tn = _pick_tile(M, tm), _pick_tile(n_cols, tn)
    if colscale is None:
        colscale = jnp.ones((n_cols,), F32)
    steps, row = _row_steps(w, M, tm)
    ss_specs = [] if row_ss is None else [pl.BlockSpec((row_ss.shape[0], tm, 1), lambda j, i: (0, row(i), 0))]
    ss_args = [] if row_ss is None else [row_ss]
    est = 2 * (_nbytes((tm, K), x.dtype) + _nbytes((tm, tn), out_dtype)) + _weight_bytes(w, K, tn) \
        + 2 * _nbytes((tm, tn), F32)
    return pl.pallas_call(
        functools.partial(_mm_body, norm_width=0 if row_ss is None else K),
        grid=(n_cols // tn, steps),
        in_specs=[pl.BlockSpec((tm, K), lambda j, i: (row(i), 0)),
                  _weight_spec(layer, K, tn),
                  pl.BlockSpec((1, tn), lambda j, i: (0, j))] + ss_specs,
        out_specs=pl.BlockSpec((tm, tn), lambda j, i: (row(i), j)),
        out_shape=jax.ShapeDtypeStruct((M, n_cols), out_dtype),
        scratch_shapes=_weight_scratch(w, K, tn),
        compiler_params=_params(("parallel", "arbitrary"), est),
        name="matmul",
    )(x, w, colscale.reshape(1, n_cols), *ss_args)


def _matmul_residual(x, w, layer, r, gain=None, tn=1024):
    M, K = x.shape
    N = w.shape[2]
    tm = 1024 if K <= 2048 else 512
    tm, tn = _pick_tile(M, tm), _pick_tile(N, tn)
    steps, row = _row_steps(w, M, tm)
    est = 2 * (_nbytes((tm, K), x.dtype) + 2 * _nbytes((tm, tn), F32)) + _weight_bytes(w, K, tn) \
        + 2 * _nbytes((tm, tn), F32)
    tile = pl.BlockSpec((tm, tn), lambda j, i: (row(i), j))
    in_specs = [pl.BlockSpec((tm, K), lambda j, i: (row(i), 0)), _weight_spec(layer, K, tn), tile]
    out_specs, out_shape, args = tile, jax.ShapeDtypeStruct((M, N), F32), [x, w, r]
    if gain is not None:
        in_specs.append(pl.BlockSpec((1, tn), lambda j, i: (0, j)))
        args.append(gain.reshape(1, N).astype(F32))
        out_specs = [tile, tile, pl.BlockSpec((None, tm, 1), lambda j, i: (j, row(i), 0))]
        out_shape = [out_shape, jax.ShapeDtypeStruct((M, N), BF16), jax.ShapeDtypeStruct((N // tn, M, 1), F32)]
    return pl.pallas_call(
        functools.partial(_mm_res_body, with_gain=gain is not None),
        grid=(N // tn, steps),
        in_specs=in_specs,
        out_specs=out_specs,
        out_shape=out_shape,
        scratch_shapes=_weight_scratch(w, K, tn),
        compiler_params=_params(("parallel", "arbitrary"), est),
        name="matmul_residual",
    )(*args)


def _matmul_swiglu(x, wg, wu, layer, tm=1024, tn=512):
    M, K = x.shape
    N = wg.shape[2]
    tm, tn = _pick_tile(M, tm), _pick_tile(N, tn)
    steps, row = _row_steps(wg, M, tm)
    est = 2 * (_nbytes((tm, K), x.dtype) + _nbytes((tm, tn), BF16)) + 2 * _weight_bytes(wg, K, tn) \
        + 4 * _nbytes((tm, tn), F32)
    return pl.pallas_call(
        _mm_swiglu_body,
        grid=(N // tn, steps),
        in_specs=[pl.BlockSpec((tm, K), lambda j, i: (row(i), 0)),
                  _weight_spec(layer, K, tn),
                  _weight_spec(layer, K, tn)],
        out_specs=pl.BlockSpec((tm, tn), lambda j, i: (row(i), j)),
        out_shape=jax.ShapeDtypeStruct((M, N), BF16),
        scratch_shapes=_weight_scratch(wg, K, tn) + _weight_scratch(wu, K, tn),
        compiler_params=_params(("parallel", "arbitrary"), est),
        name="matmul_swiglu",
    )(x, wg, wu)


def _mm_res_norm_proj_body(*refs, n_parts):
    x_refs = refs[:n_parts]
    w_ref, r_ref, g_ref, w2_ref, cs_ref, o_ref, q_ref = refs[n_parts:]
    acc = r_ref[...]
    k0 = 0
    for x_ref in x_refs:
        kw = x_ref.shape[1]
        acc = acc + jnp.dot(x_ref[...], w_ref[k0:k0 + kw, :], preferred_element_type=F32)
        k0 += kw
    o_ref[...] = acc
    ms = jnp.mean(acc * acc, axis=-1, keepdims=True)
    h = (acc * lax.rsqrt(ms + EPS) * g_ref[...]).astype(BF16)
    q_ref[...] = (jnp.dot(h, w2_ref[...], preferred_element_type=F32) * cs_ref[...]).astype(q_ref.dtype)


def _matmul_residual_norm_proj(x_parts, w, layer, r, gain, w2, layer2, colscale, tm=512):
    M, N = r.shape
    K = w.shape[1]
    N2 = w2.shape[2]
    tm = _pick_tile(M, tm)
    est = 2 * (_nbytes((tm, K), BF16) + 2 * _nbytes((tm, N), F32) + _nbytes((tm, N2), BF16)) \
        + _nbytes((K, N), BF16) + _nbytes((N, N2), BF16) + 4 * _nbytes((tm, N), F32)
    resident = pl.Buffered(1)
    return pl.pallas_call(
        functools.partial(_mm_res_norm_proj_body, n_parts=len(x_parts)),
        grid=(M // tm,),
        in_specs=[pl.BlockSpec((tm, xp.shape[1]), lambda i: (i, 0)) for xp in x_parts]
        + [pl.BlockSpec((None, K, N), lambda i: (layer, 0, 0), pipeline_mode=resident),
           pl.BlockSpec((tm, N), lambda i: (i, 0)),
           pl.BlockSpec((1, N), lambda i: (0, 0)),
           pl.BlockSpec((None, N, N2), lambda i: (layer2, 0, 0), pipeline_mode=resident),
           pl.BlockSpec((1, N2), lambda i: (0, 0))],
        out_specs=[pl.BlockSpec((tm, N), lambda i: (i, 0)),
                   pl.BlockSpec((tm, N2), lambda i: (i, 0))],
        out_shape=[jax.ShapeDtypeStruct((M, N), F32),
                   jax.ShapeDtypeStruct((M, N2), BF16)],
        compiler_params=_params(("parallel",), est),
        name="matmul_residual_norm_proj",
    )(*x_parts, w, r, gain.reshape(1, N).astype(F32), w2, colscale.reshape(1, N2).astype(F32))


def _pool_body(u_ref, halo_ref, w_ref, sc_ref, o_ref, ext_ref, *, tt, group):
    t = pl.program_id(1)
    u = u_ref[0].astype(F32)
    ext_ref[0:POOL_HALO, :] = jnp.where(t > 0, halo_ref[0].astype(F32), 0.0)
    ext_ref[POOL_HALO:POOL_HALO + tt, :] = u
    pos = t * tt + lax.broadcasted_iota(jnp.int32, (tt, 1), 0)
    for gi, win in enumerate(POOL_WINDOWS):
        c0, c1 = gi * group, (gi + 1) * group
        tok = u[:, c0:c1]
        lvl = ext_ref[:, c0:c1]
        k = 1
        while k < win:
            lvl = lvl + pltpu.roll(lvl, k, axis=0)
            k *= 2
        acc = lvl[POOL_HALO:POOL_HALO + tt]
        cnt = jnp.minimum(pos + 1, win).astype(F32)
        p = acc / cnt - tok
        y = jnp.dot(p.astype(BF16), w_ref[gi], preferred_element_type=F32)
        o_ref[0, :, c0:c1] = (y * sc_ref[:, c0:c1]).astype(o_ref.dtype)


def _pool_mixer(z, w_pool, scale, mix_a, tt=1024):
    B, T, _ = z.shape
    tt = min(tt, T)
    group = mix_a // len(POOL_WINDOWS)
    halo_blocks = tt // POOL_HALO
    est = 2 * (_nbytes((tt, mix_a), F32) + _nbytes((tt, mix_a), BF16)) + 4 * _nbytes((tt, mix_a), F32)
    return pl.pallas_call(
        functools.partial(_pool_body, tt=tt, group=group),
        grid=(B, T // tt),
        in_specs=[pl.BlockSpec((1, tt, mix_a), lambda b, t: (b, t, 0)),
                  pl.BlockSpec((1, POOL_HALO, mix_a),
                               lambda b, t: (b, jnp.maximum(t * halo_blocks - 1, 0), 0)),
                  pl.BlockSpec((len(POOL_WINDOWS), group, group), lambda b, t: (0, 0, 0)),
                  pl.BlockSpec((1, mix_a), lambda b, t: (0, 0))],
        out_specs=pl.BlockSpec((1, tt, mix_a), lambda b, t: (b, t, 0)),
        out_shape=jax.ShapeDtypeStruct((B, T, mix_a), BF16),
        scratch_shapes=[pltpu.VMEM((tt + POOL_HALO, mix_a), F32)],
        compiler_params=_params(("parallel", "parallel"), est),
        name="pool_mixer",
    )(z, z, w_pool, scale.reshape(1, mix_a).astype(F32))


def _hgrn_subtile(qf, kk, v, logf, st):
    n = qf.shape[0]
    C, R = HG_CHUNK, HG_DIAG
    row = lax.broadcasted_iota(jnp.int32, (n, HG_HEAD), 0)

    pos = row & (C - 1)
    b = logf
    sh = 1
    while sh < C:
        b = b + jnp.where(pos >= sh, pltpu.roll(b, sh, axis=0), 0.0)
        sh *= 2

    nb = n // R
    b3, q3, c3, v3 = (a.reshape(nb, R, HG_HEAD) for a in (b, qf, b - jnp.log2(kk), v))
    tpos = lax.broadcasted_iota(jnp.int32, (1, R, 1), 1)
    od = jnp.zeros((nb, R, HG_HEAD), F32)
    for s in range(R):
        w = q3 * jnp.exp2(b3 - c3[:, s:s + 1, :])
        a = jnp.sum(w, axis=-1, keepdims=True)
        a = jnp.where(tpos >= s, a, 0.0)
        od = od + a * v3[:, s:s + 1, :]
    o = od.reshape(n, HG_HEAD)

    same = lax.broadcasted_iota(jnp.int32, (n, n), 0) ^ lax.broadcasted_iota(jnp.int32, (n, n), 1)
    a_off = jnp.zeros((n, n), F32)
    h = R
    while h < C:
        g = 2 * h
        ref = b.reshape(n // g, g, HG_HEAD)[:, h - 1:h, :]
        ref = jnp.broadcast_to(ref, (n // g, g, HG_HEAD)).reshape(n, HG_HEAD)
        e = jnp.exp2(-jnp.abs(b - ref))
        right = (row & (g - 1)) >= h
        qt = jnp.where(right, qf * e, 0.0).astype(BF16)
        kt = jnp.where(right, 0.0, kk * e).astype(BF16)
        a = lax.dot_general(qt, kt, (((1,), (1,)), ((), ())), preferred_element_type=F32)
        a_off = a_off + jnp.where(same < g, a, 0.0)
        h = g
    v_bf = v.astype(BF16)
    o = o + jnp.dot(a_off.astype(BF16), v_bf, preferred_element_type=F32)

    nc = n // C
    bc = b.reshape(nc, C, HG_HEAD)
    b_last = bc[:, C - 1:C, :]
    q_in = (qf * jnp.exp2(b)).astype(BF16)
    k_out = (kk.reshape(nc, C, HG_HEAD) * jnp.exp2(b_last - bc)).astype(BF16)
    dec = jnp.exp2(b_last)
    pieces = []
    for c in range(nc):
        rows = slice(c * C, (c + 1) * C)
        o_int = lax.dot_general(q_in[rows], st.astype(BF16), (((1,), (1,)), ((), ())),
                                preferred_element_type=F32)
        pieces.append(o[rows] + o_int)
        upd = lax.dot_general(v_bf[rows], k_out[c], (((0,), (0,)), ((), ())),
                              preferred_element_type=F32)
        st = st * dec[c] + upd
    return jnp.concatenate(pieces, axis=0), st


def _hgrn_body(lbt_ref, ng_ref, q_ref, f_ref, i_ref, g_ref, o_ref, st_ref, *, layer, tt, sub):
    t = pl.program_id(2)

    @pl.when(t == 0)
    def _():
        st_ref[...] = jnp.zeros_like(st_ref)

    lbt = lbt_ref[...]
    e = jnp.exp(lbt - jnp.max(lbt, axis=0, keepdims=True))
    sm = e / jnp.sum(e, axis=0, keepdims=True)
    lb = jnp.sum(sm[1:layer + 2], axis=0, keepdims=True)

    st = st_ref[...]
    for c in range(tt // sub):
        rows = slice(c * sub, (c + 1) * sub)
        f = lb + (1.0 - lb) * jax.nn.sigmoid(f_ref[0, rows, :].astype(F32))
        q = q_ref[0, rows, :].astype(F32)
        qf = q * jax.nn.sigmoid(q) * (HG_HEAD ** -0.5)
        o, st = _hgrn_subtile(qf, 1.0 - f, i_ref[0, rows, :].astype(F32), jnp.log2(f), st)
        ms = jnp.mean(o * o, axis=-1, keepdims=True)
        o = o * lax.rsqrt(ms + EPS) * ng_ref[...]
        g = g_ref[0, rows, :].astype(F32)
        o_ref[0, rows, :] = (o * (g * jax.nn.sigmoid(g))).astype(o_ref.dtype)
    st_ref[...] = st


def _hgrn_mixer(z, lb_table, norm_g, layer, mix_a, mix_b, tt=2048):
    B, T, _ = z.shape
    tt = min(tt, T)
    sub = min(HG_SUB, tt)
    heads = mix_b // HG_HEAD
    c0 = mix_a // HG_HEAD
    est = 2 * 5 * _nbytes((tt, HG_HEAD), F32) + 32 * _nbytes((sub, HG_HEAD), F32) + 6 * _nbytes((sub, sub), F32)

    def col(off):
        return pl.BlockSpec((1, tt, HG_HEAD), lambda b, h, t, off=off: (b, t, c0 + off * heads + h))

    return pl.pallas_call(
        functools.partial(_hgrn_body, layer=layer, tt=tt, sub=sub),
        grid=(B, heads, T // tt),
        in_specs=[pl.BlockSpec((lb_table.shape[0], HG_HEAD), lambda b, h, t: (0, h)),
                  pl.BlockSpec((1, HG_HEAD), lambda b, h, t: (0, 0)),
                  col(0), col(1), col(2), col(3)],
        out_specs=pl.BlockSpec((1, tt, HG_HEAD), lambda b, h, t: (b, t, h)),
        out_shape=jax.ShapeDtypeStruct((B, T, mix_b), BF16),
        scratch_shapes=[pltpu.VMEM((HG_HEAD, HG_HEAD), F32)],
        compiler_params=_params(("parallel", "parallel", "arbitrary"), est),
        name="hgrn2_mixer",
    )(lb_table.astype(F32), norm_g.reshape(1, HG_HEAD).astype(F32), z, z, z, z)


def _fgate_body(h_ref, w_ref, b_ref, *rest, tt, norm_width):
    if norm_width:
        ss_ref, o_ref, carry_ref = rest
    else:
        o_ref, carry_ref = rest
    t = pl.program_id(1)

    @pl.when(t == 0)
    def _():
        carry_ref[...] = jnp.zeros_like(carry_ref)

    fl = jnp.dot(h_ref[0], w_ref[...], preferred_element_type=F32)
    if norm_width:
        fl = fl * _row_rsqrt(ss_ref, norm_width)
    fl = fl + b_ref[...]
    c = (jnp.minimum(fl, 0.0) - jnp.log1p(jnp.exp(-jnp.abs(fl)))) * LOG2E
    row = lax.broadcasted_iota(jnp.int32, c.shape, 0)
    sh = 1
    while sh < tt:
        c = c + jnp.where(row >= sh, pltpu.roll(c, sh, axis=0), 0.0)
        sh *= 2
    c = c + carry_ref[...]
    o_ref[0] = c
    carry_ref[...] = c[tt - 1:tt, :]


def _fox_gates(h, wf, bf, row_ss=None, tt=512):
    B, T, D = h.shape
    tt = min(tt, T)
    ss_specs = [] if row_ss is None else [pl.BlockSpec((row_ss.shape[0], None, tt, 1), lambda b, t: (0, b, t, 0))]
    ss_args = [] if row_ss is None else [row_ss]
    est = 2 * (_nbytes((tt, D), BF16) + _nbytes((D, LANES), BF16) + _nbytes((tt, LANES), F32)) \
        + 8 * _nbytes((tt, LANES), F32)
    return pl.pallas_call(
        functools.partial(_fgate_body, tt=tt, norm_width=0 if row_ss is None else D),
        grid=(B, T // tt),
        in_specs=[pl.BlockSpec((1, tt, D), lambda b, t: (b, t, 0)),
                  pl.BlockSpec((D, LANES), lambda b, t: (0, 0)),
                  pl.BlockSpec((1, LANES), lambda b, t: (0, 0))] + ss_specs,
        out_specs=pl.BlockSpec((1, tt, LANES), lambda b, t: (b, t, 0)),
        out_shape=jax.ShapeDtypeStruct((B, T, LANES), F32),
        scratch_shapes=[pltpu.VMEM((1, LANES), F32)],
        compiler_params=_params(("parallel", "arbitrary"), est),
        name="fox_gates",
    )(h, wf, bf, *ss_args)


def _fox_body(q_ref, k_ref, v_ref, f_ref, fr_ref, o_ref, vt_ref, fk_ref, p_ref, acc_ref, *, tq, seq, group):
    hg = pl.program_id(1)
    i = pl.program_id(2)
    reps = tq // LANES
    heads = range(group)

    def head_cols(g):
        return slice(g * FOX_HEAD, (g + 1) * FOX_HEAD)

    @pl.when(i == 0)
    def _():
        for c in range(seq // tq):
            rows = slice(c * tq, (c + 1) * tq)
            fblk = f_ref[0, rows, :]
            lane = lax.broadcasted_iota(jnp.int32, fblk.shape, 1)
            for g in heads:
                vt_ref[g, :, rows] = v_ref[0, rows, head_cols(g)].T
                col = jnp.sum(jnp.where(lane == hg * group + g, fblk, 0.0), axis=-1, keepdims=True)
                fk_ref[g, rows, :] = jnp.broadcast_to(col, fblk.shape)

    qt = [q_ref[0, :, head_cols(g)].T for g in heads]
    fq = [fr_ref[0, g, i] for g in heads]

    def scores(g, j):
        start = pl.multiple_of(j * tq, tq)
        return jnp.dot(k_ref[0, pl.ds(start, tq), head_cols(g)], qt[g], preferred_element_type=F32)

    def softmax_block(g, j, st, m_prev, l_prev, diagonal):
        start = pl.multiple_of(j * tq, tq)
        t = st - jnp.concatenate([fk_ref[g, pl.ds(start, tq), :]] * reps, axis=1)
        if diagonal:
            key = lax.broadcasted_iota(jnp.int32, (tq, tq), 0)
            qry = lax.broadcasted_iota(jnp.int32, (tq, tq), 1)
            t = jnp.where(key <= qry, t, -jnp.inf)
        m_new = jnp.maximum(m_prev, jnp.max(t, axis=0, keepdims=True) + fq[g])
        alpha = jnp.exp2(m_prev - m_new)
        p = jnp.exp2(t - (m_new - fq[g]))
        l_new = alpha * l_prev + jnp.sum(p, axis=0, keepdims=True)
        return p.astype(BF16), m_new, l_new, alpha

    def weighted_values(g, j, p):
        start = pl.multiple_of(j * tq, tq)
        return jnp.dot(vt_ref[g, :, pl.ds(start, tq)], p, preferred_element_type=F32)

    def block(j, j_prev, stats, first):
        out = []
        for g in heads:
            m, l = stats[g]
            st = scores(g, j)
            if first:
                p, m, l, _ = softmax_block(g, j, st, m, l, True)
                acc_ref[g] = jnp.zeros((FOX_HEAD, tq), F32)
            else:
                pv_prev = weighted_values(g, j_prev, p_ref[g])
                p, m, l, alpha = softmax_block(g, j, st, m, l, False)
                acc_ref[g] = (acc_ref[g] + pv_prev) * alpha
            p_ref[g] = p
            out.append((m, l))
        return tuple(out)

    stats = tuple((jnp.full((1, tq), -jnp.inf, F32), jnp.zeros((1, tq), F32)) for _ in heads)
    stats = block(i, i, stats, True)
    def run(j0, n, c):
        for d in range(n):
            c = block(j0 + d, jnp.where(j0 + d == 0, i, j0 + d - 1), c, False)
        return c

    one, two = i & 1, i & 2
    stats = lax.cond(one == 1, lambda c: run(0, 1, c), lambda c: c, stats)
    stats = lax.cond(two == 2, lambda c: run(one, 2, c), lambda c: c, stats)
    stats = lax.fori_loop(0, i // 4, lambda jj, c: run(one + two + 4 * jj, 4, c), stats)
    j_last = jnp.where(i == 0, i, i - 1)
    for g in heads:
        acc = acc_ref[g] + weighted_values(g, j_last, p_ref[g])
        o_ref[0, :, head_cols(g)] = (acc / stats[g][1]).astype(o_ref.dtype).T


def _fox_attention(qkv, f, frow, heads, tq, group=FOX_GROUP):
    B, T, _ = qkv.shape
    nq = T // tq
    gw = group * FOX_HEAD
    ngrp = heads // group
    est = 2 * (2 * _nbytes((T, gw), BF16) + 2 * _nbytes((tq, gw), BF16)
               + _nbytes((T, LANES), F32) + group * _nbytes((nq, 8, tq), F32)) \
        + group * (_nbytes((T, FOX_HEAD), BF16) + _nbytes((T, LANES), F32) + 5 * _nbytes((tq, tq), F32))
    return pl.pallas_call(
        functools.partial(_fox_body, tq=tq, seq=T, group=group),
        grid=(B, ngrp, nq),
        in_specs=[pl.BlockSpec((1, tq, gw), lambda b, h, i: (b, i, h)),
                  pl.BlockSpec((1, T, gw), lambda b, h, i: (b, 0, ngrp + h)),
                  pl.BlockSpec((1, T, gw), lambda b, h, i: (b, 0, 2 * ngrp + h)),
                  pl.BlockSpec((1, T, LANES), lambda b, h, i: (b, 0, 0)),
                  pl.BlockSpec((1, group, nq, 1, tq), lambda b, h, i: (b, h, 0, 0, 0))],
        out_specs=pl.BlockSpec((1, tq, gw), lambda b, h, i: (b, i, h)),
        out_shape=jax.ShapeDtypeStruct((B, T, heads * FOX_HEAD), BF16),
        scratch_shapes=[pltpu.VMEM((group, FOX_HEAD, T), BF16),
                        pltpu.VMEM((group, T, LANES), F32),
                        pltpu.VMEM((group, tq, tq), BF16),
                        pltpu.VMEM((group, FOX_HEAD, tq), F32)],
        compiler_params=_params(("parallel", "parallel", "arbitrary"), est),
        name="fox_attention",
    )(qkv, qkv, qkv, f, frow)


def _xattn_out_body(q_ref, kv_ref, w_ref, r_ref, g_ref, o_ref, h_ref, a_ref, *, d_model):
    hd = d_model // XA_HEADS
    for hh in range(XA_HEADS):
        q = q_ref[:, hh * hd:(hh + 1) * hd]
        k = kv_ref[:, hh * hd:(hh + 1) * hd]
        v = kv_ref[:, d_model + hh * hd:d_model + (hh + 1) * hd]
        s = lax.dot_general(q, k, (((1,), (1,)), ((), ())), preferred_element_type=F32)
        p = jnp.exp(s - jnp.max(s, axis=-1, keepdims=True))
        p = p / jnp.sum(p, axis=-1, keepdims=True)
        a_ref[:, hh * hd:(hh + 1) * hd] = jnp.dot(p.astype(BF16), v, preferred_element_type=F32).astype(a_ref.dtype)
    acc = r_ref[...] + jnp.dot(a_ref[...], w_ref[...], preferred_element_type=F32)
    o_ref[...] = acc
    ms = jnp.mean(acc * acc, axis=-1, keepdims=True)
    h_ref[...] = (acc * lax.rsqrt(ms + EPS) * g_ref[...]).astype(h_ref.dtype)


def _xattn_out_norm(q, kv, w, layer, r, gain, seq, tm=512):
    M, D = q.shape
    n_mem = kv.shape[1]
    tm = _pick_tile(seq, tm)
    per_batch = seq // tm
    est = 2 * (_nbytes((tm, D), BF16) + _nbytes((n_mem, 2 * D), BF16) + _nbytes((D, D), BF16)
               + 2 * _nbytes((tm, D), F32) + _nbytes((tm, D), BF16)) \
        + _nbytes((tm, D), BF16) + 3 * _nbytes((tm, D), F32)
    row = pl.BlockSpec((tm, D), lambda i: (i, 0))
    return pl.pallas_call(
        functools.partial(_xattn_out_body, d_model=D),
        grid=(M // tm,),
        in_specs=[row,
                  pl.BlockSpec((None, n_mem, 2 * D), lambda i: (i // per_batch, 0, 0)),
                  pl.BlockSpec((None, D, D), lambda i: (layer, 0, 0)),
                  row,
                  pl.BlockSpec((1, D), lambda i: (0, 0))],
        out_specs=[row, row],
        out_shape=[jax.ShapeDtypeStruct((M, D), F32),
                   jax.ShapeDtypeStruct((M, D), BF16)],
        scratch_shapes=[pltpu.VMEM((tm, D), BF16)],
        compiler_params=_params(("parallel",), est),
        name="xattn_out_norm",
    )(q, kv, w, r, gain.reshape(1, D).astype(F32))


def kernel(x, mem, lb_table, ev_norm, ev_w_in, ev_w_pool, ev_pool_scale, ev_hg_norm, ev_w_out,
           od_norm, od_w_in, od_b_f, od_w_out, xa_norm, xa_mem_norm, xa_wq, xa_wkv, xa_wo,
           ffn_norm, ffn_w_gate, ffn_w_up, ffn_w_down, final_norm):
    B, T, D = x.shape
    M = B * T
    depth = xa_norm.shape[0]
    mix_a = ev_pool_scale.shape[1]
    mix_b = lb_table.shape[1]
    fox_heads = od_b_f.shape[1]
    n_mem = mem.shape[1]
    xa_scale = (D // XA_HEADS) ** -0.5
    fox_tq = _pick_tile(T, 512)

    xs = x.reshape(M, D)
    mem2 = mem.reshape(B * n_mem, D)
    w_down = ffn_w_down.astype(BF16)
    ev_wo, od_wo, xa_wo_b = ev_w_out.astype(BF16), od_w_out.astype(BF16), xa_wo.astype(BF16)
    xa_wq_b = xa_wq.astype(BF16)
    xa_qscale = jnp.full((D,), xa_scale, F32)
    pre = None
    for l in range(depth):
        if pre is None:
            h, ss = _rmsnorm(xs, ev_norm[l // 2] if l % 2 == 0 else od_norm[l // 2], BF16), None
        else:
            h, ss = pre
        if l % 2 == 0:
            e = l // 2
            z = _matmul(h, ev_w_in, e, ev_w_in.shape[2], BF16, row_ss=ss).reshape(B, T, -1)
            ya = _pool_mixer(z, ev_w_pool[e].astype(BF16), ev_pool_scale[e], mix_a)
            yb = _hgrn_mixer(z, lb_table, ev_hg_norm[e], l, mix_a, mix_b)
            xs, q = _matmul_residual_norm_proj([ya.reshape(M, mix_a), yb.reshape(M, mix_b)], ev_wo, e, xs, xa_norm[l],
                                               xa_wq_b, l, xa_qscale)
        else:
            o = l // 2
            qscale = jnp.concatenate([jnp.full((D,), LOG2E * FOX_HEAD ** -0.5, F32), jnp.ones((2 * D,), F32)])
            w_in = od_w_in.astype(BF16)
            qkv = _matmul(h, w_in, o, 3 * D, BF16, colscale=qscale, row_ss=ss).reshape(B, T, 3 * D)
            wf = jnp.zeros((D, LANES), BF16).at[:, :fox_heads].set(w_in[o, :, 3 * D:])
            bf = jnp.zeros((1, LANES), F32).at[0, :fox_heads].set(od_b_f[o].astype(F32))
            ss4 = None if ss is None else ss.reshape(ss.shape[0], B, T, 1)
            f = _fox_gates(h.reshape(B, T, D), wf, bf, ss4)
            frow = f[..., :fox_heads].transpose(0, 2, 1).reshape(B, fox_heads, T // fox_tq, 1, fox_tq)
            y = _fox_attention(qkv, f, frow, fox_heads, fox_tq).reshape(M, D)
            xs, q = _matmul_residual_norm_proj([y], od_wo, o, xs, xa_norm[l], xa_wq_b, l, xa_qscale)

        mn = _rmsnorm(mem2, xa_mem_norm[l], BF16)
        kv = _matmul(mn, xa_wkv, l, 2 * D, BF16)
        xs, h = _xattn_out_norm(q, kv.reshape(B, n_mem, 2 * D), xa_wo_b, l, xs, ffn_norm[l], T)

        act = _matmul_swiglu(h, ffn_w_gate, ffn_w_up, l)
        if l + 1 < depth:
            nxt = l + 1
            gain = ev_norm[nxt // 2] if nxt % 2 == 0 else od_norm[nxt // 2]
            xs, hp, ss = _matmul_residual(act, w_down, l, xs, gain=gain)
            pre = (hp, ss)
        else:
            xs = _matmul_residual(act, w_down, l, xs)
    return _rmsnorm(xs, final_norm, x.dtype).reshape(B, T, D)
```

```python
import functools

import jax
import jax.numpy as jnp
from jax import lax
from jax.experimental import pallas as pl
from jax.experimental.pallas import tpu as pltpu

F32 = jnp.float32
BF16 = jnp.bfloat16

EPS = 1e-6
POOL_WINDOWS = (2, 4, 8, 16)
POOL_HALO = 16
assert all(w & (w - 1) == 0 and w <= POOL_HALO for w in POOL_WINDOWS)
HG_HEAD = 128
HG_CHUNK = 64
HG_DIAG = 8
HG_SUB = 128
FOX_HEAD = 128
FOX_GROUP = 4
XA_HEADS = 4

LOG2E = 1.4426950408889634
LANES = 128
VMEM_CAP = 56 * 1024 * 1024
VMEM_FLOOR = 32 * 1024 * 1024


def _params(semantics, vmem_estimate):
    limit = int(min(max(vmem_estimate * 5 // 4, VMEM_FLOOR), VMEM_CAP))
    return pltpu.CompilerParams(dimension_semantics=semantics, vmem_limit_bytes=limit)


def _nbytes(shape, dtype):
    n = jnp.dtype(dtype).itemsize
    for s in shape:
        n *= s
    return n


def _rmsnorm_body(x_ref, g_ref, o_ref):
    x = x_ref[...]
    ms = jnp.mean(x * x, axis=-1, keepdims=True)
    o_ref[...] = (x * lax.rsqrt(ms + EPS) * g_ref[...]).astype(o_ref.dtype)


def _rmsnorm(x, g, out_dtype, tm=512):
    M, D = x.shape
    tm = min(tm, M)
    est = 2 * (_nbytes((tm, D), x.dtype) + _nbytes((tm, D), out_dtype)) + 3 * _nbytes((tm, D), F32)
    return pl.pallas_call(
        _rmsnorm_body,
        grid=(M // tm,),
        in_specs=[pl.BlockSpec((tm, D), lambda i: (i, 0)),
                  pl.BlockSpec((1, D), lambda i: (0, 0))],
        out_specs=pl.BlockSpec((tm, D), lambda i: (i, 0)),
        out_shape=jax.ShapeDtypeStruct((M, D), out_dtype),
        compiler_params=_params(("parallel",), est),
        name="rmsnorm",
    )(x, g.reshape(1, D).astype(F32))


def _with_bf16_weights(w_refs, wb_refs, compute):
    if not wb_refs:
        compute(*(w[...] for w in w_refs))
        return
    i = pl.program_id(1)

    @pl.when(i == 0)
    def _():
        for w, wb in zip(w_refs, wb_refs):
            wb[...] = w[...].astype(BF16)

    @pl.when(i > 0)
    def _():
        compute(*(wb[...] for wb in wb_refs))


def _row_rsqrt(ss_ref, width):
    return lax.rsqrt(jnp.sum(ss_ref[...], axis=0) * (1.0 / width) + EPS)


def _mm_body(x_ref, w_ref, cs_ref, *rest, norm_width):
    if norm_width:
        ss_ref, o_ref, *wb = rest
    else:
        o_ref, *wb = rest

    def compute(w):
        acc = jnp.dot(x_ref[...], w, preferred_element_type=F32)
        if norm_width:
            acc = acc * _row_rsqrt(ss_ref, norm_width)
        o_ref[...] = (acc * cs_ref[...]).astype(o_ref.dtype)

    _with_bf16_weights((w_ref,), wb, compute)


def _mm_res_body(x_ref, w_ref, r_ref, *rest, with_gain):
    if with_gain:
        g_ref, o_ref, hp_ref, ss_ref, *wb = rest
    else:
        o_ref, *wb = rest

    def compute(w):
        acc = r_ref[...] + jnp.dot(x_ref[...], w, preferred_element_type=F32)
        o_ref[...] = acc
        if with_gain:
            hp_ref[...] = (acc * g_ref[...]).astype(hp_ref.dtype)
            ss_ref[...] = jnp.sum(acc * acc, axis=-1, keepdims=True)

    _with_bf16_weights((w_ref,), wb, compute)


def _mm_swiglu_body(x_ref, wg_ref, wu_ref, o_ref, *wb):
    def compute(wg, wu):
        x = x_ref[...]
        g = jnp.dot(x, wg, preferred_element_type=F32)
        u = jnp.dot(x, wu, preferred_element_type=F32)
        o_ref[...] = (g * jax.nn.sigmoid(g) * u).astype(o_ref.dtype)

    _with_bf16_weights((wg_ref, wu_ref), wb, compute)


def _pick_tile(n, pref):
    t = min(pref, n)
    while n % t:
        t //= 2
    return t


def _weight_spec(layer, K, tn):
    return pl.BlockSpec((None, K, tn), lambda j, i: (layer, 0, j))


def _weight_scratch(w, K, tn):
    return [] if w.dtype == BF16 else [pltpu.VMEM((K, tn), BF16)]


def _weight_bytes(w, K, tn):
    return 2 * _nbytes((K, tn), w.dtype) + (0 if w.dtype == BF16 else _nbytes((K, tn), BF16))


def _row_steps(w, M, tm):
    extra = 0 if w.dtype == BF16 else 1
    return M // tm + extra, lambda i: jnp.maximum(i - extra, 0)


def _matmul(x, w, layer, n_cols, out_dtype, colscale=None, row_ss=None, tm=1024, tn=1024):
    M, K = x.shape
    tm, tn = _pick_tile(M, tm), _pick_tile(n_cols, tn)
    if colscale is None:
        colscale = jnp.ones((n_cols,), F32)
    steps, row = _row_steps(w, M, tm)
    ss_specs = [] if row_ss is None else [pl.BlockSpec((row_ss.shape[0], tm, 1), lambda j, i: (0, row(i), 0))]
    ss_args = [] if row_ss is None else [row_ss]
    est = 2 * (_nbytes((tm, K), x.dtype) + _nbytes((tm, tn), out_dtype)) + _weight_bytes(w, K, tn) \
        + 2 * _nbytes((tm, tn), F32)
    return pl.pallas_call(
        functools.partial(_mm_body, norm_width=0 if row_ss is None else K),
        grid=(n_cols // tn, steps),
        in_specs=[pl.BlockSpec((tm, K), lambda j, i: (row(i), 0)),
                  _weight_spec(layer, K, tn),
                  pl.BlockSpec((1, tn), lambda j, i: (0, j))] + ss_specs,
        out_specs=pl.BlockSpec((tm, tn), lambda j, i: (row(i), j)),
        out_shape=jax.ShapeDtypeStruct((M, n_cols), out_dtype),
        scratch_shapes=_weight_scratch(w, K, tn),
        compiler_params=_params(("parallel", "arbitrary"), est),
        name="matmul",
    )(x, w, colscale.reshape(1, n_cols), *ss_args)


def _matmul_residual(x, w, layer, r, gain=None, tn=1024):
    M, K = x.shape
    N = w.shape[2]
    tm = 1024 if K <= 2048 else 512
    tm, tn = _pick_tile(M, tm), _pick_tile(N, tn)
    steps, row = _row_steps(w, M, tm)
    est = 2 * (_nbytes((tm, K), x.dtype) + 2 * _nbytes((tm, tn), F32)) + _weight_bytes(w, K, tn) \
        + 2 * _nbytes((tm, tn), F32)
    tile = pl.BlockSpec((tm, tn), lambda j, i: (row(i), j))
    in_specs = [pl.BlockSpec((tm, K), lambda j, i: (row(i), 0)), _weight_spec(layer, K, tn), tile]
    out_specs, out_shape, args = tile, jax.ShapeDtypeStruct((M, N), F32), [x, w, r]
    if gain is not None:
        in_specs.append(pl.BlockSpec((1, tn), lambda j, i: (0, j)))
        args.append(gain.reshape(1, N).astype(F32))
        out_specs = [tile, tile, pl.BlockSpec((None, tm, 1), lambda j, i: (j, row(i), 0))]
        out_shape = [out_shape, jax.ShapeDtypeStruct((M, N), BF16), jax.ShapeDtypeStruct((N // tn, M, 1), F32)]
    return pl.pallas_call(
        functools.partial(_mm_res_body, with_gain=gain is not None),
        grid=(N // tn, steps),
        in_specs=in_specs,
        out_specs=out_specs,
        out_shape=out_shape,
        scratch_shapes=_weight_scratch(w, K, tn),
        compiler_params=_params(("parallel", "arbitrary"), est),
        name="matmul_residual",
    )(*args)


def _matmul_swiglu(x, wg, wu, layer, tm=1024, tn=512):
    M, K = x.shape
    N = wg.shape[2]
    tm, tn = _pick_tile(M, tm), _pick_tile(N, tn)
    steps, row = _row_steps(wg, M, tm)
    est = 2 * (_nbytes((tm, K), x.dtype) + _nbytes((tm, tn), BF16)) + 2 * _weight_bytes(wg, K, tn) \
        + 4 * _nbytes((tm, tn), F32)
    return pl.pallas_call(
        _mm_swiglu_body,
        grid=(N // tn, steps),
        in_specs=[pl.BlockSpec((tm, K), lambda j, i: (row(i), 0)),
                  _weight_spec(layer, K, tn),
                  _weight_spec(layer, K, tn)],
        out_specs=pl.BlockSpec((tm, tn), lambda j, i: (row(i), j)),
        out_shape=jax.ShapeDtypeStruct((M, N), BF16),
        scratch_shapes=_weight_scratch(wg, K, tn) + _weight_scratch(wu, K, tn),
        compiler_params=_params(("parallel", "arbitrary"), est),
        name="matmul_swiglu",
    )(x, wg, wu)


def _mm_res_norm_proj_body(*refs, n_parts):
    x_refs = refs[:n_parts]
    w_ref, r_ref, g_ref, w2_ref, cs_ref, o_ref, q_ref = refs[n_parts:]
    acc = r_ref[...]
    k0 = 0
    for x_ref in x_refs:
        kw = x_ref.shape[1]
        acc = acc + jnp.dot(x_ref[...], w_ref[k0:k0 + kw, :], preferred_element_type=F32)
        k0 += kw
    o_ref[...] = acc
    ms = jnp.mean(acc * acc, axis=-1, keepdims=True)
    h = (acc * lax.rsqrt(ms + EPS) * g_ref[...]).astype(BF16)
    q_ref[...] = (jnp.dot(h, w2_ref[...], preferred_element_type=F32) * cs_ref[...]).astype(q_ref.dtype)


def _matmul_residual_norm_proj(x_parts, w, layer, r, gain, w2, layer2, colscale, tm=512):
    M, N = r.shape
    K = w.shape[1]
    N2 = w2.shape[2]
    tm = _pick_tile(M, tm)
    est = 2 * (_nbytes((tm, K), BF16) + 2 * _nbytes((tm, N), F32) + _nbytes((tm, N2), BF16)) \
        + _nbytes((K, N), BF16) + _nbytes((N, N2), BF16) + 4 * _nbytes((tm, N), F32)
    resident = pl.Buffered(1)
    return pl.pallas_call(
        functools.partial(_mm_res_norm_proj_body, n_parts=len(x_parts)),
        grid=(M // tm,),
        in_specs=[pl.BlockSpec((tm, xp.shape[1]), lambda i: (i, 0)) for xp in x_parts]
        + [pl.BlockSpec((None, K, N), lambda i: (layer, 0, 0), pipeline_mode=resident),
           pl.BlockSpec((tm, N), lambda i: (i, 0)),
           pl.BlockSpec((1, N), lambda i: (0, 0)),
           pl.BlockSpec((None, N, N2), lambda i: (layer2, 0, 0), pipeline_mode=resident),
           pl.BlockSpec((1, N2), lambda i: (0, 0))],
        out_specs=[pl.BlockSpec((tm, N), lambda i: (i, 0)),
                   pl.BlockSpec((tm, N2), lambda i: (i, 0))],
        out_shape=[jax.ShapeDtypeStruct((M, N), F32),
                   jax.ShapeDtypeStruct((M, N2), BF16)],
        compiler_params=_params(("parallel",), est),
        name="matmul_residual_norm_proj",
    )(*x_parts, w, r, gain.reshape(1, N).astype(F32), w2, colscale.reshape(1, N2).astype(F32))


def _pool_body(u_ref, halo_ref, w_ref, sc_ref, o_ref, ext_ref, *, tt, group):
    t = pl.program_id(1)
    u = u_ref[0].astype(F32)
    ext_ref[0:POOL_HALO, :] = jnp.where(t > 0, halo_ref[0].astype(F32), 0.0)
    ext_ref[POOL_HALO:POOL_HALO + tt, :] = u
    pos = t * tt + lax.broadcasted_iota(jnp.int32, (tt, 1), 0)
    for gi, win in enumerate(POOL_WINDOWS):
        c0, c1 = gi * group, (gi + 1) * group
        tok = u[:, c0:c1]
        lvl = ext_ref[:, c0:c1]
        k = 1
        while k < win:
            lvl = lvl + pltpu.roll(lvl, k, axis=0)
            k *= 2
        acc = lvl[POOL_HALO:POOL_HALO + tt]
        cnt = jnp.minimum(pos + 1, win).astype(F32)
        p = acc / cnt - tok
        y = jnp.dot(p.astype(BF16), w_ref[gi], preferred_element_type=F32)
        o_ref[0, :, c0:c1] = (y * sc_ref[:, c0:c1]).astype(o_ref.dtype)


def _pool_mixer(z, w_pool, scale, mix_a, tt=1024):
    B, T, _ = z.shape
    tt = min(tt, T)
    group = mix_a // len(POOL_WINDOWS)
    halo_blocks = tt // POOL_HALO
    est = 2 * (_nbytes((tt, mix_a), F32) + _nbytes((tt, mix_a), BF16)) + 4 * _nbytes((tt, mix_a), F32)
    return pl.pallas_call(
        functools.partial(_pool_body, tt=tt, group=group),
        grid=(B, T // tt),
        in_specs=[pl.BlockSpec((1, tt, mix_a), lambda b, t: (b, t, 0)),
                  pl.BlockSpec((1, POOL_HALO, mix_a),
                               lambda b, t: (b, jnp.maximum(t * halo_blocks - 1, 0), 0)),
                  pl.BlockSpec((len(POOL_WINDOWS), group, group), lambda b, t: (0, 0, 0)),
                  pl.BlockSpec((1, mix_a), lambda b, t: (0, 0))],
        out_specs=pl.BlockSpec((1, tt, mix_a), lambda b, t: (b, t, 0)),
        out_shape=jax.ShapeDtypeStruct((B, T, mix_a), BF16),
        scratch_shapes=[pltpu.VMEM((tt + POOL_HALO, mix_a), F32)],
        compiler_params=_params(("parallel", "parallel"), est),
        name="pool_mixer",
    )(z, z, w_pool, scale.reshape(1, mix_a).astype(F32))


def _hgrn_subtile(qf, kk, v, logf, st):
    n = qf.shape[0]
    C, R = HG_CHUNK, HG_DIAG
    row = lax.broadcasted_iota(jnp.int32, (n, HG_HEAD), 0)

    pos = row & (C - 1)
    b = logf
    sh = 1
    while sh < C:
        b = b + jnp.where(pos >= sh, pltpu.roll(b, sh, axis=0), 0.0)
        sh *= 2

    nb = n // R
    b3, q3, c3, v3 = (a.reshape(nb, R, HG_HEAD) for a in (b, qf, b - jnp.log2(kk), v))
    tpos = lax.broadcasted_iota(jnp.int32, (1, R, 1), 1)
    od = jnp.zeros((nb, R, HG_HEAD), F32)
    for s in range(R):
        w = q3 * jnp.exp2(b3 - c3[:, s:s + 1, :])
        a = jnp.sum(w, axis=-1, keepdims=True)
        a = jnp.where(tpos >= s, a, 0.0)
        od = od + a * v3[:, s:s + 1, :]
    o = od.reshape(n, HG_HEAD)

    same = lax.broadcasted_iota(jnp.int32, (n, n), 0) ^ lax.broadcasted_iota(jnp.int32, (n, n), 1)
    a_off = jnp.zeros((n, n), F32)
    h = R
    while h < C:
        g = 2 * h
        ref = b.reshape(n // g, g, HG_HEAD)[:, h - 1:h, :]
        ref = jnp.broadcast_to(ref, (n // g, g, HG_HEAD)).reshape(n, HG_HEAD)
        e = jnp.exp2(-jnp.abs(b - ref))
        right = (row & (g - 1)) >= h
        qt = jnp.where(right, qf * e, 0.0).astype(BF16)
        kt = jnp.where(right, 0.0, kk * e).astype(BF16)
        a = lax.dot_general(qt, kt, (((1,), (1,)), ((), ())), preferred_element_type=F32)
        a_off = a_off + jnp.where(same < g, a, 0.0)
        h = g
    v_bf = v.astype(BF16)
    o = o + jnp.dot(a_off.astype(BF16), v_bf, preferred_element_type=F32)

    nc = n // C
    bc = b.reshape(nc, C, HG_HEAD)
    b_last = bc[:, C - 1:C, :]
    q_in = (qf * jnp.exp2(b)).astype(BF16)
    k_out = (kk.reshape(nc, C, HG_HEAD) * jnp.exp2(b_last - bc)).astype(BF16)
    dec = jnp.exp2(b_last)
    pieces = []
    for c in range(nc):
        rows = slice(c * C, (c + 1) * C)
        o_int = lax.dot_general(q_in[rows], st.astype(BF16), (((1,), (1,)), ((), ())),
                                preferred_element_type=F32)
        pieces.append(o[rows] + o_int)
        upd = lax.dot_general(v_bf[rows], k_out[c], (((0,), (0,)), ((), ())),
                              preferred_element_type=F32)
        st = st * dec[c] + upd
    return jnp.concatenate(pieces, axis=0), st


def _hgrn_body(lbt_ref, ng_ref, q_ref, f_ref, i_ref, g_ref, o_ref, st_ref, *, layer, tt, sub):
    t = pl.program_id(2)

    @pl.when(t == 0)
    def _():
        st_ref[...] = jnp.zeros_like(st_ref)

    lbt = lbt_ref[...]
    e = jnp.exp(lbt - jnp.max(lbt, axis=0, keepdims=True))
    sm = e / jnp.sum(e, axis=0, keepdims=True)
    lb = jnp.sum(sm[1:layer + 2], axis=0, keepdims=True)

    st = st_ref[...]
    for c in range(tt // sub):
        rows = slice(c * sub, (c + 1) * sub)
        f = lb + (1.0 - lb) * jax.nn.sigmoid(f_ref[0, rows, :].astype(F32))
        q = q_ref[0, rows, :].astype(F32)
        qf = q * jax.nn.sigmoid(q) * (HG_HEAD ** -0.5)
        o, st = _hgrn_subtile(qf, 1.0 - f, i_ref[0, rows, :].astype(F32), jnp.log2(f), st)
        ms = jnp.mean(o * o, axis=-1, keepdims=True)
        o = o * lax.rsqrt(ms + EPS) * ng_ref[...]
        g = g_ref[0, rows, :].astype(F32)
        o_ref[0, rows, :] = (o * (g * jax.nn.sigmoid(g))).astype(o_ref.dtype)
    st_ref[...] = st


def _hgrn_mixer(z, lb_table, norm_g, layer, mix_a, mix_b, tt=2048):
    B, T, _ = z.shape
    tt = min(tt, T)
    sub = min(HG_SUB, tt)
    heads = mix_b // HG_HEAD
    c0 = mix_a // HG_HEAD
    est = 2 * 5 * _nbytes((tt, HG_HEAD), F32) + 32 * _nbytes((sub, HG_HEAD), F32) + 6 * _nbytes((sub, sub), F32)

    def col(off):
        return pl.BlockSpec((1, tt, HG_HEAD), lambda b, h, t, off=off: (b, t, c0 + off * heads + h))

    return pl.pallas_call(
        functools.partial(_hgrn_body, layer=layer, tt=tt, sub=sub),
        grid=(B, heads, T // tt),
        in_specs=[pl.BlockSpec((lb_table.shape[0], HG_HEAD), lambda b, h, t: (0, h)),
                  pl.BlockSpec((1, HG_HEAD), lambda b, h, t: (0, 0)),
                  col(0), col(1), col(2), col(3)],
        out_specs=pl.BlockSpec((1, tt, HG_HEAD), lambda b, h, t: (b, t, h)),
        out_shape=jax.ShapeDtypeStruct((B, T, mix_b), BF16),
        scratch_shapes=[pltpu.VMEM((HG_HEAD, HG_HEAD), F32)],
        compiler_params=_params(("parallel", "parallel", "arbitrary"), est),
        name="hgrn2_mixer",
    )(lb_table.astype(F32), norm_g.reshape(1, HG_HEAD).astype(F32), z, z, z, z)


def _fgate_body(h_ref, w_ref, b_ref, *rest, tt, norm_width):
    if norm_width:
        ss_ref, o_ref, carry_ref = rest
    else:
        o_ref, carry_ref = rest
    t = pl.program_id(1)

    @pl.when(t == 0)
    def _():
        carry_ref[...] = jnp.zeros_like(carry_ref)

    fl = jnp.dot(h_ref[0], w_ref[...], preferred_element_type=F32)
    if norm_width:
        fl = fl * _row_rsqrt(ss_ref, norm_width)
    fl = fl + b_ref[...]
    c = (jnp.minimum(fl, 0.0) - jnp.log1p(jnp.exp(-jnp.abs(fl)))) * LOG2E
    row = lax.broadcasted_iota(jnp.int32, c.shape, 0)
    sh = 1
    while sh < tt:
        c = c + jnp.where(row >= sh, pltpu.roll(c, sh, axis=0), 0.0)
        sh *= 2
    c = c + carry_ref[...]
    o_ref[0] = c
    carry_ref[...] = c[tt - 1:tt, :]


def _fox_gates(h, wf, bf, row_ss=None, tt=512):
    B, T, D = h.shape
    tt = min(tt, T)
    ss_specs = [] if row_ss is None else [pl.BlockSpec((row_ss.shape[0], None, tt, 1), lambda b, t: (0, b, t, 0))]
    ss_args = [] if row_ss is None else [row_ss]
    est = 2 * (_nbytes((tt, D), BF16) + _nbytes((D, LANES), BF16) + _nbytes((tt, LANES), F32)) \
        + 8 * _nbytes((tt, LANES), F32)
    return pl.pallas_call(
        functools.partial(_fgate_body, tt=tt, norm_width=0 if row_ss is None else D),
        grid=(B, T // tt),
        in_specs=[pl.BlockSpec((1, tt, D), lambda b, t: (b, t, 0)),
                  pl.BlockSpec((D, LANES), lambda b, t: (0, 0)),
                  pl.BlockSpec((1, LANES), lambda b, t: (0, 0))] + ss_specs,
        out_specs=pl.BlockSpec((1, tt, LANES), lambda b, t: (b, t, 0)),
        out_shape=jax.ShapeDtypeStruct((B, T, LANES), F32),
        scratch_shapes=[pltpu.VMEM((1, LANES), F32)],
        compiler_params=_params(("parallel", "arbitrary"), est),
        name="fox_gates",
    )(h, wf, bf, *ss_args)


def _fox_body(q_ref, k_ref, v_ref, f_ref, fr_ref, o_ref, vt_ref, fk_ref, p_ref, acc_ref, *, tq, seq, group):
    hg = pl.program_id(1)
    i = pl.program_id(2)
    reps = tq // LANES
    heads = range(group)

    def head_cols(g):
        return slice(g * FOX_HEAD, (g + 1) * FOX_HEAD)

    @pl.when(i == 0)
    def _():
        for c in range(seq // tq):
            rows = slice(c * tq, (c + 1) * tq)
            fblk = f_ref[0, rows, :]
            lane = lax.broadcasted_iota(jnp.int32, fblk.shape, 1)
            for g in heads:
                vt_ref[g, :, rows] = v_ref[0, rows, head_cols(g)].T
                col = jnp.sum(jnp.where(lane == hg * group + g, fblk, 0.0), axis=-1, keepdims=True)
                fk_ref[g, rows, :] = jnp.broadcast_to(col, fblk.shape)

    qt = [q_ref[0, :, head_cols(g)].T for g in heads]
    fq = [fr_ref[0, g, i] for g in heads]

    def scores(g, j):
        start = pl.multiple_of(j * tq, tq)
        return jnp.dot(k_ref[0, pl.ds(start, tq), head_cols(g)], qt[g], preferred_element_type=F32)

    def softmax_block(g, j, st, m_prev, l_prev, diagonal):
        start = pl.multiple_of(j * tq, tq)
        t = st - jnp.concatenate([fk_ref[g, pl.ds(start, tq), :]] * reps, axis=1)
        if diagonal:
            key = lax.broadcasted_iota(jnp.int32, (tq, tq), 0)
            qry = lax.broadcasted_iota(jnp.int32, (tq, tq), 1)
            t = jnp.where(key <= qry, t, -jnp.inf)
        m_new = jnp.maximum(m_prev, jnp.max(t, axis=0, keepdims=True) + fq[g])
        alpha = jnp.exp2(m_prev - m_new)
        p = jnp.exp2(t - (m_new - fq[g]))
        l_new = alpha * l_prev + jnp.sum(p, axis=0, keepdims=True)
        return p.astype(BF16), m_new, l_new, alpha

    def weighted_values(g, j, p):
        start = pl.multiple_of(j * tq, tq)
        return jnp.dot(vt_ref[g, :, pl.ds(start, tq)], p, preferred_element_type=F32)

    def block(j, j_prev, stats, first):
        out = []
        for g in heads:
            m, l = stats[g]
            st = scores(g, j)
            if first:
                p, m, l, _ = softmax_block(g, j, st, m, l, True)
                acc_ref[g] = jnp.zeros((FOX_HEAD, tq), F32)
            else:
                pv_prev = weighted_values(g, j_prev, p_ref[g])
                p, m, l, alpha = softmax_block(g, j, st, m, l, False)
                acc_ref[g] = (acc_ref[g] + pv_prev) * alpha
            p_ref[g] = p
            out.append((m, l))
        return tuple(out)

    stats = tuple((jnp.full((1, tq), -jnp.inf, F32), jnp.zeros((1, tq), F32)) for _ in heads)
    stats = block(i, i, stats, True)
    def run(j0, n, c):
        for d in range(n):
            c = block(j0 + d, jnp.where(j0 + d == 0, i, j0 + d - 1), c, False)
        return c

    one, two = i & 1, i & 2
    stats = lax.cond(one == 1, lambda c: run(0, 1, c), lambda c: c, stats)
    stats = lax.cond(two == 2, lambda c: run(one, 2, c), lambda c: c, stats)
    stats = lax.fori_loop(0, i // 4, lambda jj, c: run(one + two + 4 * jj, 4, c), stats)
    j_last = jnp.where(i == 0, i, i - 1)
    for g in heads:
        acc = acc_ref[g] + weighted_values(g, j_last, p_ref[g])
        o_ref[0, :, head_cols(g)] = (acc / stats[g][1]).astype(o_ref.dtype).T


def _fox_attention(qkv, f, frow, heads, tq, group=FOX_GROUP):
    B, T, _ = qkv.shape
    nq = T // tq
    gw = group * FOX_HEAD
    ngrp = heads // group
    est = 2 * (2 * _nbytes((T, gw), BF16) + 2 * _nbytes((tq, gw), BF16)
               + _nbytes((T, LANES), F32) + group * _nbytes((nq, 8, tq), F32)) \
        + group * (_nbytes((T, FOX_HEAD), BF16) + _nbytes((T, LANES), F32) + 5 * _nbytes((tq, tq), F32))
    return pl.pallas_call(
        functools.partial(_fox_body, tq=tq, seq=T, group=group),
        grid=(B, ngrp, nq),
        in_specs=[pl.BlockSpec((1, tq, gw), lambda b, h, i: (b, i, h)),
                  pl.BlockSpec((1, T, gw), lambda b, h, i: (b, 0, ngrp + h)),
                  pl.BlockSpec((1, T, gw), lambda b, h, i: (b, 0, 2 * ngrp + h)),
                  pl.BlockSpec((1, T, LANES), lambda b, h, i: (b, 0, 0)),
                  pl.BlockSpec((1, group, nq, 1, tq), lambda b, h, i: (b, h, 0, 0, 0))],
        out_specs=pl.BlockSpec((1, tq, gw), lambda b, h, i: (b, i, h)),
        out_shape=jax.ShapeDtypeStruct((B, T, heads * FOX_HEAD), BF16),
        scratch_shapes=[pltpu.VMEM((group, FOX_HEAD, T), BF16),
                        pltpu.VMEM((group, T, LANES), F32),
                        pltpu.VMEM((group, tq, tq), BF16),
                        pltpu.VMEM((group, FOX_HEAD, tq), F32)],
        compiler_params=_params(("parallel", "parallel", "arbitrary"), est),
        name="fox_attention",
    )(qkv, qkv, qkv, f, frow)


def _xattn_out_body(q_ref, kv_ref, w_ref, r_ref, g_ref, o_ref, h_ref, a_ref, *, d_model):
    hd = d_model // XA_HEADS
    for hh in range(XA_HEADS):
        q = q_ref[:, hh * hd:(hh + 1) * hd]
        k = kv_ref[:, hh * hd:(hh + 1) * hd]
        v = kv_ref[:, d_model + hh * hd:d_model + (hh + 1) * hd]
        s = lax.dot_general(q, k, (((1,), (1,)), ((), ())), preferred_element_type=F32)
        p = jnp.exp(s - jnp.max(s, axis=-1, keepdims=True))
        p = p / jnp.sum(p, axis=-1, keepdims=True)
        a_ref[:, hh * hd:(hh + 1) * hd] = jnp.dot(p.astype(BF16), v, preferred_element_type=F32).astype(a_ref.dtype)
    acc = r_ref[...] + jnp.dot(a_ref[...], w_ref[...], preferred_element_type=F32)
    o_ref[...] = acc
    ms = jnp.mean(acc * acc, axis=-1, keepdims=True)
    h_ref[...] = (acc * lax.rsqrt(ms + EPS) * g_ref[...]).astype(h_ref.dtype)


def _xattn_out_norm(q, kv, w, layer, r, gain, seq, tm=512):
    M, D = q.shape
    n_mem = kv.shape[1]
    tm = _pick_tile(seq, tm)
    per_batch = seq // tm
    est = 2 * (_nbytes((tm, D), BF16) + _nbytes((n_mem, 2 * D), BF16) + _nbytes((D, D), BF16)
               + 2 * _nbytes((tm, D), F32) + _nbytes((tm, D), BF16)) \
        + _nbytes((tm, D), BF16) + 3 * _nbytes((tm, D), F32)
    row = pl.BlockSpec((tm, D), lambda i: (i, 0))
    return pl.pallas_call(
        functools.partial(_xattn_out_body, d_model=D),
        grid=(M // tm,),
        in_specs=[row,
                  pl.BlockSpec((None, n_mem, 2 * D), lambda i: (i // per_batch, 0, 0)),
                  pl.BlockSpec((None, D, D), lambda i: (layer, 0, 0)),
                  row,
                  pl.BlockSpec((1, D), lambda i: (0, 0))],
        out_specs=[row, row],
        out_shape=[jax.ShapeDtypeStruct((M, D), F32),
                   jax.ShapeDtypeStruct((M, D), BF16)],
        scratch_shapes=[pltpu.VMEM((tm, D), BF16)],
        compiler_params=_params(("parallel",), est),
        name="xattn_out_norm",
    )(q, kv, w, r, gain.reshape(1, D).astype(F32))


def kernel(x, mem, lb_table, ev_norm, ev_w_in, ev_w_pool, ev_pool_scale, ev_hg_norm, ev_w_out,
           od_norm, od_w_in, od_b_f, od_w_out, xa_norm, xa_mem_norm, xa_wq, xa_wkv, xa_wo,
           ffn_norm, ffn_w_gate, ffn_w_up, ffn_w_down, final_norm):
    B, T, D = x.shape
    M = B * T
    depth = xa_norm.shape[0]
    mix_a = ev_pool_scale.shape[1]
    mix_b = lb_table.shape[1]
    fox_heads = od_b_f.shape[1]
    n_mem = mem.shape[1]
    xa_scale = (D // XA_HEADS) ** -0.5
    fox_tq = _pick_tile(T, 512)

    xs = x.reshape(M, D)
    mem2 = mem.reshape(B * n_mem, D)
    w_down = ffn_w_down.astype(BF16)
    ev_wo, od_wo, xa_wo_b = ev_w_out.astype(BF16), od_w_out.astype(BF16), xa_wo.astype(BF16)
    xa_wq_b = xa_wq.astype(BF16)
    xa_qscale = jnp.full((D,), xa_scale, F32)
    pre = None
    for l in range(depth):
        if pre is None:
            h, ss = _rmsnorm(xs, ev_norm[l // 2] if l % 2 == 0 else od_norm[l // 2], BF16), None
        else:
            h, ss = pre
        if l % 2 == 0:
            e = l // 2
            z = _matmul(h, ev_w_in, e, ev_w_in.shape[2], F32, row_ss=ss).reshape(B, T, -1)
            ya = _pool_mixer(z, ev_w_pool[e].astype(BF16), ev_pool_scale[e], mix_a)
            yb = _hgrn_mixer(z, lb_table, ev_hg_norm[e], l, mix_a, mix_b)
            xs, q = _matmul_residual_norm_proj([ya.reshape(M, mix_a), yb.reshape(M, mix_b)], ev_wo, e, xs, xa_norm[l],
                                               xa_wq_b, l, xa_qscale)
        else:
            o = l // 2
            qscale = jnp.concatenate([jnp.full((D,), LOG2E * FOX_HEAD ** -0.5, F32), jnp.ones((2 * D,), F32)])
            w_in = od_w_in.astype(BF16)
            qkv = _matmul(h, w_in, o, 3 * D, BF16, colscale=qscale, row_ss=ss).reshape(B, T, 3 * D)
            wf = jnp.zeros((D, LANES), BF16).at[:, :fox_heads].set(w_in[o, :, 3 * D:])
            bf = jnp.zeros((1, LANES), F32).at[0, :fox_heads].set(od_b_f[o].astype(F32))
            ss4 = None if ss is None else ss.reshape(ss.shape[0], B, T, 1)
            f = _fox_gates(h.reshape(B, T, D), wf, bf, ss4)
            frow = f[..., :fox_heads].transpose(0, 2, 1).reshape(B, fox_heads, T // fox_tq, 1, fox_tq)
            y = _fox_attention(qkv, f, frow, fox_heads, fox_tq).reshape(M, D)
            xs, q = _matmul_residual_norm_proj([y], od_wo, o, xs, xa_norm[l], xa_wq_b, l, xa_qscale)

        mn = _rmsnorm(mem2, xa_mem_norm[l], BF16)
        kv = _matmul(mn, xa_wkv, l, 2 * D, BF16)
        xs, h = _xattn_out_norm(q, kv.reshape(B, n_mem, 2 * D), xa_wo_b, l, xs, ffn_norm[l], T)

        act = _matmul_swiglu(h, ffn_w_gate, ffn_w_up, l)
        if l + 1 < depth:
            nxt = l + 1
            gain = ev_norm[nxt // 2] if nxt % 2 == 0 else od_norm[nxt // 2]
            xs, hp, ss = _matmul_residual(act, w_down, l, xs, gain=gain)
            pre = (hp, ss)
        else:
            xs = _matmul_residual(act, w_down, l, xs)
    return _rmsnorm(xs, final_norm, x.dtype).reshape(B, T, D)
```

```python
import functools

import jax
import jax.numpy as jnp
from jax import lax
from jax.experimental import pallas as pl
from jax.experimental.pallas import tpu as pltpu

F32 = jnp.float32
BF16 = jnp.bfloat16

EPS = 1e-6
POOL_WINDOWS = (2, 4, 8, 16)
POOL_HALO = 16
assert all(w & (w - 1) == 0 and w <= POOL_HALO for w in POOL_WINDOWS)
HG_HEAD = 128
HG_CHUNK = 64
HG_DIAG = 8
HG_SUB = 128
FOX_HEAD = 128
FOX_GROUP = 2
XA_HEADS = 4

LOG2E = 1.4426950408889634
LANES = 128
VMEM_CAP = 56 * 1024 * 1024
VMEM_FLOOR = 32 * 1024 * 1024


def _params(semantics, vmem_estimate):
    limit = int(min(max(vmem_estimate * 5 // 4, VMEM_FLOOR), VMEM_CAP))
    return pltpu.CompilerParams(dimension_semantics=semantics, vmem_limit_bytes=limit)


def _nbytes(shape, dtype):
    n = jnp.dtype(dtype).itemsize
    for s in shape:
        n *= s
    return n


def _rmsnorm_body(x_ref, g_ref, o_ref):
    x = x_ref[...]
    ms = jnp.mean(x * x, axis=-1, keepdims=True)
    o_ref[...] = (x * lax.rsqrt(ms + EPS) * g_ref[...]).astype(o_ref.dtype)


def _rmsnorm(x, g, out_dtype, tm=512):
    M, D = x.shape
    tm = min(tm, M)
    est = 2 * (_nbytes((tm, D), x.dtype) + _nbytes((tm, D), out_dtype)) + 3 * _nbytes((tm, D), F32)
    return pl.pallas_call(
        _rmsnorm_body,
        grid=(M // tm,),
        in_specs=[pl.BlockSpec((tm, D), lambda i: (i, 0)),
                  pl.BlockSpec((1, D), lambda i: (0, 0))],
        out_specs=pl.BlockSpec((tm, D), lambda i: (i, 0)),
        out_shape=jax.ShapeDtypeStruct((M, D), out_dtype),
        compiler_params=_params(("parallel",), est),
        name="rmsnorm",
    )(x, g.reshape(1, D).astype(F32))


def _with_bf16_weights(w_refs, wb_refs, compute):
    if not wb_refs:
        compute(*(w[...] for w in w_refs))
        return
    i = pl.program_id(1)

    @pl.when(i == 0)
    def _():
        for w, wb in zip(w_refs, wb_refs):
            wb[...] = w[...].astype(BF16)

    @pl.when(i > 0)
    def _():
        compute(*(wb[...] for wb in wb_refs))


def _row_rsqrt(ss_ref, width):
    return lax.rsqrt(jnp.sum(ss_ref[...], axis=0) * (1.0 / width) + EPS)


def _mm_body(x_ref, w_ref, cs_ref, *rest, norm_width):
    if norm_width:
        ss_ref, o_ref, *wb = rest
    else:
        o_ref, *wb = rest

    def compute(w):
        acc = jnp.dot(x_ref[...], w, preferred_element_type=F32)
        if norm_width:
            acc = acc * _row_rsqrt(ss_ref, norm_width)
        o_ref[...] = (acc * cs_ref[...]).astype(o_ref.dtype)

    _with_bf16_weights((w_ref,), wb, compute)


def _mm_res_body(x_ref, w_ref, r_ref, *rest, with_gain):
    if with_gain:
        g_ref, o_ref, hp_ref, ss_ref, *wb = rest
    else:
        o_ref, *wb = rest

    def compute(w):
        acc = r_ref[...] + jnp.dot(x_ref[...], w, preferred_element_type=F32)
        o_ref[...] = acc
        if with_gain:
            hp_ref[...] = (acc * g_ref[...]).astype(hp_ref.dtype)
            ss_ref[...] = jnp.sum(acc * acc, axis=-1, keepdims=True)

    _with_bf16_weights((w_ref,), wb, compute)


def _mm_swiglu_body(x_ref, wg_ref, wu_ref, o_ref, *wb):
    def compute(wg, wu):
        x = x_ref[...]
        g = jnp.dot(x, wg, preferred_element_type=F32)
        u = jnp.dot(x, wu, preferred_element_type=F32)
        o_ref[...] = (g * jax.nn.sigmoid(g) * u).astype(o_ref.dtype)

    _with_bf16_weights((wg_ref, wu_ref), wb, compute)


def _pick_tile(n, pref):
    t = min(pref, n)
    while n % t:
        t //= 2
    return t


def _weight_spec(layer, K, tn):
    return pl.BlockSpec((None, K, tn), lambda j, i: (layer, 0, j))


def _weight_scratch(w, K, tn):
    return [] if w.dtype == BF16 else [pltpu.VMEM((K, tn), BF16)]


def _weight_bytes(w, K, tn):
    return 2 * _nbytes((K, tn), w.dtype) + (0 if w.dtype == BF16 else _nbytes((K, tn), BF16))


def _row_steps(w, M, tm):
    extra = 0 if w.dtype == BF16 else 1
    return M // tm + extra, lambda i: jnp.maximum(i - extra, 0)


def _matmul(x, w, layer, n_cols, out_dtype, colscale=None, row_ss=None, tm=1024, tn=1024):
    M, K = x.shape
    tm, tn = _pick_tile(M, tm), _pick_tile(n_cols, tn)
    if colscale is None:
        colscale = jnp.ones((n_cols,), F32)
    steps, row = _row_steps(w, M, tm)
    ss_specs = [] if row_ss is None else [pl.BlockSpec((row_ss.shape[0], tm, 1), lambda j, i: (0, row(i), 0))]
    ss_args = [] if row_ss is None else [row_ss]
    est = 2 * (_nbytes((tm, K), x.dtype) + _nbytes((tm, tn), out_dtype)) + _weight_bytes(w, K, tn) \
        + 2 * _nbytes((tm, tn), F32)
    return pl.pallas_call(
        functools.partial(_mm_body, norm_width=0 if row_ss is None else K),
        grid=(n_cols // tn, steps),
        in_specs=[pl.BlockSpec((tm, K), lambda j, i: (row(i), 0)),
                  _weight_spec(layer, K, tn),
                  pl.BlockSpec((1, tn), lambda j, i: (0, j))] + ss_specs,
        out_specs=pl.BlockSpec((tm, tn), lambda j, i: (row(i), j)),
        out_shape=jax.ShapeDtypeStruct((M, n_cols), out_dtype),
        scratch_shapes=_weight_scratch(w, K, tn),
        compiler_params=_params(("parallel", "arbitrary"), est),
        name="matmul",
    )(x, w, colscale.reshape(1, n_cols), *ss_args)


def _matmul_residual(x, w, layer, r, gain=None, tn=1024):
    M, K = x.shape
    N = w.shape[2]
    tm = 1024 if K <= 2048 else 512
    tm, tn = _pick_tile(M, tm), _pick_tile(N, tn)
    steps, row = _row_steps(w, M, tm)
    est = 2 * (_nbytes((tm, K), x.dtype) + 2 * _nbytes((tm, tn), F32)) + _weight_bytes(w, K, tn) \
        + 2 * _nbytes((tm, tn), F32)
    tile = pl.BlockSpec((tm, tn), lambda j, i: (row(i), j))
    in_specs = [pl.BlockSpec((tm, K), lambda j, i: (row(i), 0)), _weight_spec(layer, K, tn), tile]
    out_specs, out_shape, args = tile, jax.ShapeDtypeStruct((M, N), F32), [x, w, r]
    if gain is not None:
        in_specs.append(pl.BlockSpec((1, tn), lambda j, i: (0, j)))
        args.append(gain.reshape(1, N).astype(F32))
        out_specs = [tile, tile, pl.BlockSpec((None, tm, 1), lambda j, i: (j, row(i), 0))]
        out_shape = [out_shape, jax.ShapeDtypeStruct((M, N), BF16), jax.ShapeDtypeStruct((N // tn, M, 1), F32)]
    return pl.pallas_call(
        functools.partial(_mm_res_body, with_gain=gain is not None),
        grid=(N // tn, steps),
        in_specs=in_specs,
        out_specs=out_specs,
        out_shape=out_shape,
        scratch_shapes=_weight_scratch(w, K, tn),
        compiler_params=_params(("parallel", "arbitrary"), est),
        name="matmul_residual",
    )(*args)


def _matmul_swiglu(x, wg, wu, layer, tm=1024, tn=512):
    M, K = x.shape
    N = wg.shape[2]
    tm, tn = _pick_tile(M, tm), _pick_tile(N, tn)
    steps, row = _row_steps(wg, M, tm)
    est = 2 * (_nbytes((tm, K), x.dtype) + _nbytes((tm, tn), BF16)) + 2 * _weight_bytes(wg, K, tn) \
        + 4 * _nbytes((tm, tn), F32)
    return pl.pallas_call(
        _mm_swiglu_body,
        grid=(N // tn, steps),
        in_specs=[pl.BlockSpec((tm, K), lambda j, i: (row(i), 0)),
                  _weight_spec(layer, K, tn),
                  _weight_spec(layer, K, tn)],
        out_specs=pl.BlockSpec((tm, tn), lambda j, i: (row(i), j)),
        out_shape=jax.ShapeDtypeStruct((M, N), BF16),
        scratch_shapes=_weight_scratch(wg, K, tn) + _weight_scratch(wu, K, tn),
        compiler_params=_params(("parallel", "arbitrary"), est),
        name="matmul_swiglu",
    )(x, wg, wu)


def _mm_res_norm_proj_body(*refs, n_parts):
    x_refs = refs[:n_parts]
    w_ref, r_ref, g_ref, w2_ref, cs_ref, o_ref, q_ref = refs[n_parts:]
    acc = r_ref[...]
    k0 = 0
    for x_ref in x_refs:
        kw = x_ref.shape[1]
        acc = acc + jnp.dot(x_ref[...], w_ref[k0:k0 + kw, :], preferred_element_type=F32)
        k0 += kw
    o_ref[...] = acc
    ms = jnp.mean(acc * acc, axis=-1, keepdims=True)
    h = (acc * lax.rsqrt(ms + EPS) * g_ref[...]).astype(BF16)
    q_ref[...] = (jnp.dot(h, w2_ref[...], preferred_element_type=F32) * cs_ref[...]).astype(q_ref.dtype)


def _matmul_residual_norm_proj(x_parts, w, layer, r, gain, w2, layer2, colscale, tm=512):
    M, N = r.shape
    K = w.shape[1]
    N2 = w2.shape[2]
    tm = _pick_tile(M, tm)
    est = 2 * (_nbytes((tm, K), BF16) + 2 * _nbytes((tm, N), F32) + _nbytes((tm, N2), BF16)) \
        + _nbytes((K, N), BF16) + _nbytes((N, N2), BF16) + 4 * _nbytes((tm, N), F32)
    resident = pl.Buffered(1)
    return pl.pallas_call(
        functools.partial(_mm_res_norm_proj_body, n_parts=len(x_parts)),
        grid=(M // tm,),
        in_specs=[pl.BlockSpec((tm, xp.shape[1]), lambda i: (i, 0)) for xp in x_parts]
        + [pl.BlockSpec((None, K, N), lambda i: (layer, 0, 0), pipeline_mode=resident),
           pl.BlockSpec((tm, N), lambda i: (i, 0)),
           pl.BlockSpec((1, N), lambda i: (0, 0)),
           pl.BlockSpec((None, N, N2), lambda i: (layer2, 0, 0), pipeline_mode=resident),
           pl.BlockSpec((1, N2), lambda i: (0, 0))],
        out_specs=[pl.BlockSpec((tm, N), lambda i: (i, 0)),
                   pl.BlockSpec((tm, N2), lambda i: (i, 0))],
        out_shape=[jax.ShapeDtypeStruct((M, N), F32),
                   jax.ShapeDtypeStruct((M, N2), BF16)],
        compiler_params=_params(("parallel",), est),
        name="matmul_residual_norm_proj",
    )(*x_parts, w, r, gain.reshape(1, N).astype(F32), w2, colscale.reshape(1, N2).astype(F32))


def _pool_body(u_ref, halo_ref, w_ref, sc_ref, o_ref, ext_ref, *, tt, group):
    t = pl.program_id(1)
    u = u_ref[0].astype(F32)
    ext_ref[0:POOL_HALO, :] = jnp.where(t > 0, halo_ref[0].astype(F32), 0.0)
    ext_ref[POOL_HALO:POOL_HALO + tt, :] = u
    pos = t * tt + lax.broadcasted_iota(jnp.int32, (tt, 1), 0)
    for gi, win in enumerate(POOL_WINDOWS):
        c0, c1 = gi * group, (gi + 1) * group
        tok = u[:, c0:c1]
        lvl = ext_ref[:, c0:c1]
        k = 1
        while k < win:
            lvl = lvl + pltpu.roll(lvl, k, axis=0)
            k *= 2
        acc = lvl[POOL_HALO:POOL_HALO + tt]
        cnt = jnp.minimum(pos + 1, win).astype(F32)
        p = acc / cnt - tok
        y = jnp.dot(p.astype(BF16), w_ref[gi], preferred_element_type=F32)
        o_ref[0, :, c0:c1] = (y * sc_ref[:, c0:c1]).astype(o_ref.dtype)


def _pool_mixer(z, w_pool, scale, mix_a, tt=1024):
    B, T, _ = z.shape
    tt = min(tt, T)
    group = mix_a // len(POOL_WINDOWS)
    halo_blocks = tt // POOL_HALO
    est = 2 * (_nbytes((tt, mix_a), F32) + _nbytes((tt, mix_a), BF16)) + 4 * _nbytes((tt, mix_a), F32)
    return pl.pallas_call(
        functools.partial(_pool_body, tt=tt, group=group),
        grid=(B, T // tt),
        in_specs=[pl.BlockSpec((1, tt, mix_a), lambda b, t: (b, t, 0)),
                  pl.BlockSpec((1, POOL_HALO, mix_a),
                               lambda b, t: (b, jnp.maximum(t * halo_blocks - 1, 0), 0)),
                  pl.BlockSpec((len(POOL_WINDOWS), group, group), lambda b, t: (0, 0, 0)),
                  pl.BlockSpec((1, mix_a), lambda b, t: (0, 0))],
        out_specs=pl.BlockSpec((1, tt, mix_a), lambda b, t: (b, t, 0)),
        out_shape=jax.ShapeDtypeStruct((B, T, mix_a), BF16),
        scratch_shapes=[pltpu.VMEM((tt + POOL_HALO, mix_a), F32)],
        compiler_params=_params(("parallel", "parallel"), est),
        name="pool_mixer",
    )(z, z, w_pool, scale.reshape(1, mix_a).astype(F32))


def _hgrn_subtile(qf, kk, v, logf, st):
    n = qf.shape[0]
    C, R = HG_CHUNK, HG_DIAG
    row = lax.broadcasted_iota(jnp.int32, (n, HG_HEAD), 0)

    pos = row & (C - 1)
    b = logf
    sh = 1
    while sh < C:
        b = b + jnp.where(pos >= sh, pltpu.roll(b, sh, axis=0), 0.0)
        sh *= 2

    nb = n // R
    b3, q3, c3, v3 = (a.reshape(nb, R, HG_HEAD) for a in (b, qf, b - jnp.log2(kk), v))
    tpos = lax.broadcasted_iota(jnp.int32, (1, R, 1), 1)
    od = jnp.zeros((nb, R, HG_HEAD), F32)
    for s in range(R):
        w = q3 * jnp.exp2(b3 - c3[:, s:s + 1, :])
        a = jnp.sum(w, axis=-1, keepdims=True)
        a = jnp.where(tpos >= s, a, 0.0)
        od = od + a * v3[:, s:s + 1, :]
    o = od.reshape(n, HG_HEAD)

    same = lax.broadcasted_iota(jnp.int32, (n, n), 0) ^ lax.broadcasted_iota(jnp.int32, (n, n), 1)
    a_off = jnp.zeros((n, n), F32)
    h = R
    while h < C:
        g = 2 * h
        ref = b.reshape(n // g, g, HG_HEAD)[:, h - 1:h, :]
        ref = jnp.broadcast_to(ref, (n // g, g, HG_HEAD)).reshape(n, HG_HEAD)
        e = jnp.exp2(-jnp.abs(b - ref))
        right = (row & (g - 1)) >= h
        qt = jnp.where(right, qf * e, 0.0).astype(BF16)
        kt = jnp.where(right, 0.0, kk * e).astype(BF16)
        a = lax.dot_general(qt, kt, (((1,), (1,)), ((), ())), preferred_element_type=F32)
        a_off = a_off + jnp.where(same < g, a, 0.0)
        h = g
    v_bf = v.astype(BF16)
    o = o + jnp.dot(a_off.astype(BF16), v_bf, preferred_element_type=F32)

    nc = n // C
    bc = b.reshape(nc, C, HG_HEAD)
    b_last = bc[:, C - 1:C, :]
    q_in = (qf * jnp.exp2(b)).astype(BF16)
    k_out = (kk.reshape(nc, C, HG_HEAD) * jnp.exp2(b_last - bc)).astype(BF16)
    dec = jnp.exp2(b_last)
    pieces = []
    for c in range(nc):
        rows = slice(c * C, (c + 1) * C)
        o_int = lax.dot_general(q_in[rows], st.astype(BF16), (((1,), (1,)), ((), ())),
                                preferred_element_type=F32)
        pieces.append(o[rows] + o_int)
        upd = lax.dot_general(v_bf[rows], k_out[c], (((0,), (0,)), ((), ())),
                              preferred_element_type=F32)
        st = st * dec[c] + upd
    return jnp.concatenate(pieces, axis=0), st


def _hgrn_body(lbt_ref, ng_ref, q_ref, f_ref, i_ref, g_ref, o_ref, st_ref, *, layer, tt, sub):
    t = pl.program_id(2)

    @pl.when(t == 0)
    def _():
        st_ref[...] = jnp.zeros_like(st_ref)

    lbt = lbt_ref[...]
    e = jnp.exp(lbt - jnp.max(lbt, axis=0, keepdims=True))
    sm = e / jnp.sum(e, axis=0, keepdims=True)
    lb = jnp.sum(sm[1:layer + 2], axis=0, keepdims=True)

    st = st_ref[...]
    for c in range(tt // sub):
        rows = slice(c * sub, (c + 1) * sub)
        f = lb + (1.0 - lb) * jax.nn.sigmoid(f_ref[0, rows, :].astype(F32))
        q = q_ref[0, rows, :].astype(F32)
        qf = q * jax.nn.sigmoid(q) * (HG_HEAD ** -0.5)
        o, st = _hgrn_subtile(qf, 1.0 - f, i_ref[0, rows, :].astype(F32), jnp.log2(f), st)
        ms = jnp.mean(o * o, axis=-1, keepdims=True)
        o = o * lax.rsqrt(ms + EPS) * ng_ref[...]
        g = g_ref[0, rows, :].astype(F32)
        o_ref[0, rows, :] = (o * (g * jax.nn.sigmoid(g))).astype(o_ref.dtype)
    st_ref[...] = st


def _hgrn_mixer(z, lb_table, norm_g, layer, mix_a, mix_b, tt=2048):
    B, T, _ = z.shape
    tt = min(tt, T)
    sub = min(HG_SUB, tt)
    heads = mix_b // HG_HEAD
    c0 = mix_a // HG_HEAD
    est = 2 * 5 * _nbytes((tt, HG_HEAD), F32) + 32 * _nbytes((sub, HG_HEAD), F32) + 6 * _nbytes((sub, sub), F32)

    def col(off):
        return pl.BlockSpec((1, tt, HG_HEAD), lambda b, h, t, off=off: (b, t, c0 + off * heads + h))

    return pl.pallas_call(
        functools.partial(_hgrn_body, layer=layer, tt=tt, sub=sub),
        grid=(B, heads, T // tt),
        in_specs=[pl.BlockSpec((lb_table.shape[0], HG_HEAD), lambda b, h, t: (0, h)),
                  pl.BlockSpec((1, HG_HEAD), lambda b, h, t: (0, 0)),
                  col(0), col(1), col(2), col(3)],
        out_specs=pl.BlockSpec((1, tt, HG_HEAD), lambda b, h, t: (b, t, h)),
        out_shape=jax.ShapeDtypeStruct((B, T, mix_b), BF16),
        scratch_shapes=[pltpu.VMEM((HG_HEAD, HG_HEAD), F32)],
        compiler_params=_params(("parallel", "parallel", "arbitrary"), est),
        name="hgrn2_mixer",
    )(lb_table.astype(F32), norm_g.reshape(1, HG_HEAD).astype(F32), z, z, z, z)


def _fgate_body(h_ref, w_ref, b_ref, *rest, tt, norm_width):
    if norm_width:
        ss_ref, o_ref, carry_ref = rest
    else:
        o_ref, carry_ref = rest
    t = pl.program_id(1)

    @pl.when(t == 0)
    def _():
        carry_ref[...] = jnp.zeros_like(carry_ref)

    fl = jnp.dot(h_ref[0], w_ref[...], preferred_element_type=F32)
    if norm_width:
        fl = fl * _row_rsqrt(ss_ref, norm_width)
    fl = fl + b_ref[...]
    c = (jnp.minimum(fl, 0.0) - jnp.log1p(jnp.exp(-jnp.abs(fl)))) * LOG2E
    row = lax.broadcasted_iota(jnp.int32, c.shape, 0)
    sh = 1
    while sh < tt:
        c = c + jnp.where(row >= sh, pltpu.roll(c, sh, axis=0), 0.0)
        sh *= 2
    c = c + carry_ref[...]
    o_ref[0] = c
    carry_ref[...] = c[tt - 1:tt, :]


def _fox_gates(h, wf, bf, row_ss=None, tt=512):
    B, T, D = h.shape
    tt = min(tt, T)
    ss_specs = [] if row_ss is None else [pl.BlockSpec((row_ss.shape[0], None, tt, 1), lambda b, t: (0, b, t, 0))]
    ss_args = [] if row_ss is None else [row_ss]
    est = 2 * (_nbytes((tt, D), BF16) + _nbytes((D, LANES), BF16) + _nbytes((tt, LANES), F32)) \
        + 8 * _nbytes((tt, LANES), F32)
    return pl.pallas_call(
        functools.partial(_fgate_body, tt=tt, norm_width=0 if row_ss is None else D),
        grid=(B, T // tt),
        in_specs=[pl.BlockSpec((1, tt, D), lambda b, t: (b, t, 0)),
                  pl.BlockSpec((D, LANES), lambda b, t: (0, 0)),
                  pl.BlockSpec((1, LANES), lambda b, t: (0, 0))] + ss_specs,
        out_specs=pl.BlockSpec((1, tt, LANES), lambda b, t: (b, t, 0)),
        out_shape=jax.ShapeDtypeStruct((B, T, LANES), F32),
        scratch_shapes=[pltpu.VMEM((1, LANES), F32)],
        compiler_params=_params(("parallel", "arbitrary"), est),
        name="fox_gates",
    )(h, wf, bf, *ss_args)


def _fox_body(q_ref, k_ref, v_ref, f_ref, fr_ref, o_ref, vt_ref, fk_ref, p_ref, acc_ref, *, tq, seq, group):
    hg = pl.program_id(1)
    i = pl.program_id(2)
    reps = tq // LANES
    heads = range(group)

    def head_cols(g):
        return slice(g * FOX_HEAD, (g + 1) * FOX_HEAD)

    @pl.when(i == 0)
    def _():
        for c in range(seq // tq):
            rows = slice(c * tq, (c + 1) * tq)
            fblk = f_ref[0, rows, :]
            lane = lax.broadcasted_iota(jnp.int32, fblk.shape, 1)
            for g in heads:
                vt_ref[g, :, rows] = v_ref[0, rows, head_cols(g)].T
                col = jnp.sum(jnp.where(lane == hg * group + g, fblk, 0.0), axis=-1, keepdims=True)
                fk_ref[g, rows, :] = jnp.broadcast_to(col, fblk.shape)

    qt = [q_ref[0, :, head_cols(g)].T for g in heads]
    fq = [fr_ref[0, g, i] for g in heads]

    def scores(g, j):
        start = pl.multiple_of(j * tq, tq)
        return jnp.dot(k_ref[0, pl.ds(start, tq), head_cols(g)], qt[g], preferred_element_type=F32)

    def softmax_block(g, j, st, m_prev, l_prev, diagonal):
        start = pl.multiple_of(j * tq, tq)
        t = st - jnp.concatenate([fk_ref[g, pl.ds(start, tq), :]] * reps, axis=1)
        if diagonal:
            key = lax.broadcasted_iota(jnp.int32, (tq, tq), 0)
            qry = lax.broadcasted_iota(jnp.int32, (tq, tq), 1)
            t = jnp.where(key <= qry, t, -jnp.inf)
        m_new = jnp.maximum(m_prev, jnp.max(t, axis=0, keepdims=True) + fq[g])
        alpha = jnp.exp2(m_prev - m_new)
        p = jnp.exp2(t - (m_new - fq[g]))
        l_new = alpha * l_prev + jnp.sum(p, axis=0, keepdims=True)
        return p.astype(BF16), m_new, l_new, alpha

    def weighted_values(g, j, p):
        start = pl.multiple_of(j * tq, tq)
        return jnp.dot(vt_ref[g, :, pl.ds(start, tq)], p, preferred_element_type=F32)

    def block(j, j_prev, stats, first):
        out = []
        for g in heads:
            m, l = stats[g]
            st = scores(g, j)
            if first:
                p, m, l, _ = softmax_block(g, j, st, m, l, True)
                acc_ref[g] = jnp.zeros((FOX_HEAD, tq), F32)
            else:
                pv_prev = weighted_values(g, j_prev, p_ref[g])
                p, m, l, alpha = softmax_block(g, j, st, m, l, False)
                acc_ref[g] = (acc_ref[g] + pv_prev) * alpha
            p_ref[g] = p
            out.append((m, l))
        return tuple(out)

    stats = tuple((jnp.full((1, tq), -jnp.inf, F32), jnp.zeros((1, tq), F32)) for _ in heads)
    stats = block(i, i, stats, True)
    def run(j0, n, c):
        for d in range(n):
            c = block(j0 + d, jnp.where(j0 + d == 0, i, j0 + d - 1), c, False)
        return c

    one, two = i & 1, i & 2
    stats = lax.cond(one == 1, lambda c: run(0, 1, c), lambda c: c, stats)
    stats = lax.cond(two == 2, lambda c: run(one, 2, c), lambda c: c, stats)
    stats = lax.fori_loop(0, i // 4, lambda jj, c: run(one + two + 4 * jj, 4, c), stats)
    j_last = jnp.where(i == 0, i, i - 1)
    for g in heads:
        acc = acc_ref[g] + weighted_values(g, j_last, p_ref[g])
        o_ref[0, :, head_cols(g)] = (acc / stats[g][1]).astype(o_ref.dtype).T


def _fox_attention(qkv, f, frow, heads, tq, group=FOX_GROUP):
    B, T, _ = qkv.shape
    nq = T // tq
    gw = group * FOX_HEAD
    ngrp = heads // group
    est = 2 * (2 * _nbytes((T, gw), BF16) + 2 * _nbytes((tq, gw), BF16)
               + _nbytes((T, LANES), F32) + group * _nbytes((nq, 8, tq), F32)) \
        + group * (_nbytes((T, FOX_HEAD), BF16) + _nbytes((T, LANES), F32) + 5 * _nbytes((tq, tq), F32))
    return pl.pallas_call(
        functools.partial(_fox_body, tq=tq, seq=T, group=group),
        grid=(B, ngrp, nq),
        in_specs=[pl.BlockSpec((1, tq, gw), lambda b, h, i: (b, i, h)),
                  pl.BlockSpec((1, T, gw), lambda b, h, i: (b, 0, ngrp + h)),
                  pl.BlockSpec((1, T, gw), lambda b, h, i: (b, 0, 2 * ngrp + h)),
                  pl.BlockSpec((1, T, LANES), lambda b, h, i: (b, 0, 0)),
                  pl.BlockSpec((1, group, nq, 1, tq), lambda b, h, i: (b, h, 0, 0, 0))],
        out_specs=pl.BlockSpec((1, tq, gw), lambda b, h, i: (b, i, h)),
        out_shape=jax.ShapeDtypeStruct((B, T, heads * FOX_HEAD), BF16),
        scratch_shapes=[pltpu.VMEM((group, FOX_HEAD, T), BF16),
                        pltpu.VMEM((group, T, LANES), F32),
                        pltpu.VMEM((group, tq, tq), BF16),
                        pltpu.VMEM((group, FOX_HEAD, tq), F32)],
        compiler_params=_params(("parallel", "parallel", "arbitrary"), est),
        name="fox_attention",
    )(qkv, qkv, qkv, f, frow)


def _xattn_out_body(q_ref, kv_ref, w_ref, r_ref, g_ref, o_ref, h_ref, a_ref, *, d_model):
    hd = d_model // XA_HEADS
    for hh in range(XA_HEADS):
        q = q_ref[:, hh * hd:(hh + 1) * hd]
        k = kv_ref[:, hh * hd:(hh + 1) * hd]
        v = kv_ref[:, d_model + hh * hd:d_model + (hh + 1) * hd]
        s = lax.dot_general(q, k, (((1,), (1,)), ((), ())), preferred_element_type=F32)
        p = jnp.exp(s - jnp.max(s, axis=-1, keepdims=True))
        p = p / jnp.sum(p, axis=-1, keepdims=True)
        a_ref[:, hh * hd:(hh + 1) * hd] = jnp.dot(p.astype(BF16), v, preferred_element_type=F32).astype(a_ref.dtype)
    acc = r_ref[...] + jnp.dot(a_ref[...], w_ref[...], preferred_element_type=F32)
    o_ref[...] = acc
    ms = jnp.mean(acc * acc, axis=-1, keepdims=True)
    h_ref[...] = (acc * lax.rsqrt(ms + EPS) * g_ref[...]).astype(h_ref.dtype)


def _xattn_out_norm(q, kv, w, layer, r, gain, seq, tm=512):
    M, D = q.shape
    n_mem = kv.shape[1]
    tm = _pick_tile(seq, tm)
    per_batch = seq // tm
    est = 2 * (_nbytes((tm, D), BF16) + _nbytes((n_mem, 2 * D), BF16) + _nbytes((D, D), BF16)
               + 2 * _nbytes((tm, D), F32) + _nbytes((tm, D), BF16)) \
        + _nbytes((tm, D), BF16) + 3 * _nbytes((tm, D), F32)
    row = pl.BlockSpec((tm, D), lambda i: (i, 0))
    return pl.pallas_call(
        functools.partial(_xattn_out_body, d_model=D),
        grid=(M // tm,),
        in_specs=[row,
                  pl.BlockSpec((None, n_mem, 2 * D), lambda i: (i // per_batch, 0, 0)),
                  pl.BlockSpec((None, D, D), lambda i: (layer, 0, 0)),
                  row,
                  pl.BlockSpec((1, D), lambda i: (0, 0))],
        out_specs=[row, row],
        out_shape=[jax.ShapeDtypeStruct((M, D), F32),
                   jax.ShapeDtypeStruct((M, D), BF16)],
        scratch_shapes=[pltpu.VMEM((tm, D), BF16)],
        compiler_params=_params(("parallel",), est),
        name="xattn_out_norm",
    )(q, kv, w, r, gain.reshape(1, D).astype(F32))


def kernel(x, mem, lb_table, ev_norm, ev_w_in, ev_w_pool, ev_pool_scale, ev_hg_norm, ev_w_out,
           od_norm, od_w_in, od_b_f, od_w_out, xa_norm, xa_mem_norm, xa_wq, xa_wkv, xa_wo,
           ffn_norm, ffn_w_gate, ffn_w_up, ffn_w_down, final_norm):
    B, T, D = x.shape
    M = B * T
    depth = xa_norm.shape[0]
    mix_a = ev_pool_scale.shape[1]
    mix_b = lb_table.shape[1]
    fox_heads = od_b_f.shape[1]
    n_mem = mem.shape[1]
    xa_scale = (D // XA_HEADS) ** -0.5
    fox_tq = _pick_tile(T, 512)

    xs = x.reshape(M, D)
    mem2 = mem.reshape(B * n_mem, D)
    w_down = ffn_w_down.astype(BF16)
    ev_wo, od_wo, xa_wo_b = ev_w_out.astype(BF16), od_w_out.astype(BF16), xa_wo.astype(BF16)
    xa_wq_b = xa_wq.astype(BF16)
    xa_qscale = jnp.full((D,), xa_scale, F32)
    pre = None
    for l in range(depth):
        if pre is None:
            h, ss = _rmsnorm(xs, ev_norm[l // 2] if l % 2 == 0 else od_norm[l // 2], BF16), None
        else:
            h, ss = pre
        if l % 2 == 0:
            e = l // 2
            z = _matmul(h, ev_w_in, e, ev_w_in.shape[2], BF16, row_ss=ss).reshape(B, T, -1)
            ya = _pool_mixer(z, ev_w_pool[e].astype(BF16), ev_pool_scale[e], mix_a)
            yb = _hgrn_mixer(z, lb_table, ev_hg_norm[e], l, mix_a, mix_b)
            xs, q = _matmul_residual_norm_proj([ya.reshape(M, mix_a), yb.reshape(M, mix_b)], ev_wo, e, xs, xa_norm[l],
                                               xa_wq_b, l, xa_qscale)
        else:
            o = l // 2
            qscale = jnp.concatenate([jnp.full((D,), LOG2E * FOX_HEAD ** -0.5, F32), jnp.ones((2 * D,), F32)])
            w_in = od_w_in.astype(BF16)
            qkv = _matmul(h, w_in, o, 3 * D, BF16, colscale=qscale, row_ss=ss).reshape(B, T, 3 * D)
            wf = jnp.zeros((D, LANES), BF16).at[:, :fox_heads].set(w_in[o, :, 3 * D:])
            bf = jnp.zeros((1, LANES), F32).at[0, :fox_heads].set(od_b_f[o].astype(F32))
            ss4 = None if ss is None else ss.reshape(ss.shape[0], B, T, 1)
            f = _fox_gates(h.reshape(B, T, D), wf, bf, ss4)
            frow = f[..., :fox_heads].transpose(0, 2, 1).reshape(B, fox_heads, T // fox_tq, 1, fox_tq)
            y = _fox_attention(qkv, f, frow, fox_heads, fox_tq).reshape(M, D)
            xs, q = _matmul_residual_norm_proj([y], od_wo, o, xs, xa_norm[l], xa_wq_b, l, xa_qscale)

        mn = _rmsnorm(mem2, xa_mem_norm[l], BF16)
        kv = _matmul(mn, xa_wkv, l, 2 * D, BF16)
        xs, h = _xattn_out_norm(q, kv.reshape(B, n_mem, 2 * D), xa_wo_b, l, xs, ffn_norm[l], T)

        act = _matmul_swiglu(h, ffn_w_gate, ffn_w_up, l)
        if l + 1 < depth:
            nxt = l + 1
            gain = ev_norm[nxt // 2] if nxt % 2 == 0 else od_norm[nxt // 2]
            xs, hp, ss = _matmul_residual(act, w_down, l, xs, gain=gain)
            pre = (hp, ss)
        else:
            xs = _matmul_residual(act, w_down, l, xs)
    return _rmsnorm(xs, final_norm, x.dtype).reshape(B, T, D)
```

```python
import functools

import jax
import jax.numpy as jnp
from jax import lax
from jax.experimental import pallas as pl
from jax.experimental.pallas import tpu as pltpu

F32 = jnp.float32
BF16 = jnp.bfloat16

EPS = 1e-6
POOL_WINDOWS = (2, 4, 8, 16)
POOL_HALO = 16
assert all(w & (w - 1) == 0 and w <= POOL_HALO for w in POOL_WINDOWS)
HG_HEAD = 128
HG_CHUNK = 64
HG_DIAG = 8
HG_SUB = 128
FOX_HEAD = 128
FOX_GROUP = 4
XA_HEADS = 4

LOG2E = 1.4426950408889634
LANES = 128
VMEM_CAP = 56 * 1024 * 1024
VMEM_FLOOR = 32 * 1024 * 1024


def _params(semantics, vmem_estimate):
    limit = int(min(max(vmem_estimate * 5 // 4, VMEM_FLOOR), VMEM_CAP))
    return pltpu.CompilerParams(dimension_semantics=semantics, vmem_limit_bytes=limit)


def _nbytes(shape, dtype):
    n = jnp.dtype(dtype).itemsize
    for s in shape:
        n *= s
    return n


def _rmsnorm_body(x_ref, g_ref, o_ref):
    x = x_ref[...]
    ms = jnp.mean(x * x, axis=-1, keepdims=True)
    o_ref[...] = (x * lax.rsqrt(ms + EPS) * g_ref[...]).astype(o_ref.dtype)


def _rmsnorm(x, g, out_dtype, tm=512):
    M, D = x.shape
    tm = min(tm, M)
    est = 2 * (_nbytes((tm, D), x.dtype) + _nbytes((tm, D), out_dtype)) + 3 * _nbytes((tm, D), F32)
    return pl.pallas_call(
        _rmsnorm_body,
        grid=(M // tm,),
        in_specs=[pl.BlockSpec((tm, D), lambda i: (i, 0)),
                  pl.BlockSpec((1, D), lambda i: (0, 0))],
        out_specs=pl.BlockSpec((tm, D), lambda i: (i, 0)),
        out_shape=jax.ShapeDtypeStruct((M, D), out_dtype),
        compiler_params=_params(("parallel",), est),
        name="rmsnorm",
    )(x, g.reshape(1, D).astype(F32))


def _with_bf16_weights(w_refs, wb_refs, compute):
    if not wb_refs:
        compute(*(w[...] for w in w_refs))
        return
    i = pl.program_id(1)

    @pl.when(i == 0)
    def _():
        for w, wb in zip(w_refs, wb_refs):
            wb[...] = w[...].astype(BF16)

    @pl.when(i > 0)
    def _():
        compute(*(wb[...] for wb in wb_refs))


def _row_rsqrt(ss_ref, width):
    return lax.rsqrt(jnp.sum(ss_ref[...], axis=0) * (1.0 / width) + EPS)


def _mm_body(x_ref, w_ref, cs_ref, *rest, norm_width):
    if norm_width:
        ss_ref, o_ref, *wb = rest
    else:
        o_ref, *wb = rest

    def compute(w):
        acc = jnp.dot(x_ref[...], w, preferred_element_type=F32)
        if norm_width:
            acc = acc * _row_rsqrt(ss_ref, norm_width)
        o_ref[...] = (acc * cs_ref[...]).astype(o_ref.dtype)

    _with_bf16_weights((w_ref,), wb, compute)


def _mm_res_body(x_ref, w_ref, r_ref, *rest, with_gain):
    if with_gain:
        g_ref, o_ref, hp_ref, ss_ref, *wb = rest
    else:
        o_ref, *wb = rest

    def compute(w):
        acc = r_ref[...] + jnp.dot(x_ref[...], w, preferred_element_type=F32)
        o_ref[...] = acc
        if with_gain:
            hp_ref[...] = (acc * g_ref[...]).astype(hp_ref.dtype)
            ss_ref[...] = jnp.sum(acc * acc, axis=-1, keepdims=True)

    _with_bf16_weights((w_ref,), wb, compute)


def _mm_swiglu_body(x_ref, wg_ref, wu_ref, o_ref, *wb):
    def compute(wg, wu):
        x = x_ref[...]
        g = jnp.dot(x, wg, preferred_element_type=F32)
        u = jnp.dot(x, wu, preferred_element_type=F32)
        o_ref[...] = (g * jax.nn.sigmoid(g) * u).astype(o_ref.dtype)

    _with_bf16_weights((wg_ref, wu_ref), wb, compute)


def _pick_tile(n, pref):
    t = min(pref, n)
    while n % t:
        t //= 2
    return t


def _weight_spec(layer, K, tn):
    return pl.BlockSpec((None, K, tn), lambda j, i: (layer, 0, j))


def _weight_scratch(w, K, tn):
    return [] if w.dtype == BF16 else [pltpu.VMEM((K, tn), BF16)]


def _weight_bytes(w, K, tn):
    return 2 * _nbytes((K, tn), w.dtype) + (0 if w.dtype == BF16 else _nbytes((K, tn), BF16))


def _row_steps(w, M, tm):
    extra = 0 if w.dtype == BF16 else 1
    return M // tm + extra, lambda i: jnp.maximum(i - extra, 0)


def _matmul(x, w, layer, n_cols, out_dtype, colscale=None, row_ss=None, tm=1024, tn=1024):
    M, K = x.shape
    tm, tn = _pick_tile(M, tm), _pick_tile(n_cols, tn)
    if colscale is None:
        colscale = jnp.ones((n_cols,), F32)
    steps, row = _row_steps(w, M, tm)
    ss_specs = [] if row_ss is None else [pl.BlockSpec((row_ss.shape[0], tm, 1), lambda j, i: (0, row(i), 0))]
    ss_args = [] if row_ss is None else [row_ss]
    est = 2 * (_nbytes((tm, K), x.dtype) + _nbytes((tm, tn), out_dtype)) + _weight_bytes(w, K, tn) \
        + 2 * _nbytes((tm, tn), F32)
    return pl.pallas_call(
        functools.partial(_mm_body, norm_width=0 if row_ss is None else K),
        grid=(n_cols // tn, steps),
        in_specs=[pl.BlockSpec((tm, K), lambda j, i: (row(i), 0)),
                  _weight_spec(layer, K, tn),
                  pl.BlockSpec((1, tn), lambda j, i: (0, j))] + ss_specs,
        out_specs=pl.BlockSpec((tm, tn), lambda j, i: (row(i), j)),
        out_shape=jax.ShapeDtypeStruct((M, n_cols), out_dtype),
        scratch_shapes=_weight_scratch(w, K, tn),
        compiler_params=_params(("parallel", "arbitrary"), est),
        name="matmul",
    )(x, w, colscale.reshape(1, n_cols), *ss_args)


def _matmul_residual(x, w, layer, r, gain=None, tn=1024):
    M, K = x.shape
    N = w.shape[2]
    tm = 1024 if K <= 2048 else 512
    tm, tn = _pick_tile(M, tm), _pick_tile(N, tn)
    steps, row = _row_steps(w, M, tm)
    est = 2 * (_nbytes((tm, K), x.dtype) + 2 * _nbytes((tm, tn), F32)) + _weight_bytes(w, K, tn) \
        + 2 * _nbytes((tm, tn), F32)
    tile = pl.BlockSpec((tm, tn), lambda j, i: (row(i), j))
    in_specs = [pl.BlockSpec((tm, K), lambda j, i: (row(i), 0)), _weight_spec(layer, K, tn), tile]
    out_specs, out_shape, args = tile, jax.ShapeDtypeStruct((M, N), F32), [x, w, r]
    if gain is not None:
        in_specs.append(pl.BlockSpec((1, tn), lambda j, i: (0, j)))
        args.append(gain.reshape(1, N).astype(F32))
        out_specs = [tile, tile, pl.BlockSpec((None, tm, 1), lambda j, i: (j, row(i), 0))]
        out_shape = [out_shape, jax.ShapeDtypeStruct((M, N), BF16), jax.ShapeDtypeStruct((N // tn, M, 1), F32)]
    return pl.pallas_call(
        functools.partial(_mm_res_body, with_gain=gain is not None),
        grid=(N // tn, steps),
        in_specs=in_specs,
        out_specs=out_specs,
        out_shape=out_shape,
        scratch_shapes=_weight_scratch(w, K, tn),
        compiler_params=_params(("parallel", "arbitrary"), est),
        name="matmul_residual",
    )(*args)


def _matmul_swiglu(x, wg, wu, layer, tm=1024, tn=512):
    M, K = x.shape
    N = wg.shape[2]
    tm, tn = _pick_tile(M, tm), _pick_tile(N, tn)
    steps, row = _row_steps(wg, M, tm)
    est = 2 * (_nbytes((tm, K), x.dtype) + _nbytes((tm, tn), BF16)) + 2 * _weight_bytes(wg, K, tn) \
        + 4 * _nbytes((tm, tn), F32)
    return pl.pallas_call(
        _mm_swiglu_body,
        grid=(N // tn, steps),
        in_specs=[pl.BlockSpec((tm, K), lambda j, i: (row(i), 0)),
                  _weight_spec(layer, K, tn),
                  _weight_spec(layer, K, tn)],
        out_specs=pl.BlockSpec((tm, tn), lambda j, i: (row(i), j)),
        out_shape=jax.ShapeDtypeStruct((M, N), BF16),
        scratch_shapes=_weight_scratch(wg, K, tn) + _weight_scratch(wu, K, tn),
        compiler_params=_params(("parallel", "arbitrary"), est),
        name="matmul_swiglu",
    )(x, wg, wu)


def _mm_res_norm_proj_body(*refs, n_parts):
    x_refs = refs[:n_parts]
    w_ref, r_ref, g_ref, w2_ref, cs_ref, o_ref, q_ref = refs[n_parts:]
    acc = r_ref[...]
    k0 = 0
    for x_ref in x_refs:
        kw = x_ref.shape[1]
        acc = acc + jnp.dot(x_ref[...], w_ref[k0:k0 + kw, :], preferred_element_type=F32)
        k0 += kw
    o_ref[...] = acc
    ms = jnp.mean(acc * acc, axis=-1, keepdims=True)
    h = (acc * lax.rsqrt(ms + EPS) * g_ref[...]).astype(BF16)
    q_ref[...] = (jnp.dot(h, w2_ref[...], preferred_element_type=F32) * cs_ref[...]).astype(q_ref.dtype)


def _matmul_residual_norm_proj(x_parts, w, layer, r, gain, w2, layer2, colscale, tm=512):
    M, N = r.shape
    K = w.shape[1]
    N2 = w2.shape[2]
    tm = _pick_tile(M, tm)
    est = 2 * (_nbytes((tm, K), BF16) + 2 * _nbytes((tm, N), F32) + _nbytes((tm, N2), BF16)) \
        + _nbytes((K, N), BF16) + _nbytes((N, N2), BF16) + 4 * _nbytes((tm, N), F32)
    resident = pl.Buffered(1)
    return pl.pallas_call(
        functools.partial(_mm_res_norm_proj_body, n_parts=len(x_parts)),
        grid=(M // tm,),
        in_specs=[pl.BlockSpec((tm, xp.shape[1]), lambda i: (i, 0)) for xp in x_parts]
        + [pl.BlockSpec((None, K, N), lambda i: (layer, 0, 0), pipeline_mode=resident),
           pl.BlockSpec((tm, N), lambda i: (i, 0)),
           pl.BlockSpec((1, N), lambda i: (0, 0)),
           pl.BlockSpec((None, N, N2), lambda i: (layer2, 0, 0), pipeline_mode=resident),
           pl.BlockSpec((1, N2), lambda i: (0, 0))],
        out_specs=[pl.BlockSpec((tm, N), lambda i: (i, 0)),
                   pl.BlockSpec((tm, N2), lambda i: (i, 0))],
        out_shape=[jax.ShapeDtypeStruct((M, N), F32),
                   jax.ShapeDtypeStruct((M, N2), BF16)],
        compiler_params=_params(("parallel",), est),
        name="matmul_residual_norm_proj",
    )(*x_parts, w, r, gain.reshape(1, N).astype(F32), w2, colscale.reshape(1, N2).astype(F32))


def _pool_body(u_ref, halo_ref, w_ref, sc_ref, o_ref, ext_ref, *, tt, group):
    t = pl.program_id(1)
    u = u_ref[0].astype(F32)
    ext_ref[0:POOL_HALO, :] = jnp.where(t > 0, halo_ref[0].astype(F32), 0.0)
    ext_ref[POOL_HALO:POOL_HALO + tt, :] = u
    pos = t * tt + lax.broadcasted_iota(jnp.int32, (tt, 1), 0)
    for gi, win in enumerate(POOL_WINDOWS):
        c0, c1 = gi * group, (gi + 1) * group
        tok = u[:, c0:c1]
        lvl = ext_ref[:, c0:c1]
        k = 1
        while k < win:
            lvl = lvl + pltpu.roll(lvl, k, axis=0)
            k *= 2
        acc = lvl[POOL_HALO:POOL_HALO + tt]
        cnt = jnp.minimum(pos + 1, win).astype(F32)
        p = acc / cnt - tok
        y = jnp.dot(p.astype(BF16), w_ref[gi], preferred_element_type=F32)
        o_ref[0, :, c0:c1] = (y * sc_ref[:, c0:c1]).astype(o_ref.dtype)


def _pool_mixer(z, w_pool, scale, mix_a, tt=1024):
    B, T, _ = z.shape
    tt = min(tt, T)
    group = mix_a // len(POOL_WINDOWS)
    halo_blocks = tt // POOL_HALO
    est = 2 * (_nbytes((tt, mix_a), F32) + _nbytes((tt, mix_a), BF16)) + 4 * _nbytes((tt, mix_a), F32)
    return pl.pallas_call(
        functools.partial(_pool_body, tt=tt, group=group),
        grid=(B, T // tt),
        in_specs=[pl.BlockSpec((1, tt, mix_a), lambda b, t: (b, t, 0)),
                  pl.BlockSpec((1, POOL_HALO, mix_a),
                               lambda b, t: (b, jnp.maximum(t * halo_blocks - 1, 0), 0)),
                  pl.BlockSpec((len(POOL_WINDOWS), group, group), lambda b, t: (0, 0, 0)),
                  pl.BlockSpec((1, mix_a), lambda b, t: (0, 0))],
        out_specs=pl.BlockSpec((1, tt, mix_a), lambda b, t: (b, t, 0)),
        out_shape=jax.ShapeDtypeStruct((B, T, mix_a), BF16),
        scratch_shapes=[pltpu.VMEM((tt + POOL_HALO, mix_a), F32)],
        compiler_params=_params(("parallel", "parallel"), est),
        name="pool_mixer",
    )(z, z, w_pool, scale.reshape(1, mix_a).astype(F32))


def _hgrn_subtile(qf, kk, v, logf, st):
    n = qf.shape[0]
    C, R = HG_CHUNK, HG_DIAG
    row = lax.broadcasted_iota(jnp.int32, (n, HG_HEAD), 0)

    pos = row & (C - 1)
    b = logf
    sh = 1
    while sh < C:
        b = b + jnp.where(pos >= sh, pltpu.roll(b, sh, axis=0), 0.0)
        sh *= 2

    nb = n // R
    b3, q3, c3, v3 = (a.reshape(nb, R, HG_HEAD) for a in (b, qf, b - jnp.log2(kk), v))
    tpos = lax.broadcasted_iota(jnp.int32, (1, R, 1), 1)
    od = jnp.zeros((nb, R, HG_HEAD), F32)
    for s in range(R):
        w = q3 * jnp.exp2(b3 - c3[:, s:s + 1, :])
        a = jnp.sum(w, axis=-1, keepdims=True)
        a = jnp.where(tpos >= s, a, 0.0)
        od = od + a * v3[:, s:s + 1, :]
    o = od.reshape(n, HG_HEAD)

    same = lax.broadcasted_iota(jnp.int32, (n, n), 0) ^ lax.broadcasted_iota(jnp.int32, (n, n), 1)
    a_off = jnp.zeros((n, n), F32)
    h = R
    while h < C:
        g = 2 * h
        ref = b.reshape(n // g, g, HG_HEAD)[:, h - 1:h, :]
        ref = jnp.broadcast_to(ref, (n // g, g, HG_HEAD)).reshape(n, HG_HEAD)
        e = jnp.exp2(-jnp.abs(b - ref))
        right = (row & (g - 1)) >= h
        qt = jnp.where(right, qf * e, 0.0).astype(BF16)
        kt = jnp.where(right, 0.0, kk * e).astype(BF16)
        a = lax.dot_general(qt, kt, (((1,), (1,)), ((), ())), preferred_element_type=F32)
        a_off = a_off + jnp.where(same < g, a, 0.0)
        h = g
    v_bf = v.astype(BF16)
    o = o + jnp.dot(a_off.astype(BF16), v_bf, preferred_element_type=F32)

    nc = n // C
    bc = b.reshape(nc, C, HG_HEAD)
    b_last = bc[:, C - 1:C, :]
    q_in = (qf * jnp.exp2(b)).astype(BF16)
    k_out = (kk.reshape(nc, C, HG_HEAD) * jnp.exp2(b_last - bc)).astype(BF16)
    dec = jnp.exp2(b_last)
    pieces = []
    for c in range(nc):
        rows = slice(c * C, (c + 1) * C)
        o_int = lax.dot_general(q_in[rows], st.astype(BF16), (((1,), (1,)), ((), ())),
                                preferred_element_type=F32)
        pieces.append(o[rows] + o_int)
        upd = lax.dot_general(v_bf[rows], k_out[c], (((0,), (0,)), ((), ())),
                              preferred_element_type=F32)
        st = st * dec[c] + upd
    return jnp.concatenate(pieces, axis=0), st


def _hgrn_body(lbt_ref, ng_ref, q_ref, f_ref, i_ref, g_ref, o_ref, st_ref, *, layer, tt, sub):
    t = pl.program_id(2)

    @pl.when(t == 0)
    def _():
        st_ref[...] = jnp.zeros_like(st_ref)

    lbt = lbt_ref[...]
    e = jnp.exp(lbt - jnp.max(lbt, axis=0, keepdims=True))
    sm = e / jnp.sum(e, axis=0, keepdims=True)
    lb = jnp.sum(sm[1:layer + 2], axis=0, keepdims=True)

    st = st_ref[...]
    for c in range(tt // sub):
        rows = slice(c * sub, (c + 1) * sub)
        f = lb + (1.0 - lb) * jax.nn.sigmoid(f_ref[0, rows, :].astype(F32))
        q = q_ref[0, rows, :].astype(F32)
        qf = q * jax.nn.sigmoid(q) * (HG_HEAD ** -0.5)
        o, st = _hgrn_subtile(qf, 1.0 - f, i_ref[0, rows, :].astype(F32), jnp.log2(f), st)
        ms = jnp.mean(o * o, axis=-1, keepdims=True)
        o = o * lax.rsqrt(ms + EPS) * ng_ref[...]
        g = g_ref[0, rows, :].astype(F32)
        o_ref[0, rows, :] = (o * (g * jax.nn.sigmoid(g))).astype(o_ref.dtype)
    st_ref[...] = st


def _hgrn_mixer(z, lb_table, norm_g, layer, mix_a, mix_b, tt=4096):
    B, T, _ = z.shape
    tt = min(tt, T)
    sub = min(HG_SUB, tt)
    heads = mix_b // HG_HEAD
    c0 = mix_a // HG_HEAD
    est = 2 * 5 * _nbytes((tt, HG_HEAD), F32) + 32 * _nbytes((sub, HG_HEAD), F32) + 6 * _nbytes((sub, sub), F32)

    def col(off):
        return pl.BlockSpec((1, tt, HG_HEAD), lambda b, h, t, off=off: (b, t, c0 + off * heads + h))

    return pl.pallas_call(
        functools.partial(_hgrn_body, layer=layer, tt=tt, sub=sub),
        grid=(B, heads, T // tt),
        in_specs=[pl.BlockSpec((lb_table.shape[0], HG_HEAD), lambda b, h, t: (0, h)),
                  pl.BlockSpec((1, HG_HEAD), lambda b, h, t: (0, 0)),
                  col(0), col(1), col(2), col(3)],
        out_specs=pl.BlockSpec((1, tt, HG_HEAD), lambda b, h, t: (b, t, h)),
        out_shape=jax.ShapeDtypeStruct((B, T, mix_b), BF16),
        scratch_shapes=[pltpu.VMEM((HG_HEAD, HG_HEAD), F32)],
        compiler_params=_params(("parallel", "parallel", "arbitrary"), est),
        name="hgrn2_mixer",
    )(lb_table.astype(F32), norm_g.reshape(1, HG_HEAD).astype(F32), z, z, z, z)


def _fgate_body(h_ref, w_ref, b_ref, *rest, tt, norm_width):
    if norm_width:
        ss_ref, o_ref, carry_ref = rest
    else:
        o_ref, carry_ref = rest
    t = pl.program_id(1)

    @pl.when(t == 0)
    def _():
        carry_ref[...] = jnp.zeros_like(carry_ref)

    fl = jnp.dot(h_ref[0], w_ref[...], preferred_element_type=F32)
    if norm_width:
        fl = fl * _row_rsqrt(ss_ref, norm_width)
    fl = fl + b_ref[...]
    c = (jnp.minimum(fl, 0.0) - jnp.log1p(jnp.exp(-jnp.abs(fl)))) * LOG2E
    row = lax.broadcasted_iota(jnp.int32, c.shape, 0)
    sh = 1
    while sh < tt:
        c = c + jnp.where(row >= sh, pltpu.roll(c, sh, axis=0), 0.0)
        sh *= 2
    c = c + carry_ref[...]
    o_ref[0] = c
    carry_ref[...] = c[tt - 1:tt, :]


def _fox_gates(h, wf, bf, row_ss=None, tt=512):
    B, T, D = h.shape
    tt = min(tt, T)
    ss_specs = [] if row_ss is None else [pl.BlockSpec((row_ss.shape[0], None, tt, 1), lambda b, t: (0, b, t, 0))]
    ss_args = [] if row_ss is None else [row_ss]
    est = 2 * (_nbytes((tt, D), BF16) + _nbytes((D, LANES), BF16) + _nbytes((tt, LANES), F32)) \
        + 8 * _nbytes((tt, LANES), F32)
    return pl.pallas_call(
        functools.partial(_fgate_body, tt=tt, norm_width=0 if row_ss is None else D),
        grid=(B, T // tt),
        in_specs=[pl.BlockSpec((1, tt, D), lambda b, t: (b, t, 0)),
                  pl.BlockSpec((D, LANES), lambda b, t: (0, 0)),
                  pl.BlockSpec((1, LANES), lambda b, t: (0, 0))] + ss_specs,
        out_specs=pl.BlockSpec((1, tt, LANES), lambda b, t: (b, t, 0)),
        out_shape=jax.ShapeDtypeStruct((B, T, LANES), F32),
        scratch_shapes=[pltpu.VMEM((1, LANES), F32)],
        compiler_params=_params(("parallel", "arbitrary"), est),
        name="fox_gates",
    )(h, wf, bf, *ss_args)


def _fox_body(q_ref, k_ref, v_ref, f_ref, fr_ref, o_ref, vt_ref, fk_ref, p_ref, acc_ref, *, tq, seq, group):
    hg = pl.program_id(1)
    i = pl.program_id(2)
    reps = tq // LANES
    heads = range(group)

    def head_cols(g):
        return slice(g * FOX_HEAD, (g + 1) * FOX_HEAD)

    @pl.when(i == 0)
    def _():
        for c in range(seq // tq):
            rows = slice(c * tq, (c + 1) * tq)
            fblk = f_ref[0, rows, :]
            lane = lax.broadcasted_iota(jnp.int32, fblk.shape, 1)
            for g in heads:
                vt_ref[g, :, rows] = v_ref[0, rows, head_cols(g)].T
                col = jnp.sum(jnp.where(lane == hg * group + g, fblk, 0.0), axis=-1, keepdims=True)
                fk_ref[g, rows, :] = jnp.broadcast_to(col, fblk.shape)

    qt = [q_ref[0, :, head_cols(g)].T for g in heads]
    fq = [fr_ref[0, g, i] for g in heads]

    def scores(g, j):
        start = pl.multiple_of(j * tq, tq)
        return jnp.dot(k_ref[0, pl.ds(start, tq), head_cols(g)], qt[g], preferred_element_type=F32)

    def softmax_block(g, j, st, m_prev, l_prev, diagonal):
        start = pl.multiple_of(j * tq, tq)
        t = st - jnp.concatenate([fk_ref[g, pl.ds(start, tq), :]] * reps, axis=1)
        if diagonal:
            key = lax.broadcasted_iota(jnp.int32, (tq, tq), 0)
            qry = lax.broadcasted_iota(jnp.int32, (tq, tq), 1)
            t = jnp.where(key <= qry, t, -jnp.inf)
        m_new = jnp.maximum(m_prev, jnp.max(t, axis=0, keepdims=True) + fq[g])
        alpha = jnp.exp2(m_prev - m_new)
        p = jnp.exp2(t - (m_new - fq[g]))
        l_new = alpha * l_prev + jnp.sum(p, axis=0, keepdims=True)
        return p.astype(BF16), m_new, l_new, alpha

    def weighted_values(g, j, p):
        start = pl.multiple_of(j * tq, tq)
        return jnp.dot(vt_ref[g, :, pl.ds(start, tq)], p, preferred_element_type=F32)

    def block(j, j_prev, stats, first):
        out = []
        for g in heads:
            m, l = stats[g]
            st = scores(g, j)
            if first:
                p, m, l, _ = softmax_block(g, j, st, m, l, True)
                acc_ref[g] = jnp.zeros((FOX_HEAD, tq), F32)
            else:
                pv_prev = weighted_values(g, j_prev, p_ref[g])
                p, m, l, alpha = softmax_block(g, j, st, m, l, False)
                acc_ref[g] = (acc_ref[g] + pv_prev) * alpha
            p_ref[g] = p
            out.append((m, l))
        return tuple(out)

    stats = tuple((jnp.full((1, tq), -jnp.inf, F32), jnp.zeros((1, tq), F32)) for _ in heads)
    stats = block(i, i, stats, True)
    def run(j0, n, c):
        for d in range(n):
            c = block(j0 + d, jnp.where(j0 + d == 0, i, j0 + d - 1), c, False)
        return c

    one, two = i & 1, i & 2
    stats = lax.cond(one == 1, lambda c: run(0, 1, c), lambda c: c, stats)
    stats = lax.cond(two == 2, lambda c: run(one, 2, c), lambda c: c, stats)
    stats = lax.fori_loop(0, i // 4, lambda jj, c: run(one + two + 4 * jj, 4, c), stats)
    j_last = jnp.where(i == 0, i, i - 1)
    for g in heads:
        acc = acc_ref[g] + weighted_values(g, j_last, p_ref[g])
        o_ref[0, :, head_cols(g)] = (acc / stats[g][1]).astype(o_ref.dtype).T


def _fox_attention(qkv, f, frow, heads, tq, group=FOX_GROUP):
    B, T, _ = qkv.shape
    nq = T // tq
    gw = group * FOX_HEAD
    ngrp = heads // group
    est = 2 * (2 * _nbytes((T, gw), BF16) + 2 * _nbytes((tq, gw), BF16)
               + _nbytes((T, LANES), F32) + group * _nbytes((nq, 8, tq), F32)) \
        + group * (_nbytes((T, FOX_HEAD), BF16) + _nbytes((T, LANES), F32) + 5 * _nbytes((tq, tq), F32))
    return pl.pallas_call(
        functools.partial(_fox_body, tq=tq, seq=T, group=group),
        grid=(B, ngrp, nq),
        in_specs=[pl.BlockSpec((1, tq, gw), lambda b, h, i: (b, i, h)),
                  pl.BlockSpec((1, T, gw), lambda b, h, i: (b, 0, ngrp + h)),
                  pl.BlockSpec((1, T, gw), lambda b, h, i: (b, 0, 2 * ngrp + h)),
                  pl.BlockSpec((1, T, LANES), lambda b, h, i: (b, 0, 0)),
                  pl.BlockSpec((1, group, nq, 1, tq), lambda b, h, i: (b, h, 0, 0, 0))],
        out_specs=pl.BlockSpec((1, tq, gw), lambda b, h, i: (b, i, h)),
        out_shape=jax.ShapeDtypeStruct((B, T, heads * FOX_HEAD), BF16),
        scratch_shapes=[pltpu.VMEM((group, FOX_HEAD, T), BF16),
                        pltpu.VMEM((group, T, LANES), F32),
                        pltpu.VMEM((group, tq, tq), BF16),
                        pltpu.VMEM((group, FOX_HEAD, tq), F32)],
        compiler_params=_params(("parallel", "parallel", "arbitrary"), est),
        name="fox_attention",
    )(qkv, qkv, qkv, f, frow)


def _xattn_out_body(q_ref, kv_ref, w_ref, r_ref, g_ref, o_ref, h_ref, a_ref, *, d_model):
    hd = d_model // XA_HEADS
    for hh in range(XA_HEADS):
        q = q_ref[:, hh * hd:(hh + 1) * hd]
        k = kv_ref[:, hh * hd:(hh + 1) * hd]
        v = kv_ref[:, d_model + hh * hd:d_model + (hh + 1) * hd]
        s = lax.dot_general(q, k, (((1,), (1,)), ((), ())), preferred_element_type=F32)
        p = jnp.exp(s - jnp.max(s, axis=-1, keepdims=True))
        p = p / jnp.sum(p, axis=-1, keepdims=True)
        a_ref[:, hh * hd:(hh + 1) * hd] = jnp.dot(p.astype(BF16), v, preferred_element_type=F32).astype(a_ref.dtype)
    acc = r_ref[...] + jnp.dot(a_ref[...], w_ref[...], preferred_element_type=F32)
    o_ref[...] = acc
    ms = jnp.mean(acc * acc, axis=-1, keepdims=True)
    h_ref[...] = (acc * lax.rsqrt(ms + EPS) * g_ref[...]).astype(h_ref.dtype)


def _xattn_out_norm(q, kv, w, layer, r, gain, seq, tm=512):
    M, D = q.shape
    n_mem = kv.shape[1]
    tm = _pick_tile(seq, tm)
    per_batch = seq // tm
    est = 2 * (_nbytes((tm, D), BF16) + _nbytes((n_mem, 2 * D), BF16) + _nbytes((D, D), BF16)
               + 2 * _nbytes((tm, D), F32) + _nbytes((tm, D), BF16)) \
        + _nbytes((tm, D), BF16) + 3 * _nbytes((tm, D), F32)
    row = pl.BlockSpec((tm, D), lambda i: (i, 0))
    return pl.pallas_call(
        functools.partial(_xattn_out_body, d_model=D),
        grid=(M // tm,),
        in_specs=[row,
                  pl.BlockSpec((None, n_mem, 2 * D), lambda i: (i // per_batch, 0, 0)),
                  pl.BlockSpec((None, D, D), lambda i: (layer, 0, 0)),
                  row,
                  pl.BlockSpec((1, D), lambda i: (0, 0))],
        out_specs=[row, row],
        out_shape=[jax.ShapeDtypeStruct((M, D), F32),
                   jax.ShapeDtypeStruct((M, D), BF16)],
        scratch_shapes=[pltpu.VMEM((tm, D), BF16)],
        compiler_params=_params(("parallel",), est),
        name="xattn_out_norm",
    )(q, kv, w, r, gain.reshape(1, D).astype(F32))


def kernel(x, mem, lb_table, ev_norm, ev_w_in, ev_w_pool, ev_pool_scale, ev_hg_norm, ev_w_out,
           od_norm, od_w_in, od_b_f, od_w_out, xa_norm, xa_mem_norm, xa_wq, xa_wkv, xa_wo,
           ffn_norm, ffn_w_gate, ffn_w_up, ffn_w_down, final_norm):
    B, T, D = x.shape
    M = B * T
    depth = xa_norm.shape[0]
    mix_a = ev_pool_scale.shape[1]
    mix_b = lb_table.shape[1]
    fox_heads = od_b_f.shape[1]
    n_mem = mem.shape[1]
    xa_scale = (D // XA_HEADS) ** -0.5
    fox_tq = _pick_tile(T, 512)

    xs = x.reshape(M, D)
    mem2 = mem.reshape(B * n_mem, D)
    w_down = ffn_w_down.astype(BF16)
    ev_wo, od_wo, xa_wo_b = ev_w_out.astype(BF16), od_w_out.astype(BF16), xa_wo.astype(BF16)
    xa_wq_b = xa_wq.astype(BF16)
    xa_qscale = jnp.full((D,), xa_scale, F32)
    pre = None
    for l in range(depth):
        if pre is None:
            h, ss = _rmsnorm(xs, ev_norm[l // 2] if l % 2 == 0 else od_norm[l // 2], BF16), None
        else:
            h, ss = pre
        if l % 2 == 0:
            e = l // 2
            z = _matmul(h, ev_w_in, e, ev_w_in.shape[2], BF16, row_ss=ss).reshape(B, T, -1)
            ya = _pool_mixer(z, ev_w_pool[e].astype(BF16), ev_pool_scale[e], mix_a)
            yb = _hgrn_mixer(z, lb_table, ev_hg_norm[e], l, mix_a, mix_b)
            xs, q = _matmul_residual_norm_proj([ya.reshape(M, mix_a), yb.reshape(M, mix_b)], ev_wo, e, xs, xa_norm[l],
                                               xa_wq_b, l, xa_qscale)
        else:
            o = l // 2
            qscale = jnp.concatenate([jnp.full((D,), LOG2E * FOX_HEAD ** -0.5, F32), jnp.ones((2 * D,), F32)])
            w_in = od_w_in.astype(BF16)
            qkv = _matmul(h, w_in, o, 3 * D, BF16, colscale=qscale, row_ss=ss).reshape(B, T, 3 * D)
            wf = jnp.zeros((D, LANES), BF16).at[:, :fox_heads].set(w_in[o, :, 3 * D:])
            bf = jnp.zeros((1, LANES), F32).at[0, :fox_heads].set(od_b_f[o].astype(F32))
            ss4 = None if ss is None else ss.reshape(ss.shape[0], B, T, 1)
            f = _fox_gates(h.reshape(B, T, D), wf, bf, ss4)
            frow = f[..., :fox_heads].transpose(0, 2, 1).reshape(B, fox_heads, T // fox_tq, 1, fox_tq)
            y = _fox_attention(qkv, f, frow, fox_heads, fox_tq).reshape(M, D)
            xs, q = _matmul_residual_norm_proj([y], od_wo, o, xs, xa_norm[l], xa_wq_b, l, xa_qscale)

        mn = _rmsnorm(mem2, xa_mem_norm[l], BF16)
        kv = _matmul(mn, xa_wkv, l, 2 * D, BF16)
        xs, h = _xattn_out_norm(q, kv.reshape(B, n_mem, 2 * D), xa_wo_b, l, xs, ffn_norm[l], T)

        act = _matmul_swiglu(h, ffn_w_gate, ffn_w_up, l)
        if l + 1 < depth:
            nxt = l + 1
            gain = ev_norm[nxt // 2] if nxt % 2 == 0 else od_norm[nxt // 2]
            xs, hp, ss = _matmul_residual(act, w_down, l, xs, gain=gain)
            pre = (hp, ss)
        else:
            xs = _matmul_residual(act, w_down, l, xs)
    return _rmsnorm(xs, final_norm, x.dtype).reshape(B, T, D)
```

```python
import functools

import jax
import jax.numpy as jnp
from jax import lax
from jax.experimental import pallas as pl
from jax.experimental.pallas import tpu as pltpu

F32 = jnp.float32
BF16 = jnp.bfloat16

EPS = 1e-6
POOL_WINDOWS = (2, 4, 8, 16)
POOL_HALO = 16
assert all(w & (w - 1) == 0 and w <= POOL_HALO for w in POOL_WINDOWS)
HG_HEAD = 128
HG_CHUNK = 64
HG_DIAG = 8
HG_SUB = 128
FOX_HEAD = 128
FOX_GROUP = 4
XA_HEADS = 4

LOG2E = 1.4426950408889634
LANES = 128
VMEM_CAP = 56 * 1024 * 1024
VMEM_FLOOR = 32 * 1024 * 1024


def _params(semantics, vmem_estimate):
    limit = int(min(max(vmem_estimate * 5 // 4, VMEM_FLOOR), VMEM_CAP))
    return pltpu.CompilerParams(dimension_semantics=semantics, vmem_limit_bytes=limit)


def _nbytes(shape, dtype):
    n = jnp.dtype(dtype).itemsize
    for s in shape:
        n *= s
    return n


def _rmsnorm_body(x_ref, g_ref, o_ref):
    x = x_ref[...]
    ms = jnp.mean(x * x, axis=-1, keepdims=True)
    o_ref[...] = (x * lax.rsqrt(ms + EPS) * g_ref[...]).astype(o_ref.dtype)


def _rmsnorm(x, g, out_dtype, tm=512):
    M, D = x.shape
    tm = min(tm, M)
    est = 2 * (_nbytes((tm, D), x.dtype) + _nbytes((tm, D), out_dtype)) + 3 * _nbytes((tm, D), F32)
    return pl.pallas_call(
        _rmsnorm_body,
        grid=(M // tm,),
        in_specs=[pl.BlockSpec((tm, D), lambda i: (i, 0)),
                  pl.BlockSpec((1, D), lambda i: (0, 0))],
        out_specs=pl.BlockSpec((tm, D), lambda i: (i, 0)),
        out_shape=jax.ShapeDtypeStruct((M, D), out_dtype),
        compiler_params=_params(("parallel",), est),
        name="rmsnorm",
    )(x, g.reshape(1, D).astype(F32))


def _with_bf16_weights(w_refs, wb_refs, compute):
    if not wb_refs:
        compute(*(w[...] for w in w_refs))
        return
    i = pl.program_id(1)

    @pl.when(i == 0)
    def _():
        for w, wb in zip(w_refs, wb_refs):
            wb[...] = w[...].astype(BF16)

    @pl.when(i > 0)
    def _():
        compute(*(wb[...] for wb in wb_refs))


def _row_rsqrt(ss_ref, width):
    return lax.rsqrt(jnp.sum(ss_ref[...], axis=0) * (1.0 / width) + EPS)


def _mm_body(x_ref, w_ref, cs_ref, *rest, norm_width):
    if norm_width:
        ss_ref, o_ref, *wb = rest
    else:
        o_ref, *wb = rest

    def compute(w):
        acc = jnp.dot(x_ref[...], w, preferred_element_type=F32)
        if norm_width:
            acc = acc * _row_rsqrt(ss_ref, norm_width)
        o_ref[...] = (acc * cs_ref[...]).astype(o_ref.dtype)

    _with_bf16_weights((w_ref,), wb, compute)


def _mm_res_body(x_ref, w_ref, r_ref, *rest, with_gain):
    if with_gain:
        g_ref, o_ref, hp_ref, ss_ref, *wb = rest
    else:
        o_ref, *wb = rest

    def compute(w):
        acc = r_ref[...] + jnp.dot(x_ref[...], w, preferred_element_type=F32)
        o_ref[...] = acc
        if with_gain:
            hp_ref[...] = (acc * g_ref[...]).astype(hp_ref.dtype)
            ss_ref[...] = jnp.sum(acc * acc, axis=-1, keepdims=True)

    _with_bf16_weights((w_ref,), wb, compute)


def _mm_swiglu_body(x_ref, wg_ref, wu_ref, o_ref, *wb):
    def compute(wg, wu):
        x = x_ref[...]
        g = jnp.dot(x, wg, preferred_element_type=F32)
        u = jnp.dot(x, wu, preferred_element_type=F32)
        o_ref[...] = (g * jax.nn.sigmoid(g) * u).astype(o_ref.dtype)

    _with_bf16_weights((wg_ref, wu_ref), wb, compute)


def _pick_tile(n, pref):
    t = min(pref, n)
    while n % t:
        t //= 2
    return t


def _weight_spec(layer, K, tn):
    return pl.BlockSpec((None, K, tn), lambda j, i: (layer, 0, j))


def _weight_scratch(w, K, tn):
    return [] if w.dtype == BF16 else [pltpu.VMEM((K, tn), BF16)]


def _weight_bytes(w, K, tn):
    return 2 * _nbytes((K, tn), w.dtype) + (0 if w.dtype == BF16 else _nbytes((K, tn), BF16))


def _row_steps(w, M, tm):
    extra = 0 if w.dtype == BF16 else 1
    return M // tm + extra, lambda i: jnp.maximum(i - extra, 0)


def _matmul(x, w, layer, n_cols, out_dtype, colscale=None, row_ss=None, tm=1024, tn=1024):
    M, K = x.shape
    tm, tn = _pick_tile(M, tm), _pick_tile(n_cols, tn)
    if colscale is None:
        colscale = jnp.ones((n_cols,), F32)
    steps, row = _row_steps(w, M, tm)
    ss_specs = [] if row_ss is None else [pl.BlockSpec((row_ss.shape[0], tm, 1), lambda j, i: (0, row(i), 0))]
    ss_args = [] if row_ss is None else [row_ss]
    est = 2 * (_nbytes((tm, K), x.dtype) + _nbytes((tm, tn), out_dtype)) + _weight_bytes(w, K, tn) \
        + 2 * _nbytes((tm, tn), F32)
    return pl.pallas_call(
        functools.partial(_mm_body, norm_width=0 if row_ss is None else K),
        grid=(n_cols // tn, steps),
        in_specs=[pl.BlockSpec((tm, K), lambda j, i: (row(i), 0)),
                  _weight_spec(layer, K, tn),
                  pl.BlockSpec((1, tn), lambda j, i: (0, j))] + ss_specs,
        out_specs=pl.BlockSpec((tm, tn), lambda j, i: (row(i), j)),
        out_shape=jax.ShapeDtypeStruct((M, n_cols), out_dtype),
        scratch_shapes=_weight_scratch(w, K, tn),
        compiler_params=_params(("parallel", "arbitrary"), est),
        name="matmul",
    )(x, w, colscale.reshape(1, n_cols), *ss_args)


def _matmul_residual(x, w, layer, r, gain=None, tn=1024):
    M, K = x.shape
    N = w.shape[2]
    tm = 1024 if K <= 2048 else 512
    tm, tn = _pick_tile(M, tm), _pick_tile(N, tn)
    steps, row = _row_steps(w, M, tm)
    est = 2 * (_nbytes((tm, K), x.dtype) + 2 * _nbytes((tm, tn), F32)) + _weight_bytes(w, K, tn) \
        + 2 * _nbytes((tm, tn), F32)
    tile = pl.BlockSpec((tm, tn), lambda j, i: (row(i), j))
    in_specs = [pl.BlockSpec((tm, K), lambda j, i: (row(i), 0)), _weight_spec(layer, K, tn), tile]
    out_specs, out_shape, args = tile, jax.ShapeDtypeStruct((M, N), F32), [x, w, r]
    if gain is not None:
        in_specs.append(pl.BlockSpec((1, tn), lambda j, i: (0, j)))
        args.append(gain.reshape(1, N).astype(F32))
        out_specs = [tile, tile, pl.BlockSpec((None, tm, 1), lambda j, i: (j, row(i), 0))]
        out_shape = [out_shape, jax.ShapeDtypeStruct((M, N), BF16), jax.ShapeDtypeStruct((N // tn, M, 1), F32)]
    return pl.pallas_call(
        functools.partial(_mm_res_body, with_gain=gain is not None),
        grid=(N // tn, steps),
        in_specs=in_specs,
        out_specs=out_specs,
        out_shape=out_shape,
        scratch_shapes=_weight_scratch(w, K, tn),
        compiler_params=_params(("parallel", "arbitrary"), est),
        name="matmul_residual",
    )(*args)


def _matmul_swiglu(x, wg, wu, layer, tm=1024, tn=512):
    M, K = x.shape
    N = wg.shape[2]
    tm, tn = _pick_tile(M, tm), _pick_tile(N, tn)
    steps, row = _row_steps(wg, M, tm)
    est = 2 * (_nbytes((tm, K), x.dtype) + _nbytes((tm, tn), BF16)) + 2 * _weight_bytes(wg, K, tn) \
        + 4 * _nbytes((tm, tn), F32)
    return pl.pallas_call(
        _mm_swiglu_body,
        grid=(N // tn, steps),
        in_specs=[pl.BlockSpec((tm, K), lambda j, i: (row(i), 0)),
                  _weight_spec(layer, K, tn),
                  _weight_spec(layer, K, tn)],
        out_specs=pl.BlockSpec((tm, tn), lambda j, i: (row(i), j)),
        out_shape=jax.ShapeDtypeStruct((M, N), BF16),
        scratch_shapes=_weight_scratch(wg, K, tn) + _weight_scratch(wu, K, tn),
        compiler_params=_params(("parallel", "arbitrary"), est),
        name="matmul_swiglu",
    )(x, wg, wu)


def _mm_res_norm_proj_body(*refs, n_parts):
    x_refs = refs[:n_parts]
    w_ref, r_ref, g_ref, w2_ref, cs_ref, o_ref, q_ref = refs[n_parts:]
    acc = r_ref[...]
    k0 = 0
    for x_ref in x_refs:
        kw = x_ref.shape[1]
        acc = acc + jnp.dot(x_ref[...], w_ref[k0:k0 + kw, :], preferred_element_type=F32)
        k0 += kw
    o_ref[...] = acc
    ms = jnp.mean(acc * acc, axis=-1, keepdims=True)
    h = (acc * lax.rsqrt(ms + EPS) * g_ref[...]).astype(BF16)
    q_ref[...] = (jnp.dot(h, w2_ref[...], preferred_element_type=F32) * cs_ref[...]).astype(q_ref.dtype)


def _matmul_residual_norm_proj(x_parts, w, layer, r, gain, w2, layer2, colscale, tm=512):
    M, N = r.shape
    K = w.shape[1]
    N2 = w2.shape[2]
    tm = _pick_tile(M, tm)
    est = 2 * (_nbytes((tm, K), BF16) + 2 * _nbytes((tm, N), F32) + _nbytes((tm, N2), BF16)) \
        + _nbytes((K, N), BF16) + _nbytes((N, N2), BF16) + 4 * _nbytes((tm, N), F32)
    resident = pl.Buffered(1)
    return pl.pallas_call(
        functools.partial(_mm_res_norm_proj_body, n_parts=len(x_parts)),
        grid=(M // tm,),
        in_specs=[pl.BlockSpec((tm, xp.shape[1]), lambda i: (i, 0)) for xp in x_parts]
        + [pl.BlockSpec((None, K, N), lambda i: (layer, 0, 0), pipeline_mode=resident),
           pl.BlockSpec((tm, N), lambda i: (i, 0)),
           pl.BlockSpec((1, N), lambda i: (0, 0)),
           pl.BlockSpec((None, N, N2), lambda i: (layer2, 0, 0), pipeline_mode=resident),
           pl.BlockSpec((1, N2), lambda i: (0, 0))],
        out_specs=[pl.BlockSpec((tm, N), lambda i: (i, 0)),
                   pl.BlockSpec((tm, N2), lambda i: (i, 0))],
        out_shape=[jax.ShapeDtypeStruct((M, N), F32),
                   jax.ShapeDtypeStruct((M, N2), BF16)],
        compiler_params=_params(("parallel",), est),
        name="matmul_residual_norm_proj",
    )(*x_parts, w, r, gain.reshape(1, N).astype(F32), w2, colscale.reshape(1, N2).astype(F32))


def _pool_body(u_ref, halo_ref, w_ref, sc_ref, o_ref, ext_ref, *, tt, group):
    t = pl.program_id(1)
    u = u_ref[0].astype(F32)
    ext_ref[0:POOL_HALO, :] = jnp.where(t > 0, halo_ref[0].astype(F32), 0.0)
    ext_ref[POOL_HALO:POOL_HALO + tt, :] = u
    pos = t * tt + lax.broadcasted_iota(jnp.int32, (tt, 1), 0)
    for gi, win in enumerate(POOL_WINDOWS):
        c0, c1 = gi * group, (gi + 1) * group
        tok = u[:, c0:c1]
        lvl = ext_ref[:, c0:c1]
        k = 1
        while k < win:
            lvl = lvl + pltpu.roll(lvl, k, axis=0)
            k *= 2
        acc = lvl[POOL_HALO:POOL_HALO + tt]
        cnt = jnp.minimum(pos + 1, win).astype(F32)
        p = acc / cnt - tok
        y = jnp.dot(p.astype(BF16), w_ref[gi], preferred_element_type=F32)
        o_ref[0, :, c0:c1] = (y * sc_ref[:, c0:c1]).astype(o_ref.dtype)


def _pool_mixer(z, w_pool, scale, mix_a, tt=1024):
    B, T, _ = z.shape
    tt = min(tt, T)
    group = mix_a // len(POOL_WINDOWS)
    halo_blocks = tt // POOL_HALO
    est = 2 * (_nbytes((tt, mix_a), F32) + _nbytes((tt, mix_a), BF16)) + 4 * _nbytes((tt, mix_a), F32)
    return pl.pallas_call(
        functools.partial(_pool_body, tt=tt, group=group),
        grid=(B, T // tt),
        in_specs=[pl.BlockSpec((1, tt, mix_a), lambda b, t: (b, t, 0)),
                  pl.BlockSpec((1, POOL_HALO, mix_a),
                               lambda b, t: (b, jnp.maximum(t * halo_blocks - 1, 0), 0)),
                  pl.BlockSpec((len(POOL_WINDOWS), group, group), lambda b, t: (0, 0, 0)),
                  pl.BlockSpec((1, mix_a), lambda b, t: (0, 0))],
        out_specs=pl.BlockSpec((1, tt, mix_a), lambda b, t: (b, t, 0)),
        out_shape=jax.ShapeDtypeStruct((B, T, mix_a), BF16),
        scratch_shapes=[pltpu.VMEM((tt + POOL_HALO, mix_a), F32)],
        compiler_params=_params(("parallel", "parallel"), est),
        name="pool_mixer",
    )(z, z, w_pool, scale.reshape(1, mix_a).astype(F32))


def _hgrn_subtile(qf, kk, v, logf, st):
    n = qf.shape[0]
    C, R = HG_CHUNK, HG_DIAG
    row = lax.broadcasted_iota(jnp.int32, (n, HG_HEAD), 0)

    pos = row & (C - 1)
    b = logf
    sh = 1
    while sh < C:
        b = b + jnp.where(pos >= sh, pltpu.roll(b, sh, axis=0), 0.0)
        sh *= 2

    nb = n // R
    b3, q3, c3, v3 = (a.reshape(nb, R, HG_HEAD) for a in (b, qf, b - jnp.log2(kk), v))
    tpos = lax.broadcasted_iota(jnp.int32, (1, R, 1), 1)
    od = jnp.zeros((nb, R, HG_HEAD), F32)
    for s in range(R):
        w = q3 * jnp.exp2(b3 - c3[:, s:s + 1, :])
        a = jnp.sum(w, axis=-1, keepdims=True)
        a = jnp.where(tpos >= s, a, 0.0)
        od = od + a * v3[:, s:s + 1, :]
    o = od.reshape(n, HG_HEAD)

    same = lax.broadcasted_iota(jnp.int32, (n, n), 0) ^ lax.broadcasted_iota(jnp.int32, (n, n), 1)
    a_off = jnp.zeros((n, n), F32)
    h = R
    while h < C:
        g = 2 * h
        ref = b.reshape(n // g, g, HG_HEAD)[:, h - 1:h, :]
        ref = jnp.broadcast_to(ref, (n // g, g, HG_HEAD)).reshape(n, HG_HEAD)
        e = jnp.exp2(-jnp.abs(b - ref))
        right = (row & (g - 1)) >= h
        qt = jnp.where(right, qf * e, 0.0).astype(BF16)
        kt = jnp.where(right, 0.0, kk * e).astype(BF16)
        a = lax.dot_general(qt, kt, (((1,), (1,)), ((), ())), preferred_element_type=F32)
        a_off = a_off + jnp.where(same < g, a, 0.0)
        h = g
    v_bf = v.astype(BF16)
    o = o + jnp.dot(a_off.astype(BF16), v_bf, preferred_element_type=F32)

    nc = n // C
    bc = b.reshape(nc, C, HG_HEAD)
    b_last = bc[:, C - 1:C, :]
    q_in = (qf * jnp.exp2(b)).astype(BF16)
    k_out = (kk.reshape(nc, C, HG_HEAD) * jnp.exp2(b_last - bc)).astype(BF16)
    dec = jnp.exp2(b_last)
    pieces = []
    for c in range(nc):
        rows = slice(c * C, (c + 1) * C)
        o_int = lax.dot_general(q_in[rows], st.astype(BF16), (((1,), (1,)), ((), ())),
                                preferred_element_type=F32)
        pieces.append(o[rows] + o_int)
        upd = lax.dot_general(v_bf[rows], k_out[c], (((0,), (0,)), ((), ())),
                              preferred_element_type=F32)
        st = st * dec[c] + upd
    return jnp.concatenate(pieces, axis=0), st


def _hgrn_body(lbt_ref, ng_ref, q_ref, f_ref, i_ref, g_ref, o_ref, st_ref, *, layer, tt, sub):
    t = pl.program_id(2)

    @pl.when(t == 0)
    def _():
        st_ref[...] = jnp.zeros_like(st_ref)

    lbt = lbt_ref[...]
    e = jnp.exp(lbt - jnp.max(lbt, axis=0, keepdims=True))
    sm = e / jnp.sum(e, axis=0, keepdims=True)
    lb = jnp.sum(sm[1:layer + 2], axis=0, keepdims=True)

    st = st_ref[...]
    for c in range(tt // sub):
        rows = slice(c * sub, (c + 1) * sub)
        f = lb + (1.0 - lb) * jax.nn.sigmoid(f_ref[0, rows, :].astype(F32))
        q = q_ref[0, rows, :].astype(F32)
        qf = q * jax.nn.sigmoid(q) * (HG_HEAD ** -0.5)
        o, st = _hgrn_subtile(qf, 1.0 - f, i_ref[0, rows, :].astype(F32), jnp.log2(f), st)
        ms = jnp.mean(o * o, axis=-1, keepdims=True)
        o = o * lax.rsqrt(ms + EPS) * ng_ref[...]
        g = g_ref[0, rows, :].astype(F32)
        o_ref[0, rows, :] = (o * (g * jax.nn.sigmoid(g))).astype(o_ref.dtype)
    st_ref[...] = st


def _hgrn_mixer(z, lb_table, norm_g, layer, mix_a, mix_b, tt=4096):
    B, T, _ = z.shape
    tt = min(tt, T)
    sub = min(HG_SUB, tt)
    heads = mix_b // HG_HEAD
    c0 = mix_a // HG_HEAD
    est = 2 * 5 * _nbytes((tt, HG_HEAD), F32) + 32 * _nbytes((sub, HG_HEAD), F32) + 6 * _nbytes((sub, sub), F32)

    def col(off):
        return pl.BlockSpec((1, tt, HG_HEAD), lambda b, h, t, off=off: (b, t, c0 + off * heads + h))

    return pl.pallas_call(
        functools.partial(_hgrn_body, layer=layer, tt=tt, sub=sub),
        grid=(B, heads, T // tt),
        in_specs=[pl.BlockSpec((lb_table.shape[0], HG_HEAD), lambda b, h, t: (0, h)),
                  pl.BlockSpec((1, HG_HEAD), lambda b, h, t: (0, 0)),
                  col(0), col(1), col(2), col(3)],
        out_specs=pl.BlockSpec((1, tt, HG_HEAD), lambda b, h, t: (b, t, h)),
        out_shape=jax.ShapeDtypeStruct((B, T, mix_b), BF16),
        scratch_shapes=[pltpu.VMEM((HG_HEAD, HG_HEAD), F32)],
        compiler_params=_params(("parallel", "parallel", "arbitrary"), est),
        name="hgrn2_mixer",
    )(lb_table.astype(F32), norm_g.reshape(1, HG_HEAD).astype(F32), z, z, z, z)


def _fgate_body(h_ref, w_ref, b_ref, *rest, tt, norm_width):
    if norm_width:
        ss_ref, o_ref, carry_ref = rest
    else:
        o_ref, carry_ref = rest
    t = pl.program_id(1)

    @pl.when(t == 0)
    def _():
        carry_ref[...] = jnp.zeros_like(carry_ref)

    fl = jnp.dot(h_ref[0], w_ref[...], preferred_element_type=F32)
    if norm_width:
        fl = fl * _row_rsqrt(ss_ref, norm_width)
    fl = fl + b_ref[...]
    c = (jnp.minimum(fl, 0.0) - jnp.log1p(jnp.exp(-jnp.abs(fl)))) * LOG2E
    row = lax.broadcasted_iota(jnp.int32, c.shape, 0)
    sh = 1
    while sh < tt:
        c = c + jnp.where(row >= sh, pltpu.roll(c, sh, axis=0), 0.0)
        sh *= 2
    c = c + carry_ref[...]
    o_ref[0] = c
    carry_ref[...] = c[tt - 1:tt, :]


def _fox_gates(h, wf, bf, row_ss=None, tt=512):
    B, T, D = h.shape
    tt = min(tt, T)
    ss_specs = [] if row_ss is None else [pl.BlockSpec((row_ss.shape[0], None, tt, 1), lambda b, t: (0, b, t, 0))]
    ss_args = [] if row_ss is None else [row_ss]
    est = 2 * (_nbytes((tt, D), BF16) + _nbytes((D, LANES), BF16) + _nbytes((tt, LANES), F32)) \
        + 8 * _nbytes((tt, LANES), F32)
    return pl.pallas_call(
        functools.partial(_fgate_body, tt=tt, norm_width=0 if row_ss is None else D),
        grid=(B, T // tt),
        in_specs=[pl.BlockSpec((1, tt, D), lambda b, t: (b, t, 0)),
                  pl.BlockSpec((D, LANES), lambda b, t: (0, 0)),
                  pl.BlockSpec((1, LANES), lambda b, t: (0, 0))] + ss_specs,
        out_specs=pl.BlockSpec((1, tt, LANES), lambda b, t: (b, t, 0)),
        out_shape=jax.ShapeDtypeStruct((B, T, LANES), F32),
        scratch_shapes=[pltpu.VMEM((1, LANES), F32)],
        compiler_params=_params(("parallel", "arbitrary"), est),
        name="fox_gates",
    )(h, wf, bf, *ss_args)


def _fox_body(q_ref, k_ref, v_ref, f_ref, fr_ref, o_ref, vt_ref, fk_ref, p_ref, acc_ref, *, tq, seq, group):
    hg = pl.program_id(1)
    i = pl.program_id(2)
    heads = range(group)

    def head_cols(g):
        return slice(g * FOX_HEAD, (g + 1) * FOX_HEAD)

    @pl.when(i == 0)
    def _():
        for c in range(seq // tq):
            rows = slice(c * tq, (c + 1) * tq)
            fblk = f_ref[0, rows, :]
            lane = lax.broadcasted_iota(jnp.int32, fblk.shape, 1)
            for g in heads:
                vt_ref[g, :, rows] = v_ref[0, rows, head_cols(g)].T
                col = jnp.sum(jnp.where(lane == hg * group + g, fblk, 0.0), axis=-1, keepdims=True)
                fk_ref[g, rows, :] = jnp.broadcast_to(col, fblk.shape)

    qt = [q_ref[0, :, head_cols(g)].T for g in heads]
    fq = [fr_ref[0, g, i] for g in heads]

    halves = [slice(c * (tq // 2), (c + 1) * (tq // 2)) for c in range(2)]

    def scores(g, j, qs):
        start = pl.multiple_of(j * tq, tq)
        return jnp.dot(k_ref[0, pl.ds(start, tq), head_cols(g)], qt[g][:, qs], preferred_element_type=F32)

    def softmax_block(g, j, st, m_prev, l_prev, diagonal, qs):
        start = pl.multiple_of(j * tq, tq)
        width = st.shape[1]
        t = st - jnp.concatenate([fk_ref[g, pl.ds(start, tq), :]] * (width // LANES), axis=1)
        if diagonal:
            key = lax.broadcasted_iota(jnp.int32, (tq, width), 0)
            qry = lax.broadcasted_iota(jnp.int32, (tq, width), 1) + qs.start
            t = jnp.where(key <= qry, t, -jnp.inf)
        fqh = fq[g][:, qs]
        m_new = jnp.maximum(m_prev, jnp.max(t, axis=0, keepdims=True) + fqh)
        alpha = jnp.exp2(m_prev - m_new)
        p = jnp.exp2(t - (m_new - fqh))
        l_new = alpha * l_prev + jnp.sum(p, axis=0, keepdims=True)
        return p.astype(BF16), m_new, l_new, alpha

    def weighted_values(g, j, p):
        start = pl.multiple_of(j * tq, tq)
        return jnp.dot(vt_ref[g, :, pl.ds(start, tq)], p, preferred_element_type=F32)

    def block(j, j_prev, stats, first):
        out = []
        for g in heads:
            m, l = stats[g]
            if not first:
                pv_prev = weighted_values(g, j_prev, p_ref[g])
            ms, ls, alphas = [], [], []
            for qs in halves:
                p, m_h, l_h, alpha = softmax_block(g, j, scores(g, j, qs), m[:, qs], l[:, qs], first, qs)
                p_ref[g, :, qs] = p
                ms.append(m_h)
                ls.append(l_h)
                alphas.append(alpha)
            if first:
                acc_ref[g] = jnp.zeros((FOX_HEAD, tq), F32)
            else:
                acc_ref[g] = (acc_ref[g] + pv_prev) * jnp.concatenate(alphas, axis=1)
            out.append((jnp.concatenate(ms, axis=1), jnp.concatenate(ls, axis=1)))
        return tuple(out)

    stats = tuple((jnp.full((1, tq), -jnp.inf, F32), jnp.zeros((1, tq), F32)) for _ in heads)
    stats = block(i, i, stats, True)
    def run(j0, n, c):
        for d in range(n):
            c = block(j0 + d, jnp.where(j0 + d == 0, i, j0 + d - 1), c, False)
        return c

    one, two = i & 1, i & 2
    stats = lax.cond(one == 1, lambda c: run(0, 1, c), lambda c: c, stats)
    stats = lax.cond(two == 2, lambda c: run(one, 2, c), lambda c: c, stats)
    stats = lax.fori_loop(0, i // 4, lambda jj, c: run(one + two + 4 * jj, 4, c), stats)
    j_last = jnp.where(i == 0, i, i - 1)
    for g in heads:
        acc = acc_ref[g] + weighted_values(g, j_last, p_ref[g])
        o_ref[0, :, head_cols(g)] = (acc / stats[g][1]).astype(o_ref.dtype).T


def _fox_attention(qkv, f, frow, heads, tq, group=FOX_GROUP):
    B, T, _ = qkv.shape
    nq = T // tq
    gw = group * FOX_HEAD
    ngrp = heads // group
    est = 2 * (2 * _nbytes((T, gw), BF16) + 2 * _nbytes((tq, gw), BF16)
               + _nbytes((T, LANES), F32) + group * _nbytes((nq, 8, tq), F32)) \
        + group * (_nbytes((T, FOX_HEAD), BF16) + _nbytes((T, LANES), F32) + 5 * _nbytes((tq, tq), F32))
    return pl.pallas_call(
        functools.partial(_fox_body, tq=tq, seq=T, group=group),
        grid=(B, ngrp, nq),
        in_specs=[pl.BlockSpec((1, tq, gw), lambda b, h, i: (b, i, h)),
                  pl.BlockSpec((1, T, gw), lambda b, h, i: (b, 0, ngrp + h)),
                  pl.BlockSpec((1, T, gw), lambda b, h, i: (b, 0, 2 * ngrp + h)),
                  pl.BlockSpec((1, T, LANES), lambda b, h, i: (b, 0, 0)),
                  pl.BlockSpec((1, group, nq, 1, tq), lambda b, h, i: (b, h, 0, 0, 0))],
        out_specs=pl.BlockSpec((1, tq, gw), lambda b, h, i: (b, i, h)),
        out_shape=jax.ShapeDtypeStruct((B, T, heads * FOX_HEAD), BF16),
        scratch_shapes=[pltpu.VMEM((group, FOX_HEAD, T), BF16),
                        pltpu.VMEM((group, T, LANES), F32),
                        pltpu.VMEM((group, tq, tq), BF16),
                        pltpu.VMEM((group, FOX_HEAD, tq), F32)],
        compiler_params=_params(("parallel", "parallel", "arbitrary"), est),
        name="fox_attention",
    )(qkv, qkv, qkv, f, frow)


def _xattn_out_body(q_ref, kv_ref, w_ref, r_ref, g_ref, o_ref, h_ref, a_ref, *, d_model):
    hd = d_model // XA_HEADS
    for hh in range(XA_HEADS):
        q = q_ref[:, hh * hd:(hh + 1) * hd]
        k = kv_ref[:, hh * hd:(hh + 1) * hd]
        v = kv_ref[:, d_model + hh * hd:d_model + (hh + 1) * hd]
        s = lax.dot_general(q, k, (((1,), (1,)), ((), ())), preferred_element_type=F32)
        p = jnp.exp(s - jnp.max(s, axis=-1, keepdims=True))
        p = p / jnp.sum(p, axis=-1, keepdims=True)
        a_ref[:, hh * hd:(hh + 1) * hd] = jnp.dot(p.astype(BF16), v, preferred_element_type=F32).astype(a_ref.dtype)
    acc = r_ref[...] + jnp.dot(a_ref[...], w_ref[...], preferred_element_type=F32)
    o_ref[...] = acc
    ms = jnp.mean(acc * acc, axis=-1, keepdims=True)
    h_ref[...] = (acc * lax.rsqrt(ms + EPS) * g_ref[...]).astype(h_ref.dtype)


def _xattn_out_norm(q, kv, w, layer, r, gain, seq, tm=512):
    M, D = q.shape
    n_mem = kv.shape[1]
    tm = _pick_tile(seq, tm)
    per_batch = seq // tm
    est = 2 * (_nbytes((tm, D), BF16) + _nbytes((n_mem, 2 * D), BF16) + _nbytes((D, D), BF16)
               + 2 * _nbytes((tm, D), F32) + _nbytes((tm, D), BF16)) \
        + _nbytes((tm, D), BF16) + 3 * _nbytes((tm, D), F32)
    row = pl.BlockSpec((tm, D), lambda i: (i, 0))
    return pl.pallas_call(
        functools.partial(_xattn_out_body, d_model=D),
        grid=(M // tm,),
        in_specs=[row,
                  pl.BlockSpec((None, n_mem, 2 * D), lambda i: (i // per_batch, 0, 0)),
                  pl.BlockSpec((None, D, D), lambda i: (layer, 0, 0)),
                  row,
                  pl.BlockSpec((1, D), lambda i: (0, 0))],
        out_specs=[row, row],
        out_shape=[jax.ShapeDtypeStruct((M, D), F32),
                   jax.ShapeDtypeStruct((M, D), BF16)],
        scratch_shapes=[pltpu.VMEM((tm, D), BF16)],
        compiler_params=_params(("parallel",), est),
        name="xattn_out_norm",
    )(q, kv, w, r, gain.reshape(1, D).astype(F32))


def kernel(x, mem, lb_table, ev_norm, ev_w_in, ev_w_pool, ev_pool_scale, ev_hg_norm, ev_w_out,
           od_norm, od_w_in, od_b_f, od_w_out, xa_norm, xa_mem_norm, xa_wq, xa_wkv, xa_wo,
           ffn_norm, ffn_w_gate, ffn_w_up, ffn_w_down, final_norm):
    B, T, D = x.shape
    M = B * T
    depth = xa_norm.shape[0]
    mix_a = ev_pool_scale.shape[1]
    mix_b = lb_table.shape[1]
    fox_heads = od_b_f.shape[1]
    n_mem = mem.shape[1]
    xa_scale = (D // XA_HEADS) ** -0.5
    fox_tq = _pick_tile(T, 512)

    xs = x.reshape(M, D)
    mem2 = mem.reshape(B * n_mem, D)
    w_down = ffn_w_down.astype(BF16)
    ev_wo, od_wo, xa_wo_b = ev_w_out.astype(BF16), od_w_out.astype(BF16), xa_wo.astype(BF16)
    xa_wq_b = xa_wq.astype(BF16)
    xa_qscale = jnp.full((D,), xa_scale, F32)
    pre = None
    for l in range(depth):
        if pre is None:
            h, ss = _rmsnorm(xs, ev_norm[l // 2] if l % 2 == 0 else od_norm[l // 2], BF16), None
        else:
            h, ss = pre
        if l % 2 == 0:
            e = l // 2
            z = _matmul(h, ev_w_in, e, ev_w_in.shape[2], BF16, row_ss=ss).reshape(B, T, -1)
            ya = _pool_mixer(z, ev_w_pool[e].astype(BF16), ev_pool_scale[e], mix_a)
            yb = _hgrn_mixer(z, lb_table, ev_hg_norm[e], l, mix_a, mix_b)
            xs, q = _matmul_residual_norm_proj([ya.reshape(M, mix_a), yb.reshape(M, mix_b)], ev_wo, e, xs, xa_norm[l],
                                               xa_wq_b, l, xa_qscale)
        else:
            o = l // 2
            qscale = jnp.concatenate([jnp.full((D,), LOG2E * FOX_HEAD ** -0.5, F32), jnp.ones((2 * D,), F32)])
            w_in = od_w_in.astype(BF16)
            qkv = _matmul(h, w_in, o, 3 * D, BF16, colscale=qscale, row_ss=ss).reshape(B, T, 3 * D)
            wf = jnp.zeros((D, LANES), BF16).at[:, :fox_heads].set(w_in[o, :, 3 * D:])
            bf = jnp.zeros((1, LANES), F32).at[0, :fox_heads].set(od_b_f[o].astype(F32))
            ss4 = None if ss is None else ss.reshape(ss.shape[0], B, T, 1)
            f = _fox_gates(h.reshape(B, T, D), wf, bf, ss4)
            frow = f[..., :fox_heads].transpose(0, 2, 1).reshape(B, fox_heads, T // fox_tq, 1, fox_tq)
            y = _fox_attention(qkv, f, frow, fox_heads, fox_tq).reshape(M, D)
            xs, q = _matmul_residual_norm_proj([y], od_wo, o, xs, xa_norm[l], xa_wq_b, l, xa_qscale)

        mn = _rmsnorm(mem2, xa_mem_norm[l], BF16)
        kv = _matmul(mn, xa_wkv, l, 2 * D, BF16)
        xs, h = _xattn_out_norm(q, kv.reshape(B, n_mem, 2 * D), xa_wo_b, l, xs, ffn_norm[l], T)

        act = _matmul_swiglu(h, ffn_w_gate, ffn_w_up, l)
        if l + 1 < depth:
            nxt = l + 1
            gain = ev_norm[nxt // 2] if nxt % 2 == 0 else od_norm[nxt // 2]
            xs, hp, ss = _matmul_residual(act, w_down, l, xs, gain=gain)
            pre = (hp, ss)
        else:
            xs = _matmul_residual(act, w_down, l, xs)
    return _rmsnorm(xs, final_norm, x.dtype).reshape(B, T, D)
```
